```python
import math
import jax, jax.numpy as jnp
from jax import lax
import numpy as np

D_MODEL = 1024
BATCH = 4
SEQ = 4096
DEPTH = 2

D_MIX = D_MODEL
HEAD_DIM = 64
RET_HEADS = 4
RET_WIDTH = RET_HEADS * HEAD_DIM
RET_CHUNK = 128
ROPE_BASE = 10000.0
NSA_HEADS = 8
NSA_KV_HEADS = 2
NSA_GROUP = NSA_HEADS // NSA_KV_HEADS
NSA_WIDTH = NSA_HEADS * HEAD_DIM
NSA_KV_WIDTH = NSA_KV_HEADS * HEAD_DIM
CMP_LEN = 32
CMP_STRIDE = 16
SEL_LEN = 64
N_SELECT = 16
WINDOW = 512
NSA_QBLOCK = 64
N_BRANCH = 3
FORCE_SCORE = 1.0e4
LRU_WIDTH = D_MIX - RET_WIDTH - NSA_WIDTH
LRU_BLOCKS = 4
LRU_BLOCK_DIM = LRU_WIDTH // LRU_BLOCKS
CONV_WIDTH = 4
LRU_C = 8.0
DEEPNORM_ALPHA = (2.0 * DEPTH) ** 0.25
DEEPNORM_BETA = (8.0 * DEPTH) ** -0.25
LN_EPS = 1e-5

IN_SIZES = (RET_WIDTH,) * 4 + (NSA_WIDTH,) + (NSA_KV_WIDTH,) * 6 + (NSA_WIDTH, NSA_HEADS * N_BRANCH, LRU_WIDTH, LRU_WIDTH)
D_IN = sum(IN_SIZES)
IN_SPLIT_POINTS = tuple(int(v) for v in np.cumsum(IN_SIZES)[:-1])

kernel_name = "hybrid_retention_nsa_rglru_deepnorm"


def _normalize(x):
    xf = x.astype(jnp.float32)
    mu = jnp.mean(xf, axis=-1, keepdims=True)
    var = jnp.mean(jnp.square(xf - mu), axis=-1, keepdims=True)
    return (xf - mu) * lax.rsqrt(var + LN_EPS)


def layer_norm(x, g, b):
    return (_normalize(x) * g.astype(jnp.float32) + b.astype(jnp.float32)).astype(x.dtype)


def rotary(x, pos):
    half = x.shape[-1] // 2
    inv = 1.0 / (ROPE_BASE ** (jnp.arange(half, dtype=jnp.float32) / half))
    ang = pos.astype(jnp.float32)[:, None] * inv[None, :]
    cos, sin = jnp.cos(ang), jnp.sin(ang)
    x1, x2 = x[..., :half], x[..., half:]
    return jnp.concatenate([x1 * cos - x2 * sin, x1 * sin + x2 * cos], axis=-1).astype(x.dtype)


def masked_softmax(s, mask):
    logits = jnp.where(mask, s.astype(jnp.float32), -1e30)
    return jax.nn.softmax(logits, axis=-1)


def retention(q, k, v):
    B, S, _ = q.shape
    dt = q.dtype
    H, d, C = RET_HEADS, HEAD_DIM, RET_CHUNK
    nc = S // C

    def heads(t):
        return t.reshape(B, S, H, d).transpose(0, 2, 1, 3)

    pos = jnp.arange(S)
    qh = rotary(heads(q), pos)
    kh = rotary(heads(k), pos) * (d ** -0.5)
    vh = heads(v)
    log_g = jnp.log(1.0 - 2.0 ** (-5.0 - jnp.arange(H, dtype=jnp.float32)))
    idx = jnp.arange(C, dtype=jnp.float32)
    diff = idx[:, None] - idx[None, :]
    inner_decay = jnp.where(diff >= 0, jnp.exp(jnp.maximum(diff, 0.0)[None] * log_g[:, None, None]), 0.0).astype(dt)
    key_decay = jnp.exp((C - 1.0 - idx)[None, :] * log_g[:, None]).astype(dt)
    query_decay = jnp.exp((idx + 1.0)[None, :] * log_g[:, None]).astype(dt)
    chunk_decay = jnp.exp(C * log_g).astype(dt)

    qc = qh.reshape(B, H, nc, C, d)
    kc = kh.reshape(B, H, nc, C, d)
    vc = vh.reshape(B, H, nc, C, d)
    scores = jnp.einsum('bhnid,bhnjd->bhnij', qc, kc) * inner_decay[None, :, None]
    inner = jnp.einsum('bhnij,bhnjd->bhnid', scores, vc)
    kv = jnp.einsum('bhnjd,bhnje->nbhde', kc * key_decay[None, :, None, :, None], vc)

    def step(state, kv_c):
        return state * chunk_decay[None, :, None, None] + kv_c, state

    _, states = lax.scan(step, jnp.zeros((B, H, d, d), dt), kv)
    cross = jnp.einsum('bhnid,nbhde->bhnie', qc, states) * query_decay[None, :, None, :, None]
    y = (inner + cross).reshape(B, H, S, d)
    y = _normalize(y).astype(dt)
    return y.transpose(0, 2, 1, 3).reshape(B, S, RET_WIDTH)


def nsa(xq, k_c, v_c, k_s, v_s, k_w, v_w, gate_logits, cmp_pos, cmp_w):
    B, S, _ = xq.shape
    dt = xq.dtype
    G, R, dh, QB = NSA_KV_HEADS, NSA_GROUP, HEAD_DIM, NSA_QBLOCK
    q = xq.reshape(B, S, G, R, dh).transpose(0, 2, 3, 1, 4) * (dh ** -0.5)

    def kvh(t):
        return t.reshape(B, S, G, dh).transpose(0, 2, 1, 3)

    n_cmp = (S - CMP_LEN) // CMP_STRIDE + 1
    tok = np.arange(n_cmp)[:, None] * CMP_STRIDE + np.arange(CMP_LEN)[None, :]

    def compress(t, pos, w):
        blocks = t[:, :, tok] + pos
        return blocks.reshape(B, G, n_cmp, CMP_LEN * dh) @ w

    kc = compress(kvh(k_c), cmp_pos[0], cmp_w[0])
    vc = compress(kvh(v_c), cmp_pos[1], cmp_w[1])
    cmp_end = np.arange(n_cmp) * CMP_STRIDE + CMP_LEN - 1

    n_sel = S // SEL_LEN
    ci = np.arange(n_cmp)[:, None]
    sj = np.arange(n_sel)[None, :]
    overlap = np.minimum(ci * CMP_STRIDE + CMP_LEN, sj * SEL_LEN + SEL_LEN) - np.maximum(ci * CMP_STRIDE, sj * SEL_LEN)
    sel_map = jnp.asarray(np.clip(overlap, 0, None) / CMP_STRIDE, dtype=jnp.float32)
    top = min(N_SELECT, n_sel)
    ks_blocks = kvh(k_s).reshape(B, G, n_sel, SEL_LEN, dh)
    vs_blocks = kvh(v_s).reshape(B, G, n_sel, SEL_LEN, dh)
    gather = jax.vmap(jax.vmap(lambda blk, ix: blk[ix]))

    pad = ((0, 0), (0, 0), (WINDOW, 0), (0, 0))
    kw = jnp.pad(kvh(k_w), pad)
    vw = jnp.pad(kvh(v_w), pad)

    gates = jax.nn.sigmoid(gate_logits.astype(jnp.float32)).astype(dt)
    gates = gates.reshape(B, S, G, R, N_BRANCH).transpose(0, 2, 3, 1, 4)

    def block(i):
        start = i * QB
        t = start + jnp.arange(QB)
        qb = lax.dynamic_slice_in_dim(q, start, QB, axis=3)
        gb = lax.dynamic_slice_in_dim(gates, start, QB, axis=3)
        m_c = cmp_end[None, :] <= t[:, None]
        p_c = masked_softmax(jnp.einsum('bgrqd,bgnd->bgrqn', qb, kc), m_c)
        p_c = p_c * jnp.any(m_c, axis=-1)[:, None]
        o_c = jnp.einsum('bgrqn,bgnd->bgrqd', p_c.astype(dt), vc)
        imp = jnp.einsum('bgrqn,nj->bgqj', p_c, sel_map)
        j = jnp.arange(n_sel)[None, :]
        cur = (t // SEL_LEN)[:, None]
        forced = (j == 0) | (j == cur) | (j == cur - 1)
        causal = j * SEL_LEN <= t[:, None]
        imp = jnp.where(forced, FORCE_SCORE, jnp.where(causal, imp, -FORCE_SCORE))
        _, idx = lax.top_k(imp, top)
        k_sel = gather(ks_blocks, idx).reshape(B, G, QB, top * SEL_LEN, dh)
        v_sel = gather(vs_blocks, idx).reshape(B, G, QB, top * SEL_LEN, dh)
        kpos = (idx[..., None] * SEL_LEN + jnp.arange(SEL_LEN)).reshape(B, G, QB, top * SEL_LEN)
        m_s = (kpos <= t[:, None])[:, :, None]
        p_s = masked_softmax(jnp.einsum('bgrqd,bgqkd->bgrqk', qb, k_sel), m_s)
        o_s = jnp.einsum('bgrqk,bgqkd->bgrqd', p_s.astype(dt), v_sel)
        kwb = lax.dynamic_slice_in_dim(kw, start, WINDOW + QB, axis=2)
        vwb = lax.dynamic_slice_in_dim(vw, start, WINDOW + QB, axis=2)
        jpos = start - WINDOW + jnp.arange(WINDOW + QB)
        m_w = (jpos[None, :] >= 0) & (jpos[None, :] <= t[:, None]) & (jpos[None, :] > t[:, None] - WINDOW)
        p_w = masked_softmax(jnp.einsum('bgrqd,bgkd->bgrqk', qb, kwb), m_w)
        o_w = jnp.einsum('bgrqk,bgkd->bgrqd', p_w.astype(dt), vwb)
        return gb[..., 0:1] * o_c + gb[..., 1:2] * o_s + gb[..., 2:3] * o_w

    out = lax.map(block, jnp.arange(S // QB))
    return out.transpose(1, 0, 4, 2, 3, 5).reshape(B, S, NSA_WIDTH)


def rg_lru(x, conv_w, conv_b, w_a, b_a, w_x, b_x, lam):
    B, S, C = x.shape
    xc = lax.conv_general_dilated(x, conv_w[:, None, :], window_strides=(1,),
                                  padding=((CONV_WIDTH - 1, 0),),
                                  dimension_numbers=('NWC', 'WIO', 'NWC'),
                                  feature_group_count=C) + conv_b
    xb = xc.reshape(B, S, LRU_BLOCKS, LRU_BLOCK_DIM)
    r = jax.nn.sigmoid((jnp.einsum('bsnd,nde->bsne', xb, w_a).reshape(B, S, C) + b_a).astype(jnp.float32))
    i = jax.nn.sigmoid((jnp.einsum('bsnd,nde->bsne', xb, w_x).reshape(B, S, C) + b_x).astype(jnp.float32))
    log_a = -LRU_C * r * jax.nn.softplus(-lam.astype(jnp.float32))
    a = jnp.exp(log_a)
    u = jnp.sqrt(-jnp.expm1(2.0 * log_a)) * (i * xc.astype(jnp.float32))

    def combine(lhs, rhs):
        a1, b1 = lhs
        a2, b2 = rhs
        return a1 * a2, a2 * b1 + b2

    _, h = lax.associative_scan(combine, (a, u), axis=1)
    return h.astype(x.dtype)


def hybrid_layer(x, w_in, w_out, ln_g, ln_b, cmp_pos, cmp_w, conv_w, conv_b, w_a, b_a, w_x, b_x, lam):
    proj = x @ w_in
    (rq, rk, rv, rg, nq, nkc, nvc, nks, nvs, nkw, nvw, ng, ngl, lx, lg) = jnp.split(proj, IN_SPLIT_POINTS, axis=-1)
    y_ret = retention(rq, rk, rv) * jax.nn.silu(rg)
    y_nsa = nsa(nq, nkc, nvc, nks, nvs, nkw, nvw, ngl, cmp_pos, cmp_w) * jax.nn.silu(ng)
    y_lru = rg_lru(lx, conv_w, conv_b, w_a, b_a, w_x, b_x, lam) * jax.nn.silu(lg)
    y = jnp.concatenate([y_ret, y_nsa, y_lru], axis=-1) @ w_out
    return layer_norm(DEEPNORM_ALPHA * x + y, ln_g, ln_b)


def setup_inputs(seed: int = 0) -> dict:
    key = jax.random.key(seed)
    ks = jax.random.split(key, 16)
    f32 = jnp.float32
    x = jax.random.normal(ks[0], (BATCH, SEQ, D_MODEL), f32)
    w_in = jax.random.normal(ks[1], (DEPTH, D_MODEL, D_IN), f32) * D_MODEL ** -0.5
    w_out = jax.random.normal(ks[2], (DEPTH, D_MIX, D_MODEL), f32) * (D_MIX ** -0.5) * DEEPNORM_BETA
    ln_g = 1.0 + 0.02 * jax.random.normal(ks[3], (DEPTH, D_MODEL), f32)
    ln_b = 0.02 * jax.random.normal(ks[4], (DEPTH, D_MODEL), f32)
    nsa_cmp_pos = 0.1 * jax.random.normal(ks[5], (DEPTH, 2, CMP_LEN, HEAD_DIM), f32)
    nsa_cmp_w = jax.random.normal(ks[6], (DEPTH, 2, CMP_LEN * HEAD_DIM, HEAD_DIM), f32) * (CMP_LEN * HEAD_DIM) ** -0.5
    lru_conv_w = jax.random.normal(ks[7], (DEPTH, CONV_WIDTH, LRU_WIDTH), f32) * CONV_WIDTH ** -0.5
    lru_conv_b = 0.02 * jax.random.normal(ks[8], (DEPTH, LRU_WIDTH), f32)
    lru_w_a = jax.random.normal(ks[9], (DEPTH, LRU_BLOCKS, LRU_BLOCK_DIM, LRU_BLOCK_DIM), f32) * LRU_BLOCK_DIM ** -0.5
    lru_b_a = 0.02 * jax.random.normal(ks[10], (DEPTH, LRU_WIDTH), f32)
    lru_w_x = jax.random.normal(ks[11], (DEPTH, LRU_BLOCKS, LRU_BLOCK_DIM, LRU_BLOCK_DIM), f32) * LRU_BLOCK_DIM ** -0.5
    lru_b_x = 0.02 * jax.random.normal(ks[12], (DEPTH, LRU_WIDTH), f32)
    u = jax.random.uniform(ks[13], (DEPTH, LRU_WIDTH), f32, minval=0.9, maxval=0.999)
    a0 = u ** (1.0 / LRU_C)
    lru_lambda = jnp.log(a0) - jnp.log1p(-a0)
    return {"x": x, "w_in": w_in, "w_out": w_out, "ln_g": ln_g, "ln_b": ln_b,
            "nsa_cmp_pos": nsa_cmp_pos, "nsa_cmp_w": nsa_cmp_w,
            "lru_conv_w": lru_conv_w, "lru_conv_b": lru_conv_b,
            "lru_w_a": lru_w_a, "lru_b_a": lru_b_a, "lru_w_x": lru_w_x, "lru_b_x": lru_b_x,
            "lru_lambda": lru_lambda}


def reference(x, w_in, w_out, ln_g, ln_b, nsa_cmp_pos, nsa_cmp_w, lru_conv_w, lru_conv_b,
              lru_w_a, lru_b_a, lru_w_x, lru_b_x, lru_lambda):
    for l in range(DEPTH):
        x = hybrid_layer(x, w_in[l], w_out[l], ln_g[l], ln_b[l], nsa_cmp_pos[l], nsa_cmp_w[l],
                         lru_conv_w[l], lru_conv_b[l], lru_w_a[l], lru_b_a[l], lru_w_x[l], lru_b_x[l],
                         lru_lambda[l])
    return x
```

```python
import functools
import math

import jax
import jax.numpy as jnp
import numpy as np
from jax import lax
from jax.experimental import pallas as pl
from jax.experimental.pallas import tpu as pltpu

F32 = jnp.float32
BF16 = jnp.bfloat16

D_MODEL = 1024
DEPTH = 2
HEAD_DIM = 64
RET_HEADS = 4
RET_WIDTH = RET_HEADS * HEAD_DIM
ROPE_BASE = 10000.0
NSA_HEADS = 8
NSA_KV_HEADS = 2
NSA_GROUP = NSA_HEADS // NSA_KV_HEADS
NSA_WIDTH = NSA_HEADS * HEAD_DIM
NSA_KV_WIDTH = NSA_KV_HEADS * HEAD_DIM
CMP_LEN = 32
CMP_STRIDE = 16
SEL_LEN = 64
N_SELECT = 16
WINDOW = 512
N_BRANCH = 3
FORCE_SCORE = 1.0e4
LRU_WIDTH = 256
LRU_BLOCKS = 4
LRU_BLOCK_DIM = LRU_WIDTH // LRU_BLOCKS
CONV_WIDTH = 4
LRU_C = 8.0
DEEPNORM_ALPHA = (2.0 * DEPTH) ** 0.25
LN_EPS = 1e-5
NEG = -1e30

IN_SIZES = (RET_WIDTH,) * 4 + (NSA_WIDTH,) + (NSA_KV_WIDTH,) * 6 + (
    NSA_WIDTH, NSA_HEADS * N_BRANCH, LRU_WIDTH, LRU_WIDTH)
IN_OFFS = tuple(int(v) for v in np.cumsum((0,) + IN_SIZES))

V7X_VMEM_LIMIT = 48 * 1024 * 1024
PROJ_TM = 256
RET_CHUNK = 256
LRU_TS = 256
OUT_TM = 256
NSA_TQ = 128
NSA_KB = 128
GL_ROWS = 16

_CONTRACT_LAST = (((1,), (1,)), ((), ()))
_CONTRACT_FIRST = (((0,), (0,)), ((), ()))


def _cparams(sem):
    return pltpu.CompilerParams(dimension_semantics=sem, vmem_limit_bytes=V7X_VMEM_LIMIT)


_WN_COLS = (4 * RET_WIDTH, 2 * NSA_KV_WIDTH, NSA_KV_WIDTH, NSA_KV_WIDTH, NSA_WIDTH, 2 * LRU_WIDTH)
_WN_OFFS = tuple(int(v) for v in np.cumsum((0,) + _WN_COLS))
_WT_ROWS = (NSA_WIDTH, NSA_KV_WIDTH, NSA_KV_WIDTH, NSA_KV_HEADS * GL_ROWS)
_WT_OFFS = tuple(int(v) for v in np.cumsum((0,) + _WT_ROWS))


def _proj_kernel(x_ref, wn_ref, wt_ref, ret_ref, kvc_ref, ks_ref, kw_ref, ng_ref, lru_ref,
                 qt_ref, vst_ref, vwt_ref, glt_ref):
    xb = x_ref[...].astype(BF16)

    def nat(i):
        return jnp.dot(xb, wn_ref[:, _WN_OFFS[i]:_WN_OFFS[i + 1]], preferred_element_type=F32)

    def tr(i):
        return lax.dot_general(wt_ref[_WT_OFFS[i]:_WT_OFFS[i + 1], :], xb, _CONTRACT_LAST,
                               preferred_element_type=F32)

    ret_ref[...] = nat(0)
    kvc_ref[...] = nat(1)
    ks = nat(2)
    kw = nat(3)
    for g in range(NSA_KV_HEADS):
        ks_ref[g] = ks[:, g * HEAD_DIM:(g + 1) * HEAD_DIM].astype(BF16)
        kw_ref[g] = kw[:, g * HEAD_DIM:(g + 1) * HEAD_DIM].astype(BF16)
    ng_ref[...] = nat(4)
    lru_ref[...] = nat(5)
    qt_ref[...] = tr(0).astype(BF16)
    vst_ref[...] = tr(1).astype(BF16)
    vwt_ref[...] = tr(2).astype(BF16)
    glt_ref[...] = tr(3)


def _project(x, wn, wt):
    B, S, _ = x.shape
    tm = PROJ_TM
    nn, nt = wn.shape[1], wt.shape[0]
    G = NSA_KV_HEADS
    row = lambda w: pl.BlockSpec((None, tm, w), lambda b, i: (b, i, 0))
    col = lambda r: pl.BlockSpec((None, r, tm), lambda b, i: (b, 0, i))
    kspec = pl.BlockSpec((None, G, tm, HEAD_DIM), lambda b, i: (b, 0, i, 0))
    out_shape = (
        jax.ShapeDtypeStruct((B, S, 4 * RET_WIDTH), F32),
        jax.ShapeDtypeStruct((B, S, 2 * NSA_KV_WIDTH), F32),
        jax.ShapeDtypeStruct((B, G, S, HEAD_DIM), BF16),
        jax.ShapeDtypeStruct((B, G, S, HEAD_DIM), BF16),
        jax.ShapeDtypeStruct((B, S, NSA_WIDTH), F32),
        jax.ShapeDtypeStruct((B, S, 2 * LRU_WIDTH), F32),
        jax.ShapeDtypeStruct((B, NSA_WIDTH, S), BF16),
        jax.ShapeDtypeStruct((B, NSA_KV_WIDTH, S), BF16),
        jax.ShapeDtypeStruct((B, NSA_KV_WIDTH, S), BF16),
        jax.ShapeDtypeStruct((B, G * GL_ROWS, S), F32),
    )
    return pl.pallas_call(
        _proj_kernel,
        grid=(B, S // tm),
        in_specs=[row(D_MODEL),
                  pl.BlockSpec((D_MODEL, nn), lambda b, i: (0, 0)),
                  pl.BlockSpec((nt, D_MODEL), lambda b, i: (0, 0))],
        out_specs=(row(4 * RET_WIDTH), row(2 * NSA_KV_WIDTH), kspec, kspec, row(NSA_WIDTH),
                   row(2 * LRU_WIDTH), col(NSA_WIDTH), col(NSA_KV_WIDTH), col(NSA_KV_WIDTH),
                   col(G * GL_ROWS)),
        out_shape=out_shape,
        compiler_params=_cparams(("parallel", "parallel")),
        name="in_proj",
    )(x, wn, wt)


def _prep_in_weights(w_in):
    seg = [w_in[:, IN_OFFS[i]:IN_OFFS[i + 1]] for i in range(len(IN_SIZES))]
    (rq, rk, rv, rg, nq, nkc, nvc, nks, nvs, nkw, nvw, ng, ngl, lx, lg) = seg
    wn = jnp.concatenate([rq, rk, rv, rg, nkc, nvc, nks, nkw, ng, lx, lg], axis=1).astype(BF16)
    glt = ngl.T.reshape(NSA_KV_HEADS, NSA_GROUP * N_BRANCH, D_MODEL)
    glt = jnp.pad(glt, ((0, 0), (0, GL_ROWS - NSA_GROUP * N_BRANCH), (0, 0)))
    wt = jnp.concatenate([nq.T * (HEAD_DIM ** -0.5), nvs.T, nvw.T,
                          glt.reshape(NSA_KV_HEADS * GL_ROWS, D_MODEL)], axis=0).astype(BF16)
    return wn, wt


def _ret_kernel(ret_ref, cos_ref, sa_ref, sb_ref, qd_ref, kd_ref, dmat_ref, out_ref, state_ref,
                *, chunk_decay):
    c = pl.program_id(1)

    @pl.when(c == 0)
    def _():
        state_ref[...] = jnp.zeros_like(state_ref)

    W = RET_WIDTH
    q = ret_ref[:, 0:W]
    k = ret_ref[:, W:2 * W]
    v = ret_ref[:, 2 * W:3 * W]
    gate = ret_ref[:, 3 * W:4 * W]
    cos, sa, sb = cos_ref[...], sa_ref[...], sb_ref[...]
    half = HEAD_DIM // 2

    def rope(t):
        return t * cos + pltpu.roll(t, W - half, 1) * sa + pltpu.roll(t, half, 1) * sb

    qr = rope(q)
    kr = rope(k) * (HEAD_DIM ** -0.5)
    qd = qr * qd_ref[...]
    kd = kr * kd_ref[...]
    ys = []
    for h in range(RET_HEADS):
        sl = slice(h * HEAD_DIM, (h + 1) * HEAD_DIM)
        qb, kb, vb = qr[:, sl].astype(BF16), kr[:, sl].astype(BF16), v[:, sl].astype(BF16)
        s = lax.dot_general(qb, kb, _CONTRACT_LAST, preferred_element_type=F32) * dmat_ref[h]
        inner = jnp.dot(s.astype(BF16), vb, preferred_element_type=F32)
        st = state_ref[h]
        cross = jnp.dot(qd[:, sl].astype(BF16), st.astype(BF16), preferred_element_type=F32)
        y = inner + cross
        mu = jnp.mean(y, axis=-1, keepdims=True)
        yc = y - mu
        var = jnp.mean(yc * yc, axis=-1, keepdims=True)
        ys.append(yc * lax.rsqrt(var + LN_EPS))
        kv = lax.dot_general(kd[:, sl].astype(BF16), vb, _CONTRACT_FIRST,
                             preferred_element_type=F32)
        state_ref[h] = st * chunk_decay[h] + kv
    y = jnp.concatenate(ys, axis=1)
    out_ref[...] = y * (gate * jax.nn.sigmoid(gate))


def _ret_tables(S):
    C, H, d = RET_CHUNK, RET_HEADS, HEAD_DIM
    half = d // 2
    inv = 1.0 / (ROPE_BASE ** (np.arange(half, dtype=np.float64) / half))
    ang = np.arange(S, dtype=np.float64)[:, None] * inv[None, :]
    cos_h = np.concatenate([np.cos(ang), np.cos(ang)], axis=1)
    sin_lo = np.concatenate([-np.sin(ang), np.zeros_like(ang)], axis=1)
    sin_hi = np.concatenate([np.zeros_like(ang), np.sin(ang)], axis=1)
    tile = lambda a: np.tile(a, (1, H)).astype(np.float32)
    log_g = np.log(1.0 - 2.0 ** (-5.0 - np.arange(H, dtype=np.float64)))
    idx = np.arange(C, dtype=np.float64)
    diff = idx[:, None] - idx[None, :]
    dmat = np.where(diff >= 0, np.exp(np.maximum(diff, 0.0)[None] * log_g[:, None, None]), 0.0)
    kd = np.exp((C - 1.0 - idx)[:, None] * log_g[None, :])
    qd = np.exp((idx + 1.0)[:, None] * log_g[None, :])
    rep = lambda a: np.repeat(a, d, axis=1).astype(np.float32)
    chunk_decay = tuple(float(v) for v in np.exp(C * log_g))
    return (tile(cos_h), tile(sin_lo), tile(sin_hi), rep(qd), rep(kd),
            dmat.astype(np.float32), chunk_decay)


def _retention(ret):
    B, S, _ = ret.shape
    C, W = RET_CHUNK, RET_WIDTH
    cos, sa, sb, qd, kd, dmat, chunk_decay = _ret_tables(S)
    pos = pl.BlockSpec((C, W), lambda b, c: (c, 0))
    fixed = pl.BlockSpec((C, W), lambda b, c: (0, 0))
    return pl.pallas_call(
        functools.partial(_ret_kernel, chunk_decay=chunk_decay),
        grid=(B, S // C),
        in_specs=[pl.BlockSpec((None, C, 4 * W), lambda b, c: (b, c, 0)), pos, pos, pos,
                  fixed, fixed, pl.BlockSpec((RET_HEADS, C, C), lambda b, c: (0, 0, 0))],
        out_specs=pl.BlockSpec((None, C, W), lambda b, c: (b, c, 0)),
        out_shape=jax.ShapeDtypeStruct((B, S, W), F32),
        scratch_shapes=[pltpu.VMEM((RET_HEADS, HEAD_DIM, HEAD_DIM), F32)],
        compiler_params=_cparams(("parallel", "arbitrary")),
        name="retention",
    )(ret, jnp.asarray(cos), jnp.asarray(sa), jnp.asarray(sb), jnp.asarray(qd), jnp.asarray(kd),
      jnp.asarray(dmat))


_CMP_ROWS = 256


def _cmp_kernel(kr_ref, pos_ref, wbig_ref, kc_ref, vct_ref):
    n = _CMP_ROWS
    half = 2 * NSA_KV_WIDTH
    lhs = jnp.concatenate([kr_ref[...], pos_ref[...]], axis=0).astype(BF16)
    out = jnp.dot(lhs, wbig_ref[...], preferred_element_type=F32)
    lo = out[0:n, 0:half]
    hi = out[0:n, half:2 * half]
    const = out[n:n + 1, 0:half] + out[n + 1:n + 2, half:2 * half]
    kv = lo + pltpu.roll(hi, n - 1, 0) + const
    for g in range(NSA_KV_HEADS):
        kc_ref[g] = kv[:, g * HEAD_DIM:(g + 1) * HEAD_DIM].astype(BF16)
    vct_ref[...] = kv[:, NSA_KV_WIDTH:2 * NSA_KV_WIDTH].T.astype(BF16)


def _prep_cmp_weights(cmp_pos, cmp_w):
    L2, d, G = CMP_STRIDE, HEAD_DIM, NSA_KV_HEADS
    w5 = cmp_w.reshape(2, 2, L2, d, d)
    eye_kv = jnp.eye(2, dtype=F32)
    eye_g = jnp.eye(G, dtype=F32)
    wbig = jnp.einsum('khlde,kK,gG->lkgdhKGe', w5, eye_kv, eye_g)
    wbig = wbig.reshape(L2 * 2 * G * d, 2 * 2 * G * d).astype(BF16)
    p4 = cmp_pos.reshape(2, 2, L2, d)
    prow = jnp.broadcast_to(p4.transpose(1, 2, 0, 3)[:, :, :, None, :], (2, L2, 2, G, d))
    prow = prow.reshape(2, L2 * 2 * G * d)
    prow = jnp.pad(prow, ((0, 6), (0, 0)))
    return prow, wbig


def _compress(kvc, prow, wbig):
    B, S, _ = kvc.shape
    n, G = _CMP_ROWS, NSA_KV_HEADS
    kr = kvc.reshape(B, n, CMP_STRIDE * 2 * NSA_KV_WIDTH)
    kw = kr.shape[-1]
    return pl.pallas_call(
        _cmp_kernel,
        grid=(B,),
        in_specs=[pl.BlockSpec((None, n, kw), lambda b: (b, 0, 0)),
                  pl.BlockSpec((8, kw), lambda b: (0, 0)),
                  pl.BlockSpec(wbig.shape, lambda b: (0, 0))],
        out_specs=(pl.BlockSpec((None, G, n, HEAD_DIM), lambda b: (b, 0, 0, 0)),
                   pl.BlockSpec((None, NSA_KV_WIDTH, n), lambda b: (b, 0, 0))),
        out_shape=(jax.ShapeDtypeStruct((B, G, n, HEAD_DIM), BF16),
                   jax.ShapeDtypeStruct((B, NSA_KV_WIDTH, n), BF16)),
        compiler_params=_cparams(("parallel",)),
        name="nsa_compress",
    )(kr, prow, wbig)


def _nsa_kernel(qt_ref, kc_ref, vct_ref, ks_ref, vst_ref, kw_ref, vwt_ref, glt_ref, ng_ref,
                selmap_ref, band_ref, out_ref, val_ref, bias_ref):
    TQ, KB, R, d = NSA_TQ, NSA_KB, NSA_GROUP, HEAD_DIM
    LW = R * TQ
    n_sel = selmap_ref.shape[0]
    qi = pl.program_id(2)
    t0 = qi * TQ
    qs = jnp.concatenate([qt_ref[r * d:(r + 1) * d, :] for r in range(R)], axis=1)

    n_cmp = kc_ref.shape[0]
    tq_w = t0 + lax.rem(lax.broadcasted_iota(jnp.int32, (1, LW), 1), TQ)
    cmp_end = lax.broadcasted_iota(jnp.int32, (n_cmp, 1), 0) * CMP_STRIDE + (CMP_LEN - 1)
    valid = cmp_end <= tq_w
    s = jnp.dot(kc_ref[...], qs, preferred_element_type=F32)
    s = jnp.where(valid, s, NEG)
    m = jnp.max(s, axis=0, keepdims=True)
    p = jnp.where(valid, jnp.exp(s - m), 0.0)
    l = jnp.sum(p, axis=0, keepdims=True)
    p = p * jnp.where(l > 0.0, 1.0 / l, 0.0)
    o_c = jnp.dot(vct_ref[...], p.astype(BF16), preferred_element_type=F32)

    psum = p[:, 0:TQ]
    for r in range(1, R):
        psum = psum + p[:, r * TQ:(r + 1) * TQ]
    p_hi = psum.astype(BF16)
    p_lo = (psum - p_hi.astype(F32)).astype(BF16)
    imp = (jnp.dot(selmap_ref[...], p_hi, preferred_element_type=F32)
           + jnp.dot(selmap_ref[...], p_lo, preferred_element_type=F32))

    tq = t0 + lax.broadcasted_iota(jnp.int32, (1, TQ), 1)
    jrow = lax.broadcasted_iota(jnp.int32, (n_sel, 1), 0)
    cur = lax.shift_right_logical(tq, int(math.log2(SEL_LEN)))
    forced = (jrow == 0) | (jrow == cur) | (jrow == cur - 1)
    causal = jrow * SEL_LEN <= tq
    val = jnp.where(forced, FORCE_SCORE, jnp.where(causal, imp, -FORCE_SCORE))
    val_ref[...] = val
    n_live = 2 * qi + 2

    def rank_body(jp, cnt):
        rowv = val_ref[pl.ds(jp, 1), :]
        beats = (rowv > val) | ((rowv == val) & (jp < jrow))
        return cnt + jnp.where(beats, 1.0, 0.0)

    cnt = lax.fori_loop(0, n_live, rank_body, jnp.zeros((n_sel, TQ), F32))
    bias = jnp.where(cnt < float(N_SELECT), 0.0, NEG)
    bias_ref[...] = jnp.concatenate([bias] * R, axis=1)

    def attend(s, vt, carry):
        m, l, acc = carry
        m_new = jnp.maximum(m, jnp.max(s, axis=0, keepdims=True))
        alpha = jnp.exp(m - m_new)
        p = jnp.exp(s - m_new)
        l = alpha * l + jnp.sum(p, axis=0, keepdims=True)
        acc = alpha * acc + jnp.dot(vt, p.astype(BF16), preferred_element_type=F32)
        return m_new, l, acc

    init = (jnp.full((1, LW), NEG, F32), jnp.zeros((1, LW), F32), jnp.zeros((d, LW), F32))
    half = KB // 2

    def sel_scores(kb):
        off = pl.multiple_of(kb * KB, KB)
        s = jnp.dot(ks_ref[pl.ds(off, KB), :], qs, preferred_element_type=F32)
        b0 = jnp.broadcast_to(bias_ref[pl.ds(2 * kb, 1), :], (half, LW))
        b1 = jnp.broadcast_to(bias_ref[pl.ds(2 * kb + 1, 1), :], (half, LW))
        return s + jnp.concatenate([b0, b1], axis=0), vst_ref[:, pl.ds(off, KB)]

    def sel_body(kb, carry):
        s, vt = sel_scores(kb)
        return attend(s, vt, carry)

    carry = lax.fori_loop(0, qi, sel_body, init)
    s, vt = sel_scores(qi)
    _, l_s, acc_s = attend(s + band_ref[WINDOW // KB], vt, carry)

    def win_body(kb, carry):
        off = pl.multiple_of(kb * KB, KB)
        s = jnp.dot(kw_ref[pl.ds(off, KB), :], qs, preferred_element_type=F32)
        s = s + band_ref[kb - qi + WINDOW // KB]
        return attend(s, vwt_ref[:, pl.ds(off, KB)], carry)

    _, l_w, acc_w = lax.fori_loop(jnp.maximum(qi - WINDOW // KB, 0), qi + 1, win_body, init)

    gl = jax.nn.sigmoid(glt_ref[...])

    def gate(br):
        return jnp.concatenate([gl[r * N_BRANCH + br:r * N_BRANCH + br + 1, :] for r in range(R)],
                               axis=1)

    res = gate(0) * o_c + gate(1) * (acc_s / l_s) + gate(2) * (acc_w / l_w)
    res = jnp.concatenate([res[:, r * TQ:(r + 1) * TQ] for r in range(R)], axis=0)
    ng = ng_ref[...]
    out_ref[...] = res.T * (ng * jax.nn.sigmoid(ng))


def _nsa_tables(S):
    n_cmp = (S - CMP_LEN) // CMP_STRIDE + 1
    n_sel = S // SEL_LEN
    ci = np.arange(n_cmp)[:, None]
    sj = np.arange(n_sel)[None, :]
    overlap = (np.minimum(ci * CMP_STRIDE + CMP_LEN, sj * SEL_LEN + SEL_LEN)
               - np.maximum(ci * CMP_STRIDE, sj * SEL_LEN))
    sel_map = np.clip(overlap, 0, None) / CMP_STRIDE
    selmap_t = np.zeros((n_sel, _CMP_ROWS), np.float32)
    selmap_t[:, :n_cmp] = sel_map.T
    nb = WINDOW // NSA_KB + 1
    jl = np.arange(NSA_KB)[:, None]
    tl = np.arange(NSA_TQ)[None, :]
    band = np.zeros((nb, NSA_KB, NSA_TQ), np.float32)
    for i in range(nb):
        rel = (i - WINDOW // NSA_KB) * NSA_KB + jl - tl
        band[i] = np.where((rel <= 0) & (rel > -WINDOW), 0.0, NEG)
    band = np.tile(band, (1, 1, NSA_GROUP))
    return selmap_t, band


def _nsa_attention(qt, kc, vct, ks, vst, kw, vwt, glt, ng):
    B, _, S = qt.shape
    G, R, d, TQ = NSA_KV_HEADS, NSA_GROUP, HEAD_DIM, NSA_TQ
    selmap_t, band = _nsa_tables(S)
    n_sel = S // SEL_LEN
    per_bg_rows = lambda n: pl.BlockSpec((None, None, n, d), lambda b, g, q: (b, g, 0, 0))
    per_bg_cols = lambda n: pl.BlockSpec((None, d, n), lambda b, g, q: (b, g, 0))
    return pl.pallas_call(
        _nsa_kernel,
        grid=(B, G, S // TQ),
        in_specs=[pl.BlockSpec((None, R * d, TQ), lambda b, g, q: (b, g, q)),
                  per_bg_rows(_CMP_ROWS), per_bg_cols(_CMP_ROWS),
                  per_bg_rows(S), per_bg_cols(S), per_bg_rows(S), per_bg_cols(S),
                  pl.BlockSpec((None, GL_ROWS, TQ), lambda b, g, q: (b, g, q)),
                  pl.BlockSpec((None, TQ, R * d), lambda b, g, q: (b, q, g)),
                  pl.BlockSpec(selmap_t.shape, lambda b, g, q: (0, 0)),
                  pl.BlockSpec(band.shape, lambda b, g, q: (0, 0, 0))],
        out_specs=pl.BlockSpec((None, TQ, R * d), lambda b, g, q: (b, q, g)),
        out_shape=jax.ShapeDtypeStruct((B, S, NSA_WIDTH), F32),
        scratch_shapes=[pltpu.VMEM((n_sel, TQ), F32), pltpu.VMEM((n_sel, R * TQ), F32)],
        compiler_params=_cparams(("parallel", "parallel", "arbitrary")),
        name="nsa_attention",
    )(qt, kc, vct, ks, vst, kw, vwt, glt, ng, jnp.asarray(selmap_t, dtype=BF16), jnp.asarray(band))


def _lru_kernel(lru_ref, cw_ref, cb_ref, wa_ref, ba_ref, wx_ref, bx_ref, lam_ref, out_ref,
                xext_ref, h_ref):
    ts, W = LRU_TS, LRU_WIDTH
    i = pl.program_id(1)

    @pl.when(i == 0)
    def _():
        xext_ref[0:8, :] = jnp.zeros((8, W), F32)
        h_ref[...] = jnp.zeros_like(h_ref)

    x = lru_ref[:, 0:W]
    gate = lru_ref[:, W:2 * W]
    xext_ref[8:8 + ts, :] = x
    xc = cb_ref[...] + cw_ref[0:1, :] * xext_ref[pl.ds(8 - (CONV_WIDTH - 1), ts), :]
    for w in range(1, CONV_WIDTH):
        xc = xc + cw_ref[w:w + 1, :] * xext_ref[pl.ds(8 - (CONV_WIDTH - 1) + w, ts), :]
    xext_ref[0:8, :] = x[ts - 8:ts, :]
    xcb = xc.astype(BF16)
    r = jax.nn.sigmoid(jnp.dot(xcb, wa_ref[...], preferred_element_type=F32) + ba_ref[...])
    gi = jax.nn.sigmoid(jnp.dot(xcb, wx_ref[...], preferred_element_type=F32) + bx_ref[...])
    nl = -lam_ref[...]
    softplus = jnp.maximum(nl, 0.0) + jnp.log1p(jnp.exp(-jnp.abs(nl)))
    log_a = (-LRU_C) * r * softplus
    a = jnp.exp(log_a)
    u = jnp.sqrt(-jnp.tanh(log_a) * (a * a + 1.0)) * (gi * xc)
    row = lax.broadcasted_iota(jnp.int32, (ts, 1), 0)
    step = 1
    while step < ts:
        keep = row >= step
        a_sh = jnp.where(keep, pltpu.roll(a, step, 0), 1.0)
        u_sh = jnp.where(keep, pltpu.roll(u, step, 0), 0.0)
        u = a * u_sh + u
        a = a * a_sh
        step *= 2
    h = u + a * h_ref[0:1, :]
    h_ref[0:1, :] = h[ts - 1:ts, :]
    out_ref[...] = h * (gate * jax.nn.sigmoid(gate))


def _block_diag(w):
    n, d, _ = w.shape
    return jnp.einsum('nde,nm->ndme', w, jnp.eye(n, dtype=w.dtype)).reshape(n * d, n * d)


def _rg_lru(lru, conv_w, conv_b, w_a, b_a, w_x, b_x, lam):
    B, S, _ = lru.shape
    ts, W = LRU_TS, LRU_WIDTH
    vec = lambda a: a.reshape(1, W)
    fixed = lambda r: pl.BlockSpec((r, W), lambda b, i: (0, 0))
    return pl.pallas_call(
        _lru_kernel,
        grid=(B, S // ts),
        in_specs=[pl.BlockSpec((None, ts, 2 * W), lambda b, i: (b, i, 0)),
                  fixed(CONV_WIDTH), fixed(1), fixed(W), fixed(1), fixed(W), fixed(1), fixed(1)],
        out_specs=pl.BlockSpec((None, ts, W), lambda b, i: (b, i, 0)),
        out_shape=jax.ShapeDtypeStruct((B, S, W), F32),
        scratch_shapes=[pltpu.VMEM((ts + 8, W), F32), pltpu.VMEM((8, W), F32)],
        compiler_params=_cparams(("parallel", "arbitrary")),
        name="rg_lru",
    )(lru, conv_w, vec(conv_b), _block_diag(w_a).astype(BF16), vec(b_a),
      _block_diag(w_x).astype(BF16), vec(b_x), vec(lam))


def _out_kernel(x_ref, yr_ref, yn_ref, yl_ref, wo_ref, g_ref, b_ref, out_ref):
    o1, o2 = RET_WIDTH, RET_WIDTH + NSA_WIDTH
    y = jnp.dot(yr_ref[...].astype(BF16), wo_ref[0:o1, :], preferred_element_type=F32)
    y = y + jnp.dot(yn_ref[...].astype(BF16), wo_ref[o1:o2, :], preferred_element_type=F32)
    y = y + jnp.dot(yl_ref[...].astype(BF16), wo_ref[o2:D_MODEL, :], preferred_element_type=F32)
    z = DEEPNORM_ALPHA * x_ref[...] + y
    mu = jnp.mean(z, axis=-1, keepdims=True)
    zc = z - mu
    var = jnp.mean(zc * zc, axis=-1, keepdims=True)
    out_ref[...] = zc * lax.rsqrt(var + LN_EPS) * g_ref[...] + b_ref[...]


def _out_proj(x, y_ret, y_nsa, y_lru, w_out, ln_g, ln_b):
    B, S, _ = x.shape
    tm = OUT_TM
    row = lambda w: pl.BlockSpec((None, tm, w), lambda b, i: (b, i, 0))
    fixed = lambda r: pl.BlockSpec((r, D_MODEL), lambda b, i: (0, 0))
    return pl.pallas_call(
        _out_kernel,
        grid=(B, S // tm),
        in_specs=[row(D_MODEL), row(RET_WIDTH), row(NSA_WIDTH), row(LRU_WIDTH),
                  fixed(D_MODEL), fixed(1), fixed(1)],
        out_specs=row(D_MODEL),
        out_shape=jax.ShapeDtypeStruct((B, S, D_MODEL), F32),
        compiler_params=_cparams(("parallel", "parallel")),
        name="out_proj_ln",
    )(x, y_ret, y_nsa, y_lru, w_out.astype(BF16), ln_g.reshape(1, D_MODEL),
      ln_b.reshape(1, D_MODEL))


def _layer(x, w_in, w_out, ln_g, ln_b, cmp_pos, cmp_w, conv_w, conv_b, w_a, b_a, w_x, b_x, lam):
    wn, wt = _prep_in_weights(w_in)
    ret, kvc, ks, kw, ng, lru, qt, vst, vwt, glt = _project(x, wn, wt)
    y_ret = _retention(ret)
    prow, wbig = _prep_cmp_weights(cmp_pos, cmp_w)
    kc, vct = _compress(kvc, prow, wbig)
    y_nsa = _nsa_attention(qt, kc, vct, ks, vst, kw, vwt, glt, ng)
    y_lru = _rg_lru(lru, conv_w, conv_b, w_a, b_a, w_x, b_x, lam)
    return _out_proj(x, y_ret, y_nsa, y_lru, w_out, ln_g, ln_b)


def kernel(x, w_in, w_out, ln_g, ln_b, nsa_cmp_pos, nsa_cmp_w, lru_conv_w, lru_conv_b,
           lru_w_a, lru_b_a, lru_w_x, lru_b_x, lru_lambda):
    assert x.shape[1] % max(PROJ_TM, RET_CHUNK, LRU_TS, OUT_TM, NSA_TQ) == 0
    assert x.shape[1] // CMP_STRIDE == _CMP_ROWS
    for l in range(DEPTH):
        x = _layer(x, w_in[l], w_out[l], ln_g[l], ln_b[l], nsa_cmp_pos[l], nsa_cmp_w[l],
                   lru_conv_w[l], lru_conv_b[l], lru_w_a[l], lru_b_a[l], lru_w_x[l], lru_b_x[l],
                   lru_lambda[l])
    return x
```

```python
import functools
import math

import jax
import jax.numpy as jnp
import numpy as np
from jax import lax
from jax.experimental import pallas as pl
from jax.experimental.pallas import tpu as pltpu

F32 = jnp.float32
BF16 = jnp.bfloat16

D_MODEL = 1024
DEPTH = 2
HEAD_DIM = 64
RET_HEADS = 4
RET_WIDTH = RET_HEADS * HEAD_DIM
ROPE_BASE = 10000.0
NSA_HEADS = 8
NSA_KV_HEADS = 2
NSA_GROUP = NSA_HEADS // NSA_KV_HEADS
NSA_WIDTH = NSA_HEADS * HEAD_DIM
NSA_KV_WIDTH = NSA_KV_HEADS * HEAD_DIM
CMP_LEN = 32
CMP_STRIDE = 16
SEL_LEN = 64
N_SELECT = 16
WINDOW = 512
N_BRANCH = 3
FORCE_SCORE = 1.0e4
LRU_WIDTH = 256
LRU_BLOCKS = 4
LRU_BLOCK_DIM = LRU_WIDTH // LRU_BLOCKS
CONV_WIDTH = 4
LRU_C = 8.0
DEEPNORM_ALPHA = (2.0 * DEPTH) ** 0.25
LN_EPS = 1e-5
NEG = -1e30
LOG2E = math.log2(math.e)
SEL_SHIFT = int(math.log2(SEL_LEN))

IN_SIZES = (RET_WIDTH,) * 4 + (NSA_WIDTH,) + (NSA_KV_WIDTH,) * 6 + (
    NSA_WIDTH, NSA_HEADS * N_BRANCH, LRU_WIDTH, LRU_WIDTH)
IN_OFFS = tuple(int(v) for v in np.cumsum((0,) + IN_SIZES))

V7X_VMEM_LIMIT = 48 * 1024 * 1024
PROJ_TM = 256
RET_CHUNK = 256
LRU_TS = 256
OUT_TM = 256
NSA_TQ = 128
NSA_KB = 256
GL_ROWS = 16

_CONTRACT_LAST = (((1,), (1,)), ((), ()))
_CONTRACT_FIRST = (((0,), (0,)), ((), ()))


def _cparams(sem):
    return pltpu.CompilerParams(dimension_semantics=sem, vmem_limit_bytes=V7X_VMEM_LIMIT)


_WN_COLS = (4 * RET_WIDTH, 2 * NSA_KV_WIDTH, NSA_KV_WIDTH, NSA_KV_WIDTH, NSA_WIDTH, 2 * LRU_WIDTH)
_WN_OFFS = tuple(int(v) for v in np.cumsum((0,) + _WN_COLS))
_WT_ROWS = (NSA_WIDTH, NSA_KV_WIDTH, NSA_KV_WIDTH, NSA_KV_HEADS * GL_ROWS)
_WT_OFFS = tuple(int(v) for v in np.cumsum((0,) + _WT_ROWS))


def _proj_kernel(x_ref, wn_ref, wt_ref, ret_ref, kvc_ref, ks_ref, kw_ref, ng_ref, lru_ref,
                 qt_ref, vst_ref, vwt_ref, glt_ref):
    xb = x_ref[...].astype(BF16)

    def nat(i):
        return jnp.dot(xb, wn_ref[:, _WN_OFFS[i]:_WN_OFFS[i + 1]], preferred_element_type=F32)

    def tr(i):
        return lax.dot_general(wt_ref[_WT_OFFS[i]:_WT_OFFS[i + 1], :], xb, _CONTRACT_LAST,
                               preferred_element_type=F32)

    ret_ref[...] = nat(0)
    kvc_ref[...] = nat(1)
    ks = nat(2)
    tm = ks.shape[0]
    key = pl.program_id(1) * tm + lax.broadcasted_iota(jnp.int32, (tm, 1), 0)
    blk = lax.broadcasted_iota(jnp.int32, (1, HEAD_DIM), 1)
    onehot = jnp.where(lax.shift_right_logical(key, SEL_SHIFT) == blk, 1.0, 0.0)
    for g in range(NSA_KV_HEADS):
        ks_ref[g] = jnp.concatenate([ks[:, g * HEAD_DIM:(g + 1) * HEAD_DIM], onehot],
                                    axis=1).astype(BF16)
    kw_ref[...] = nat(3).astype(BF16)
    ng_ref[...] = nat(4)
    lru_ref[...] = nat(5)
    qt_ref[...] = (tr(0) * LOG2E).astype(BF16)
    vst_ref[...] = tr(1).astype(BF16)
    vwt_ref[...] = tr(2).astype(BF16)
    glt_ref[...] = tr(3)


def _project(x, wn, wt):
    B, S, _ = x.shape
    tm = PROJ_TM
    nn, nt = wn.shape[1], wt.shape[0]
    G = NSA_KV_HEADS
    row = lambda w: pl.BlockSpec((None, tm, w), lambda b, i: (b, i, 0))
    col = lambda r: pl.BlockSpec((None, r, tm), lambda b, i: (b, 0, i))
    kspec = pl.BlockSpec((None, G, tm, 2 * HEAD_DIM), lambda b, i: (b, 0, i, 0))
    out_shape = (
        jax.ShapeDtypeStruct((B, S, 4 * RET_WIDTH), F32),
        jax.ShapeDtypeStruct((B, S, 2 * NSA_KV_WIDTH), F32),
        jax.ShapeDtypeStruct((B, G, S, 2 * HEAD_DIM), BF16),
        jax.ShapeDtypeStruct((B, S, NSA_KV_WIDTH), BF16),
        jax.ShapeDtypeStruct((B, S, NSA_WIDTH), F32),
        jax.ShapeDtypeStruct((B, S, 2 * LRU_WIDTH), F32),
        jax.ShapeDtypeStruct((B, NSA_WIDTH, S), BF16),
        jax.ShapeDtypeStruct((B, NSA_KV_WIDTH, S), BF16),
        jax.ShapeDtypeStruct((B, NSA_KV_WIDTH, S), BF16),
        jax.ShapeDtypeStruct((B, G * GL_ROWS, S), F32),
    )
    return pl.pallas_call(
        _proj_kernel,
        grid=(B, S // tm),
        in_specs=[row(D_MODEL),
                  pl.BlockSpec((D_MODEL, nn), lambda b, i: (0, 0)),
                  pl.BlockSpec((nt, D_MODEL), lambda b, i: (0, 0))],
        out_specs=(row(4 * RET_WIDTH), row(2 * NSA_KV_WIDTH), kspec, row(NSA_KV_WIDTH), row(NSA_WIDTH),
                   row(2 * LRU_WIDTH), col(NSA_WIDTH), col(NSA_KV_WIDTH), col(NSA_KV_WIDTH),
                   col(G * GL_ROWS)),
        out_shape=out_shape,
        compiler_params=_cparams(("parallel", "parallel")),
        name="in_proj",
    )(x, wn, wt)


def _prep_in_weights(w_in):
    seg = [w_in[:, IN_OFFS[i]:IN_OFFS[i + 1]] for i in range(len(IN_SIZES))]
    (rq, rk, rv, rg, nq, nkc, nvc, nks, nvs, nkw, nvw, ng, ngl, lx, lg) = seg
    wn = jnp.concatenate([rq, rk, rv, rg, nkc, nvc, nks, nkw, ng, lx, lg], axis=1).astype(BF16)
    glt = ngl.T.reshape(NSA_KV_HEADS, NSA_GROUP * N_BRANCH, D_MODEL)
    glt = jnp.pad(glt, ((0, 0), (0, GL_ROWS - NSA_GROUP * N_BRANCH), (0, 0)))
    wt = jnp.concatenate([nq.T * (HEAD_DIM ** -0.5), nvs.T, nvw.T,
                          glt.reshape(NSA_KV_HEADS * GL_ROWS, D_MODEL)], axis=0).astype(BF16)
    return wn, wt


def _ret_kernel(ret_ref, cos_ref, sa_ref, sb_ref, qd_ref, kd_ref, dmat_ref, out_ref, state_ref,
                *, chunk_decay):
    c = pl.program_id(1)

    @pl.when(c == 0)
    def _():
        state_ref[...] = jnp.zeros_like(state_ref)

    W = RET_WIDTH
    q = ret_ref[:, 0:W]
    k = ret_ref[:, W:2 * W]
    v = ret_ref[:, 2 * W:3 * W]
    gate = ret_ref[:, 3 * W:4 * W]
    cos, sa, sb = cos_ref[...], sa_ref[...], sb_ref[...]
    half = HEAD_DIM // 2

    def rope(t):
        return t * cos + pltpu.roll(t, W - half, 1) * sa + pltpu.roll(t, half, 1) * sb

    qr = rope(q)
    kr = rope(k) * (HEAD_DIM ** -0.5)
    qd = qr * qd_ref[...]
    kd = kr * kd_ref[...]
    ys = []
    for h in range(RET_HEADS):
        sl = slice(h * HEAD_DIM, (h + 1) * HEAD_DIM)
        qb, kb, vb = qr[:, sl].astype(BF16), kr[:, sl].astype(BF16), v[:, sl].astype(BF16)
        s = lax.dot_general(qb, kb, _CONTRACT_LAST, preferred_element_type=F32) * dmat_ref[h]
        inner = jnp.dot(s.astype(BF16), vb, preferred_element_type=F32)
        st = state_ref[h]
        cross = jnp.dot(qd[:, sl].astype(BF16), st.astype(BF16), preferred_element_type=F32)
        y = inner + cross
        mu = jnp.mean(y, axis=-1, keepdims=True)
        yc = y - mu
        var = jnp.mean(yc * yc, axis=-1, keepdims=True)
        ys.append(yc * lax.rsqrt(var + LN_EPS))
        kv = lax.dot_general(kd[:, sl].astype(BF16), vb, _CONTRACT_FIRST,
                             preferred_element_type=F32)
        state_ref[h] = st * chunk_decay[h] + kv
    y = jnp.concatenate(ys, axis=1)
    out_ref[...] = y * (gate * jax.nn.sigmoid(gate))


def _ret_tables(S):
    C, H, d = RET_CHUNK, RET_HEADS, HEAD_DIM
    half = d // 2
    inv = 1.0 / (ROPE_BASE ** (np.arange(half, dtype=np.float64) / half))
    ang = np.arange(S, dtype=np.float64)[:, None] * inv[None, :]
    cos_h = np.concatenate([np.cos(ang), np.cos(ang)], axis=1)
    sin_lo = np.concatenate([-np.sin(ang), np.zeros_like(ang)], axis=1)
    sin_hi = np.concatenate([np.zeros_like(ang), np.sin(ang)], axis=1)
    tile = lambda a: np.tile(a, (1, H)).astype(np.float32)
    log_g = np.log(1.0 - 2.0 ** (-5.0 - np.arange(H, dtype=np.float64)))
    idx = np.arange(C, dtype=np.float64)
    diff = idx[:, None] - idx[None, :]
    dmat = np.where(diff >= 0, np.exp(np.maximum(diff, 0.0)[None] * log_g[:, None, None]), 0.0)
    kd = np.exp((C - 1.0 - idx)[:, None] * log_g[None, :])
    qd = np.exp((idx + 1.0)[:, None] * log_g[None, :])
    rep = lambda a: np.repeat(a, d, axis=1).astype(np.float32)
    chunk_decay = tuple(float(v) for v in np.exp(C * log_g))
    return (tile(cos_h), tile(sin_lo), tile(sin_hi), rep(qd), rep(kd),
            dmat.astype(np.float32), chunk_decay)


def _retention(ret):
    B, S, _ = ret.shape
    C, W = RET_CHUNK, RET_WIDTH
    cos, sa, sb, qd, kd, dmat, chunk_decay = _ret_tables(S)
    pos = pl.BlockSpec((C, W), lambda b, c: (c, 0))
    fixed = pl.BlockSpec((C, W), lambda b, c: (0, 0))
    return pl.pallas_call(
        functools.partial(_ret_kernel, chunk_decay=chunk_decay),
        grid=(B, S // C),
        in_specs=[pl.BlockSpec((None, C, 4 * W), lambda b, c: (b, c, 0)), pos, pos, pos,
                  fixed, fixed, pl.BlockSpec((RET_HEADS, C, C), lambda b, c: (0, 0, 0))],
        out_specs=pl.BlockSpec((None, C, W), lambda b, c: (b, c, 0)),
        out_shape=jax.ShapeDtypeStruct((B, S, W), F32),
        scratch_shapes=[pltpu.VMEM((RET_HEADS, HEAD_DIM, HEAD_DIM), F32)],
        compiler_params=_cparams(("parallel", "arbitrary")),
        name="retention",
    )(ret, jnp.asarray(cos), jnp.asarray(sa), jnp.asarray(sb), jnp.asarray(qd), jnp.asarray(kd),
      jnp.asarray(dmat))


_CMP_ROWS = 256


def _cmp_kernel(kr_ref, pos_ref, wbig_ref, kc_ref, vct_ref):
    n = _CMP_ROWS
    half = 2 * NSA_KV_WIDTH
    lhs = jnp.concatenate([kr_ref[...], pos_ref[...]], axis=0).astype(BF16)
    out = jnp.dot(lhs, wbig_ref[...], preferred_element_type=F32)
    lo = out[0:n, 0:half]
    hi = out[0:n, half:2 * half]
    const = out[n:n + 1, 0:half] + out[n + 1:n + 2, half:2 * half]
    kv = lo + pltpu.roll(hi, n - 1, 0) + const
    kc_ref[...] = kv[:, 0:NSA_KV_WIDTH].astype(BF16)
    vct_ref[...] = kv[:, NSA_KV_WIDTH:2 * NSA_KV_WIDTH].T.astype(BF16)


def _prep_cmp_weights(cmp_pos, cmp_w):
    L2, d, G = CMP_STRIDE, HEAD_DIM, NSA_KV_HEADS
    w5 = cmp_w.reshape(2, 2, L2, d, d)
    eye_kv = jnp.eye(2, dtype=F32)
    eye_g = jnp.eye(G, dtype=F32)
    wbig = jnp.einsum('khlde,kK,gG->lkgdhKGe', w5, eye_kv, eye_g)
    wbig = wbig.reshape(L2 * 2 * G * d, 2 * 2 * G * d).astype(BF16)
    p4 = cmp_pos.reshape(2, 2, L2, d)
    prow = jnp.broadcast_to(p4.transpose(1, 2, 0, 3)[:, :, :, None, :], (2, L2, 2, G, d))
    prow = prow.reshape(2, L2 * 2 * G * d)
    prow = jnp.pad(prow, ((0, 6), (0, 0)))
    return prow, wbig


def _compress(kvc, prow, wbig):
    B, S, _ = kvc.shape
    n, G = _CMP_ROWS, NSA_KV_HEADS
    kr = kvc.reshape(B, n, CMP_STRIDE * 2 * NSA_KV_WIDTH)
    kw = kr.shape[-1]
    return pl.pallas_call(
        _cmp_kernel,
        grid=(B,),
        in_specs=[pl.BlockSpec((None, n, kw), lambda b: (b, 0, 0)),
                  pl.BlockSpec((8, kw), lambda b: (0, 0)),
                  pl.BlockSpec(wbig.shape, lambda b: (0, 0))],
        out_specs=(pl.BlockSpec((None, n, NSA_KV_WIDTH), lambda b: (b, 0, 0)),
                   pl.BlockSpec((None, NSA_KV_WIDTH, n), lambda b: (b, 0, 0))),
        out_shape=(jax.ShapeDtypeStruct((B, n, NSA_KV_WIDTH), BF16),
                   jax.ShapeDtypeStruct((B, NSA_KV_WIDTH, n), BF16)),
        compiler_params=_cparams(("parallel",)),
        name="nsa_compress",
    )(kr, prow, wbig)


def _nsa_kernel(qt_ref, kc_ref, vct_ref, ks_ref, vst_ref, kw_ref, vwt_ref, glt_ref, ng_ref,
                selmap_ref, band_ref, out_ref, val_ref):
    TQ, KB, R, d = NSA_TQ, NSA_KB, NSA_GROUP, HEAD_DIM
    LW = R * TQ
    n_sel = selmap_ref.shape[0]
    g = pl.program_id(1)
    qi = pl.program_id(2)
    t0 = qi * TQ
    qs = jnp.concatenate([qt_ref[r * d:(r + 1) * d, :] for r in range(R)], axis=1)
    zq = jnp.zeros_like(qs)
    qs_nat = jnp.concatenate([jnp.where(g == gg, qs, zq) for gg in range(NSA_KV_HEADS)], axis=0)

    n_cmp = kc_ref.shape[0]
    tq_w = t0 + lax.rem(lax.broadcasted_iota(jnp.int32, (1, LW), 1), TQ)
    cmp_end = lax.broadcasted_iota(jnp.int32, (n_cmp, 1), 0) * CMP_STRIDE + (CMP_LEN - 1)
    valid = cmp_end <= tq_w
    s = jnp.dot(kc_ref[...], qs_nat, preferred_element_type=F32)
    s = jnp.where(valid, s, NEG)
    m = jnp.max(s, axis=0, keepdims=True)
    p = jnp.where(valid, jnp.exp2(s - m), 0.0)
    l = jnp.sum(p, axis=0, keepdims=True)
    p = p * jnp.where(l > 0.0, 1.0 / l, 0.0)
    o_c = jnp.dot(vct_ref[...], p.astype(BF16), preferred_element_type=F32)

    psum = p[:, 0:TQ]
    for r in range(1, R):
        psum = psum + p[:, r * TQ:(r + 1) * TQ]
    p_hi = psum.astype(BF16)
    p_lo = (psum - p_hi.astype(F32)).astype(BF16)
    imp = (jnp.dot(selmap_ref[...], p_hi, preferred_element_type=F32)
           + jnp.dot(selmap_ref[...], p_lo, preferred_element_type=F32))

    tq = t0 + lax.broadcasted_iota(jnp.int32, (1, TQ), 1)
    jrow = lax.broadcasted_iota(jnp.int32, (n_sel, 1), 0)
    cur = lax.shift_right_logical(tq, int(math.log2(SEL_LEN)))
    forced = (jrow == 0) | (jrow == cur) | (jrow == cur - 1)
    causal = jrow * SEL_LEN <= tq
    val = jnp.where(forced, FORCE_SCORE, jnp.where(causal, imp, -FORCE_SCORE))
    val_ref[...] = val
    n_live = 2 * qi + 2

    def rank_body(jp, cnt):
        rowv = val_ref[pl.ds(jp, 1), :]
        beats = (rowv > val) | ((rowv == val) & (jp < jrow))
        return cnt + jnp.where(beats, 1.0, 0.0)

    cnt = lax.fori_loop(0, n_live, rank_body, jnp.zeros((n_sel, TQ), F32))
    bias = jnp.where(cnt < float(N_SELECT), 0.0, NEG)
    qs_sel = jnp.concatenate([qs, jnp.concatenate([bias] * R, axis=1).astype(BF16)], axis=0)

    def attend(s, vt, carry):
        m, l, acc = carry
        m_new = jnp.maximum(m, jnp.max(s, axis=0, keepdims=True))
        alpha = jnp.exp2(m - m_new)
        p = jnp.exp2(s - m_new)
        l = alpha * l + jnp.sum(p, axis=0, keepdims=True)
        acc = alpha * acc + jnp.dot(vt, p.astype(BF16), preferred_element_type=F32)
        return m_new, l, acc

    def band_bias(delta):
        return jnp.concatenate([band_ref[delta]] * R, axis=1)

    init = (jnp.full((1, LW), NEG, F32), jnp.zeros((1, LW), F32), jnp.zeros((d, LW), F32))
    kb_diag = lax.shift_right_logical(qi, 1)

    def sel_scores(kb):
        off = pl.multiple_of(kb * KB, KB)
        s = jnp.dot(ks_ref[pl.ds(off, KB), :], qs_sel, preferred_element_type=F32)
        return s, vst_ref[:, pl.ds(off, KB)]

    def sel_body(kb, carry):
        s, vt = sel_scores(kb)
        return attend(s, vt, carry)

    carry = lax.fori_loop(0, kb_diag, sel_body, init)
    s, vt = sel_scores(kb_diag)
    _, l_s, acc_s = attend(s + band_bias(qi - 2 * kb_diag), vt, carry)

    def win_body(kb, carry):
        off = pl.multiple_of(kb * KB, KB)
        s = jnp.dot(kw_ref[pl.ds(off, KB), :], qs_nat, preferred_element_type=F32)
        return attend(s + band_bias(qi - 2 * kb), vwt_ref[:, pl.ds(off, KB)], carry)

    kb_lo = lax.shift_right_logical(jnp.maximum(qi - WINDOW // TQ, 0), 1)
    _, l_w, acc_w = lax.fori_loop(kb_lo, kb_diag + 1, win_body, init)

    gl = jax.nn.sigmoid(glt_ref[...])

    def gate(br):
        return jnp.concatenate([gl[r * N_BRANCH + br:r * N_BRANCH + br + 1, :] for r in range(R)],
                               axis=1)

    res = gate(0) * o_c + gate(1) * (acc_s / l_s) + gate(2) * (acc_w / l_w)
    res = jnp.concatenate([res[:, r * TQ:(r + 1) * TQ] for r in range(R)], axis=0)
    ng = ng_ref[...]
    out_ref[...] = res.T * (ng * jax.nn.sigmoid(ng))


def _nsa_tables(S):
    n_cmp = (S - CMP_LEN) // CMP_STRIDE + 1
    n_sel = S // SEL_LEN
    ci = np.arange(n_cmp)[:, None]
    sj = np.arange(n_sel)[None, :]
    overlap = (np.minimum(ci * CMP_STRIDE + CMP_LEN, sj * SEL_LEN + SEL_LEN)
               - np.maximum(ci * CMP_STRIDE, sj * SEL_LEN))
    sel_map = np.clip(overlap, 0, None) / CMP_STRIDE
    selmap_t = np.zeros((n_sel, _CMP_ROWS), np.float32)
    selmap_t[:, :n_cmp] = sel_map.T
    nb = WINDOW // NSA_TQ + 2
    jl = np.arange(NSA_KB)[:, None]
    tl = np.arange(NSA_TQ)[None, :]
    band = np.zeros((nb, NSA_KB, NSA_TQ), np.float32)
    for delta in range(nb):
        rel = jl - delta * NSA_TQ - tl
        band[delta] = np.where((rel <= 0) & (rel > -WINDOW), 0.0, NEG)
    return selmap_t, band


def _nsa_attention(qt, kc, vct, ks, vst, kw, vwt, glt, ng):
    B, _, S = qt.shape
    G, R, d, TQ = NSA_KV_HEADS, NSA_GROUP, HEAD_DIM, NSA_TQ
    selmap_t, band = _nsa_tables(S)
    n_sel = S // SEL_LEN
    assert NSA_KB == 2 * TQ
    assert n_sel == d
    per_b_rows = lambda n: pl.BlockSpec((None, n, G * d), lambda b, g, q: (b, 0, 0))
    per_bg_cols = lambda n: pl.BlockSpec((None, d, n), lambda b, g, q: (b, g, 0))
    return pl.pallas_call(
        _nsa_kernel,
        grid=(B, G, S // TQ),
        in_specs=[pl.BlockSpec((None, R * d, TQ), lambda b, g, q: (b, g, q)),
                  per_b_rows(_CMP_ROWS), per_bg_cols(_CMP_ROWS),
                  pl.BlockSpec((None, None, S, 2 * d), lambda b, g, q: (b, g, 0, 0)),
                  per_bg_cols(S), per_b_rows(S), per_bg_cols(S),
                  pl.BlockSpec((None, GL_ROWS, TQ), lambda b, g, q: (b, g, q)),
                  pl.BlockSpec((None, TQ, R * d), lambda b, g, q: (b, q, g)),
                  pl.BlockSpec(selmap_t.shape, lambda b, g, q: (0, 0)),
                  pl.BlockSpec(band.shape, lambda b, g, q: (0, 0, 0))],
        out_specs=pl.BlockSpec((None, TQ, R * d), lambda b, g, q: (b, q, g)),
        out_shape=jax.ShapeDtypeStruct((B, S, NSA_WIDTH), F32),
        scratch_shapes=[pltpu.VMEM((n_sel, TQ), F32)],
        compiler_params=_cparams(("parallel", "parallel", "arbitrary")),
        name="nsa_attention",
    )(qt, kc, vct, ks, vst, kw, vwt, glt, ng, jnp.asarray(selmap_t, dtype=BF16), jnp.asarray(band))


def _lru_kernel(lru_ref, cw_ref, cb_ref, wa_ref, ba_ref, wx_ref, bx_ref, lam_ref, out_ref,
                xext_ref, h_ref):
    ts, W = LRU_TS, LRU_WIDTH
    i = pl.program_id(1)

    @pl.when(i == 0)
    def _():
        xext_ref[0:8, :] = jnp.zeros((8, W), F32)
        h_ref[...] = jnp.zeros_like(h_ref)

    x = lru_ref[:, 0:W]
    gate = lru_ref[:, W:2 * W]
    xext_ref[8:8 + ts, :] = x
    xc = cb_ref[...] + cw_ref[0:1, :] * xext_ref[pl.ds(8 - (CONV_WIDTH - 1), ts), :]
    for w in range(1, CONV_WIDTH):
        xc = xc + cw_ref[w:w + 1, :] * xext_ref[pl.ds(8 - (CONV_WIDTH - 1) + w, ts), :]
    xext_ref[0:8, :] = x[ts - 8:ts, :]
    xcb = xc.astype(BF16)
    r = jax.nn.sigmoid(jnp.dot(xcb, wa_ref[...], preferred_element_type=F32) + ba_ref[...])
    gi = jax.nn.sigmoid(jnp.dot(xcb, wx_ref[...], preferred_element_type=F32) + bx_ref[...])
    nl = -lam_ref[...]
    softplus = jnp.maximum(nl, 0.0) + jnp.log1p(jnp.exp(-jnp.abs(nl)))
    log_a = (-LRU_C) * r * softplus
    a = jnp.exp(log_a)
    u = jnp.sqrt(-jnp.tanh(log_a) * (a * a + 1.0)) * (gi * xc)
    row = lax.broadcasted_iota(jnp.int32, (ts, 1), 0)
    step = 1
    while step < ts:
        keep = row >= step
        a_sh = jnp.where(keep, pltpu.roll(a, step, 0), 1.0)
        u_sh = jnp.where(keep, pltpu.roll(u, step, 0), 0.0)
        u = a * u_sh + u
        a = a * a_sh
        step *= 2
    h = u + a * h_ref[0:1, :]
    h_ref[0:1, :] = h[ts - 1:ts, :]
    out_ref[...] = h * (gate * jax.nn.sigmoid(gate))


def _block_diag(w):
    n, d, _ = w.shape
    return jnp.einsum('nde,nm->ndme', w, jnp.eye(n, dtype=w.dtype)).reshape(n * d, n * d)


def _rg_lru(lru, conv_w, conv_b, w_a, b_a, w_x, b_x, lam):
    B, S, _ = lru.shape
    ts, W = LRU_TS, LRU_WIDTH
    vec = lambda a: a.reshape(1, W)
    fixed = lambda r: pl.BlockSpec((r, W), lambda b, i: (0, 0))
    return pl.pallas_call(
        _lru_kernel,
        grid=(B, S // ts),
        in_specs=[pl.BlockSpec((None, ts, 2 * W), lambda b, i: (b, i, 0)),
                  fixed(CONV_WIDTH), fixed(1), fixed(W), fixed(1), fixed(W), fixed(1), fixed(1)],
        out_specs=pl.BlockSpec((None, ts, W), lambda b, i: (b, i, 0)),
        out_shape=jax.ShapeDtypeStruct((B, S, W), F32),
        scratch_shapes=[pltpu.VMEM((ts + 8, W), F32), pltpu.VMEM((8, W), F32)],
        compiler_params=_cparams(("parallel", "arbitrary")),
        name="rg_lru",
    )(lru, conv_w, vec(conv_b), _block_diag(w_a).astype(BF16), vec(b_a),
      _block_diag(w_x).astype(BF16), vec(b_x), vec(lam))


def _out_kernel(x_ref, yr_ref, yn_ref, yl_ref, wo_ref, g_ref, b_ref, out_ref):
    o1, o2 = RET_WIDTH, RET_WIDTH + NSA_WIDTH
    y = jnp.dot(yr_ref[...].astype(BF16), wo_ref[0:o1, :], preferred_element_type=F32)
    y = y + jnp.dot(yn_ref[...].astype(BF16), wo_ref[o1:o2, :], preferred_element_type=F32)
    y = y + jnp.dot(yl_ref[...].astype(BF16), wo_ref[o2:D_MODEL, :], preferred_element_type=F32)
    z = DEEPNORM_ALPHA * x_ref[...] + y
    mu = jnp.mean(z, axis=-1, keepdims=True)
    zc = z - mu
    var = jnp.mean(zc * zc, axis=-1, keepdims=True)
    out_ref[...] = zc * lax.rsqrt(var + LN_EPS) * g_ref[...] + b_ref[...]


def _out_proj(x, y_ret, y_nsa, y_lru, w_out, ln_g, ln_b):
    B, S, _ = x.shape
    tm = OUT_TM
    row = lambda w: pl.BlockSpec((None, tm, w), lambda b, i: (b, i, 0))
    fixed = lambda r: pl.BlockSpec((r, D_MODEL), lambda b, i: (0, 0))
    return pl.pallas_call(
        _out_kernel,
        grid=(B, S // tm),
        in_specs=[row(D_MODEL), row(RET_WIDTH), row(NSA_WIDTH), row(LRU_WIDTH),
                  fixed(D_MODEL), fixed(1), fixed(1)],
        out_specs=row(D_MODEL),
        out_shape=jax.ShapeDtypeStruct((B, S, D_MODEL), F32),
        compiler_params=_cparams(("parallel", "parallel")),
        name="out_proj_ln",
    )(x, y_ret, y_nsa, y_lru, w_out.astype(BF16), ln_g.reshape(1, D_MODEL),
      ln_b.reshape(1, D_MODEL))


def _layer(x, w_in, w_out, ln_g, ln_b, cmp_pos, cmp_w, conv_w, conv_b, w_a, b_a, w_x, b_x, lam):
    wn, wt = _prep_in_weights(w_in)
    ret, kvc, ks, kw, ng, lru, qt, vst, vwt, glt = _project(x, wn, wt)
    y_ret = _retention(ret)
    prow, wbig = _prep_cmp_weights(cmp_pos, cmp_w)
    kc, vct = _compress(kvc, prow, wbig)
    y_nsa = _nsa_attention(qt, kc, vct, ks, vst, kw, vwt, glt, ng)
    y_lru = _rg_lru(lru, conv_w, conv_b, w_a, b_a, w_x, b_x, lam)
    return _out_proj(x, y_ret, y_nsa, y_lru, w_out, ln_g, ln_b)


def kernel(x, w_in, w_out, ln_g, ln_b, nsa_cmp_pos, nsa_cmp_w, lru_conv_w, lru_conv_b,
           lru_w_a, lru_b_a, lru_w_x, lru_b_x, lru_lambda):
    assert x.shape[1] % max(PROJ_TM, RET_CHUNK, LRU_TS, OUT_TM, NSA_TQ) == 0
    assert x.shape[1] // CMP_STRIDE == _CMP_ROWS
    for l in range(DEPTH):
        x = _layer(x, w_in[l], w_out[l], ln_g[l], ln_b[l], nsa_cmp_pos[l], nsa_cmp_w[l],
                   lru_conv_w[l], lru_conv_b[l], lru_w_a[l], lru_b_a[l], lru_w_x[l], lru_b_x[l],
                   lru_lambda[l])
    return x
```

```python
import functools
import math

import jax
import jax.numpy as jnp
import numpy as np
from jax import lax
from jax.experimental import pallas as pl
from jax.experimental.pallas import tpu as pltpu

F32 = jnp.float32
BF16 = jnp.bfloat16

D_MODEL = 1024
DEPTH = 2
HEAD_DIM = 64
RET_HEADS = 4
RET_WIDTH = RET_HEADS * HEAD_DIM
ROPE_BASE = 10000.0
NSA_HEADS = 8
NSA_KV_HEADS = 2
NSA_GROUP = NSA_HEADS // NSA_KV_HEADS
NSA_WIDTH = NSA_HEADS * HEAD_DIM
NSA_KV_WIDTH = NSA_KV_HEADS * HEAD_DIM
CMP_LEN = 32
CMP_STRIDE = 16
SEL_LEN = 64
N_SELECT = 16
WINDOW = 512
N_BRANCH = 3
FORCE_SCORE = 1.0e4
LRU_WIDTH = 256
LRU_BLOCKS = 4
LRU_BLOCK_DIM = LRU_WIDTH // LRU_BLOCKS
CONV_WIDTH = 4
LRU_C = 8.0
DEEPNORM_ALPHA = (2.0 * DEPTH) ** 0.25
LN_EPS = 1e-5
NEG = -1e30
LOG2E = math.log2(math.e)
SEL_SHIFT = int(math.log2(SEL_LEN))

IN_SIZES = (RET_WIDTH,) * 4 + (NSA_WIDTH,) + (NSA_KV_WIDTH,) * 6 + (
    NSA_WIDTH, NSA_HEADS * N_BRANCH, LRU_WIDTH, LRU_WIDTH)
IN_OFFS = tuple(int(v) for v in np.cumsum((0,) + IN_SIZES))

V7X_VMEM_LIMIT = 48 * 1024 * 1024
PROJ_TM = 256
RET_CHUNK = 256
LRU_TS = 256
OUT_TM = 256
NSA_TQ = 256
NSA_KB = 256
GL_ROWS = 16

_CONTRACT_LAST = (((1,), (1,)), ((), ()))
_CONTRACT_FIRST = (((0,), (0,)), ((), ()))


def _cparams(sem):
    return pltpu.CompilerParams(dimension_semantics=sem, vmem_limit_bytes=V7X_VMEM_LIMIT)


_WN_COLS = (4 * RET_WIDTH, 2 * NSA_KV_WIDTH, NSA_KV_WIDTH, NSA_KV_WIDTH, NSA_WIDTH, 2 * LRU_WIDTH)
_WN_OFFS = tuple(int(v) for v in np.cumsum((0,) + _WN_COLS))
_WT_ROWS = (NSA_WIDTH, NSA_KV_WIDTH, NSA_KV_WIDTH, NSA_KV_HEADS * GL_ROWS)
_WT_OFFS = tuple(int(v) for v in np.cumsum((0,) + _WT_ROWS))


def _proj_kernel(x_ref, wn_ref, wt_ref, ret_ref, kvc_ref, ks_ref, kw_ref, ng_ref, lru_ref,
                 qt_ref, vst_ref, vwt_ref, glt_ref):
    xb = x_ref[...].astype(BF16)

    def nat(i):
        return jnp.dot(xb, wn_ref[:, _WN_OFFS[i]:_WN_OFFS[i + 1]], preferred_element_type=F32)

    def tr(i):
        return lax.dot_general(wt_ref[_WT_OFFS[i]:_WT_OFFS[i + 1], :], xb, _CONTRACT_LAST,
                               preferred_element_type=F32)

    ret_ref[...] = nat(0)
    kvc_ref[...] = nat(1)
    ks = nat(2)
    tm = ks.shape[0]
    key = pl.program_id(1) * tm + lax.broadcasted_iota(jnp.int32, (tm, 1), 0)
    blk = lax.broadcasted_iota(jnp.int32, (1, HEAD_DIM), 1)
    onehot = jnp.where(lax.shift_right_logical(key, SEL_SHIFT) == blk, 1.0, 0.0)
    for g in range(NSA_KV_HEADS):
        ks_ref[g] = jnp.concatenate([ks[:, g * HEAD_DIM:(g + 1) * HEAD_DIM], onehot],
                                    axis=1).astype(BF16)
    kw_ref[...] = nat(3).astype(BF16)
    ng_ref[...] = nat(4)
    lru_ref[...] = nat(5)
    qt_ref[...] = (tr(0) * LOG2E).astype(BF16)
    vst_ref[...] = tr(1).astype(BF16)
    vwt_ref[...] = tr(2).astype(BF16)
    glt_ref[...] = tr(3)


def _project(x, wn, wt):
    B, S, _ = x.shape
    tm = PROJ_TM
    nn, nt = wn.shape[1], wt.shape[0]
    G = NSA_KV_HEADS
    row = lambda w: pl.BlockSpec((None, tm, w), lambda b, i: (b, i, 0))
    col = lambda r: pl.BlockSpec((None, r, tm), lambda b, i: (b, 0, i))
    kspec = pl.BlockSpec((None, G, tm, 2 * HEAD_DIM), lambda b, i: (b, 0, i, 0))
    out_shape = (
        jax.ShapeDtypeStruct((B, S, 4 * RET_WIDTH), F32),
        jax.ShapeDtypeStruct((B, S, 2 * NSA_KV_WIDTH), F32),
        jax.ShapeDtypeStruct((B, G, S, 2 * HEAD_DIM), BF16),
        jax.ShapeDtypeStruct((B, S, NSA_KV_WIDTH), BF16),
        jax.ShapeDtypeStruct((B, S, NSA_WIDTH), F32),
        jax.ShapeDtypeStruct((B, S, 2 * LRU_WIDTH), F32),
        jax.ShapeDtypeStruct((B, NSA_WIDTH, S), BF16),
        jax.ShapeDtypeStruct((B, NSA_KV_WIDTH, S), BF16),
        jax.ShapeDtypeStruct((B, NSA_KV_WIDTH, S), BF16),
        jax.ShapeDtypeStruct((B, G * GL_ROWS, S), F32),
    )
    return pl.pallas_call(
        _proj_kernel,
        grid=(B, S // tm),
        in_specs=[row(D_MODEL),
                  pl.BlockSpec((D_MODEL, nn), lambda b, i: (0, 0)),
                  pl.BlockSpec((nt, D_MODEL), lambda b, i: (0, 0))],
        out_specs=(row(4 * RET_WIDTH), row(2 * NSA_KV_WIDTH), kspec, row(NSA_KV_WIDTH), row(NSA_WIDTH),
                   row(2 * LRU_WIDTH), col(NSA_WIDTH), col(NSA_KV_WIDTH), col(NSA_KV_WIDTH),
                   col(G * GL_ROWS)),
        out_shape=out_shape,
        compiler_params=_cparams(("parallel", "parallel")),
        name="in_proj",
    )(x, wn, wt)


def _prep_in_weights(w_in):
    seg = [w_in[:, IN_OFFS[i]:IN_OFFS[i + 1]] for i in range(len(IN_SIZES))]
    (rq, rk, rv, rg, nq, nkc, nvc, nks, nvs, nkw, nvw, ng, ngl, lx, lg) = seg
    wn = jnp.concatenate([rq, rk, rv, rg, nkc, nvc, nks, nkw, ng, lx, lg], axis=1).astype(BF16)
    glt = ngl.T.reshape(NSA_KV_HEADS, NSA_GROUP * N_BRANCH, D_MODEL)
    glt = jnp.pad(glt, ((0, 0), (0, GL_ROWS - NSA_GROUP * N_BRANCH), (0, 0)))
    wt = jnp.concatenate([nq.T * (HEAD_DIM ** -0.5), nvs.T, nvw.T,
                          glt.reshape(NSA_KV_HEADS * GL_ROWS, D_MODEL)], axis=0).astype(BF16)
    return wn, wt


def _ret_kernel(ret_ref, cos_ref, sa_ref, sb_ref, qd_ref, kd_ref, dmat_ref, out_ref, state_ref,
                *, chunk_decay):
    c = pl.program_id(1)

    @pl.when(c == 0)
    def _():
        state_ref[...] = jnp.zeros_like(state_ref)

    W = RET_WIDTH
    q = ret_ref[:, 0:W]
    k = ret_ref[:, W:2 * W]
    v = ret_ref[:, 2 * W:3 * W]
    gate = ret_ref[:, 3 * W:4 * W]
    cos, sa, sb = cos_ref[...], sa_ref[...], sb_ref[...]
    half = HEAD_DIM // 2

    def rope(t):
        return t * cos + pltpu.roll(t, W - half, 1) * sa + pltpu.roll(t, half, 1) * sb

    qr = rope(q)
    kr = rope(k) * (HEAD_DIM ** -0.5)
    qd = qr * qd_ref[...]
    kd = kr * kd_ref[...]
    ys = []
    for h in range(RET_HEADS):
        sl = slice(h * HEAD_DIM, (h + 1) * HEAD_DIM)
        qb, kb, vb = qr[:, sl].astype(BF16), kr[:, sl].astype(BF16), v[:, sl].astype(BF16)
        s = lax.dot_general(qb, kb, _CONTRACT_LAST, preferred_element_type=F32) * dmat_ref[h]
        inner = jnp.dot(s.astype(BF16), vb, preferred_element_type=F32)
        st = state_ref[h]
        cross = jnp.dot(qd[:, sl].astype(BF16), st.astype(BF16), preferred_element_type=F32)
        y = inner + cross
        mu = jnp.mean(y, axis=-1, keepdims=True)
        yc = y - mu
        var = jnp.mean(yc * yc, axis=-1, keepdims=True)
        ys.append(yc * lax.rsqrt(var + LN_EPS))
        kv = lax.dot_general(kd[:, sl].astype(BF16), vb, _CONTRACT_FIRST,
                             preferred_element_type=F32)
        state_ref[h] = st * chunk_decay[h] + kv
    y = jnp.concatenate(ys, axis=1)
    out_ref[...] = y * (gate * jax.nn.sigmoid(gate))


def _ret_tables(S):
    C, H, d = RET_CHUNK, RET_HEADS, HEAD_DIM
    half = d // 2
    inv = 1.0 / (ROPE_BASE ** (np.arange(half, dtype=np.float64) / half))
    ang = np.arange(S, dtype=np.float64)[:, None] * inv[None, :]
    cos_h = np.concatenate([np.cos(ang), np.cos(ang)], axis=1)
    sin_lo = np.concatenate([-np.sin(ang), np.zeros_like(ang)], axis=1)
    sin_hi = np.concatenate([np.zeros_like(ang), np.sin(ang)], axis=1)
    tile = lambda a: np.tile(a, (1, H)).astype(np.float32)
    log_g = np.log(1.0 - 2.0 ** (-5.0 - np.arange(H, dtype=np.float64)))
    idx = np.arange(C, dtype=np.float64)
    diff = idx[:, None] - idx[None, :]
    dmat = np.where(diff >= 0, np.exp(np.maximum(diff, 0.0)[None] * log_g[:, None, None]), 0.0)
    kd = np.exp((C - 1.0 - idx)[:, None] * log_g[None, :])
    qd = np.exp((idx + 1.0)[:, None] * log_g[None, :])
    rep = lambda a: np.repeat(a, d, axis=1).astype(np.float32)
    chunk_decay = tuple(float(v) for v in np.exp(C * log_g))
    return (tile(cos_h), tile(sin_lo), tile(sin_hi), rep(qd), rep(kd),
            dmat.astype(np.float32), chunk_decay)


def _retention(ret):
    B, S, _ = ret.shape
    C, W = RET_CHUNK, RET_WIDTH
    cos, sa, sb, qd, kd, dmat, chunk_decay = _ret_tables(S)
    pos = pl.BlockSpec((C, W), lambda b, c: (c, 0))
    fixed = pl.BlockSpec((C, W), lambda b, c: (0, 0))
    return pl.pallas_call(
        functools.partial(_ret_kernel, chunk_decay=chunk_decay),
        grid=(B, S // C),
        in_specs=[pl.BlockSpec((None, C, 4 * W), lambda b, c: (b, c, 0)), pos, pos, pos,
                  fixed, fixed, pl.BlockSpec((RET_HEADS, C, C), lambda b, c: (0, 0, 0))],
        out_specs=pl.BlockSpec((None, C, W), lambda b, c: (b, c, 0)),
        out_shape=jax.ShapeDtypeStruct((B, S, W), F32),
        scratch_shapes=[pltpu.VMEM((RET_HEADS, HEAD_DIM, HEAD_DIM), F32)],
        compiler_params=_cparams(("parallel", "arbitrary")),
        name="retention",
    )(ret, jnp.asarray(cos), jnp.asarray(sa), jnp.asarray(sb), jnp.asarray(qd), jnp.asarray(kd),
      jnp.asarray(dmat))


_CMP_ROWS = 256


def _cmp_kernel(kr_ref, pos_ref, wbig_ref, kc_ref, vct_ref):
    n = _CMP_ROWS
    half = 2 * NSA_KV_WIDTH
    lhs = jnp.concatenate([kr_ref[...], pos_ref[...]], axis=0).astype(BF16)
    out = jnp.dot(lhs, wbig_ref[...], preferred_element_type=F32)
    lo = out[0:n, 0:half]
    hi = out[0:n, half:2 * half]
    const = out[n:n + 1, 0:half] + out[n + 1:n + 2, half:2 * half]
    kv = lo + pltpu.roll(hi, n - 1, 0) + const
    kc_ref[...] = kv[:, 0:NSA_KV_WIDTH].astype(BF16)
    vct_ref[...] = kv[:, NSA_KV_WIDTH:2 * NSA_KV_WIDTH].T.astype(BF16)


def _prep_cmp_weights(cmp_pos, cmp_w):
    L2, d, G = CMP_STRIDE, HEAD_DIM, NSA_KV_HEADS
    w5 = cmp_w.reshape(2, 2, L2, d, d)
    eye_kv = jnp.eye(2, dtype=F32)
    eye_g = jnp.eye(G, dtype=F32)
    wbig = jnp.einsum('khlde,kK,gG->lkgdhKGe', w5, eye_kv, eye_g)
    wbig = wbig.reshape(L2 * 2 * G * d, 2 * 2 * G * d).astype(BF16)
    p4 = cmp_pos.reshape(2, 2, L2, d)
    prow = jnp.broadcast_to(p4.transpose(1, 2, 0, 3)[:, :, :, None, :], (2, L2, 2, G, d))
    prow = prow.reshape(2, L2 * 2 * G * d)
    prow = jnp.pad(prow, ((0, 6), (0, 0)))
    return prow, wbig


def _compress(kvc, prow, wbig):
    B, S, _ = kvc.shape
    n, G = _CMP_ROWS, NSA_KV_HEADS
    kr = kvc.reshape(B, n, CMP_STRIDE * 2 * NSA_KV_WIDTH)
    kw = kr.shape[-1]
    return pl.pallas_call(
        _cmp_kernel,
        grid=(B,),
        in_specs=[pl.BlockSpec((None, n, kw), lambda b: (b, 0, 0)),
                  pl.BlockSpec((8, kw), lambda b: (0, 0)),
                  pl.BlockSpec(wbig.shape, lambda b: (0, 0))],
        out_specs=(pl.BlockSpec((None, n, NSA_KV_WIDTH), lambda b: (b, 0, 0)),
                   pl.BlockSpec((None, NSA_KV_WIDTH, n), lambda b: (b, 0, 0))),
        out_shape=(jax.ShapeDtypeStruct((B, n, NSA_KV_WIDTH), BF16),
                   jax.ShapeDtypeStruct((B, NSA_KV_WIDTH, n), BF16)),
        compiler_params=_cparams(("parallel",)),
        name="nsa_compress",
    )(kr, prow, wbig)


def _nsa_kernel(qt_ref, kc_ref, vct_ref, ks_ref, vst_ref, kw_ref, vwt_ref, glt_ref, ng_ref,
                selmap_ref, band_ref, out_ref, val_ref, s_ref, p_ref, acc_ref, st_ref):
    TQ, KB, R, d = NSA_TQ, NSA_KB, NSA_GROUP, HEAD_DIM
    LW = R * TQ
    n_sel = selmap_ref.shape[0]
    g = pl.program_id(1)
    qi = pl.program_id(2)
    t0 = qi * TQ
    qs = jnp.concatenate([qt_ref[r * d:(r + 1) * d, :] for r in range(R)], axis=1)
    zq = jnp.zeros_like(qs)
    qs_nat = jnp.concatenate([jnp.where(g == gg, qs, zq) for gg in range(NSA_KV_HEADS)], axis=0)

    n_cmp = kc_ref.shape[0]
    tq_w = t0 + lax.rem(lax.broadcasted_iota(jnp.int32, (1, LW), 1), TQ)
    cmp_end = lax.broadcasted_iota(jnp.int32, (n_cmp, 1), 0) * CMP_STRIDE + (CMP_LEN - 1)
    valid = cmp_end <= tq_w
    s = jnp.dot(kc_ref[...], qs_nat, preferred_element_type=F32)
    s = jnp.where(valid, s, NEG)
    m = jnp.max(s, axis=0, keepdims=True)
    p = jnp.where(valid, jnp.exp2(s - m), 0.0)
    l = jnp.sum(p, axis=0, keepdims=True)
    p = p * jnp.where(l > 0.0, 1.0 / l, 0.0)
    o_c = jnp.dot(vct_ref[...], p.astype(BF16), preferred_element_type=F32)

    psum = p[:, 0:TQ]
    for r in range(1, R):
        psum = psum + p[:, r * TQ:(r + 1) * TQ]
    p_hi = psum.astype(BF16)
    p_lo = (psum - p_hi.astype(F32)).astype(BF16)
    imp = (jnp.dot(selmap_ref[...], p_hi, preferred_element_type=F32)
           + jnp.dot(selmap_ref[...], p_lo, preferred_element_type=F32))

    tq = t0 + lax.broadcasted_iota(jnp.int32, (1, TQ), 1)
    jrow = lax.broadcasted_iota(jnp.int32, (n_sel, 1), 0)
    cur = lax.shift_right_logical(tq, int(math.log2(SEL_LEN)))
    forced = (jrow == 0) | (jrow == cur) | (jrow == cur - 1)
    causal = jrow * SEL_LEN <= tq
    val = jnp.where(forced, FORCE_SCORE, jnp.where(causal, imp, -FORCE_SCORE))
    val_ref[...] = val
    n_live = (qi + 1) * (TQ // SEL_LEN)

    def rank_body(jp, cnt):
        rowv = val_ref[pl.ds(jp, 1), :]
        beats = (rowv > val) | ((rowv == val) & (jp < jrow))
        return cnt + jnp.where(beats, 1.0, 0.0)

    cnt = lax.fori_loop(0, n_live, rank_body, jnp.zeros((n_sel, TQ), F32))
    bias = jnp.where(cnt < float(N_SELECT), 0.0, NEG)
    qs_sel = jnp.concatenate([qs, jnp.concatenate([bias] * R, axis=1).astype(BF16)], axis=0)

    causal_bias = jnp.concatenate([band_ref[0]] * R, axis=1)
    start_bias = jnp.concatenate([band_ref[1]] * R, axis=1)

    def key_off(kb):
        return pl.multiple_of(kb * KB, KB)

    n_wb = WINDOW // KB + 1
    w_scores, w_vt = [], []
    for w in range(n_wb):
        kb = qi - (n_wb - 1) + w
        off = key_off(jnp.maximum(kb, 0))
        s = jnp.dot(kw_ref[pl.ds(off, KB), :], qs_nat, preferred_element_type=F32)
        if w == 0:
            s = s + start_bias
        elif w == n_wb - 1:
            s = s + causal_bias
        if w < n_wb - 1:
            s = s + jnp.where(kb >= 0, 0.0, NEG)
        w_scores.append(s)
        w_vt.append(vwt_ref[:, pl.ds(off, KB)])
    m_w = functools.reduce(jnp.maximum, [jnp.max(s, axis=0, keepdims=True) for s in w_scores])
    w_p = [jnp.exp2(s - m_w) for s in w_scores]
    l_w = functools.reduce(jnp.add, [jnp.sum(p, axis=0, keepdims=True) for p in w_p])
    acc_w = functools.reduce(jnp.add, [jnp.dot(vt, p.astype(BF16), preferred_element_type=F32)
                                       for vt, p in zip(w_vt, w_p)])

    def sel_scores(kb):
        return jnp.dot(ks_ref[pl.ds(key_off(kb), KB), :], qs_sel, preferred_element_type=F32)

    def sel_values(kb, p):
        return jnp.dot(vst_ref[:, pl.ds(key_off(kb), KB)], p, preferred_element_type=F32)

    ROW_M, ROW_L, ROW_A = 0, 1, 2

    def sel_stage(i, cur, *, diag):
        nxt = 1 - cur
        if not diag:
            s_ref[nxt] = sel_scores(i + 1)
        pv_prev = sel_values(jnp.maximum(i - 1, 0), p_ref[nxt])
        s = s_ref[cur]
        if diag:
            s = s + causal_bias
        m = st_ref[ROW_M:ROW_M + 1, :]
        m_new = jnp.maximum(m, jnp.max(s, axis=0, keepdims=True))
        alpha = jnp.exp2(m - m_new)
        p = jnp.exp2(s - m_new)
        p_ref[cur] = p.astype(BF16)
        acc_ref[...] = st_ref[ROW_A:ROW_A + 1, :] * acc_ref[...] + pv_prev
        st_ref[ROW_M:ROW_M + 1, :] = m_new
        st_ref[ROW_L:ROW_L + 1, :] = (alpha * st_ref[ROW_L:ROW_L + 1, :]
                                      + jnp.sum(p, axis=0, keepdims=True))
        st_ref[ROW_A:ROW_A + 1, :] = alpha

    def sel_finish(i, cur):
        acc_ref[...] = st_ref[ROW_A:ROW_A + 1, :] * acc_ref[...] + sel_values(i, p_ref[cur])

    s_ref[0] = sel_scores(0)
    p_ref[1] = jnp.zeros((KB, LW), BF16)
    acc_ref[...] = jnp.zeros((d, LW), F32)
    st_ref[ROW_M:ROW_M + 1, :] = jnp.full((1, LW), NEG, F32)
    st_ref[ROW_L:ROW_L + 1, :] = jnp.zeros((1, LW), F32)
    st_ref[ROW_A:ROW_A + 1, :] = jnp.ones((1, LW), F32)

    def sel_pair(j, c):
        sel_stage(2 * j, 0, diag=False)
        sel_stage(2 * j + 1, 1, diag=False)
        return c

    lax.fori_loop(0, lax.shift_right_logical(qi, 1), sel_pair, 0)
    odd = lax.rem(qi, 2) == 1

    @pl.when(odd)
    def _():
        sel_stage(qi - 1, 0, diag=False)
        sel_stage(qi, 1, diag=True)
        sel_finish(qi, 1)

    @pl.when(jnp.logical_not(odd))
    def _():
        sel_stage(qi, 0, diag=True)
        sel_finish(qi, 0)

    acc_s = acc_ref[...]
    l_s = st_ref[ROW_L:ROW_L + 1, :]

    gl = jax.nn.sigmoid(glt_ref[...])

    def gate(br):
        return jnp.concatenate([gl[r * N_BRANCH + br:r * N_BRANCH + br + 1, :] for r in range(R)],
                               axis=1)

    res = gate(0) * o_c + gate(1) * (acc_s / l_s) + gate(2) * (acc_w / l_w)
    res = jnp.concatenate([res[:, r * TQ:(r + 1) * TQ] for r in range(R)], axis=0)
    ng = ng_ref[...]
    out_ref[...] = res.T * (ng * jax.nn.sigmoid(ng))


def _nsa_tables(S):
    n_cmp = (S - CMP_LEN) // CMP_STRIDE + 1
    n_sel = S // SEL_LEN
    ci = np.arange(n_cmp)[:, None]
    sj = np.arange(n_sel)[None, :]
    overlap = (np.minimum(ci * CMP_STRIDE + CMP_LEN, sj * SEL_LEN + SEL_LEN)
               - np.maximum(ci * CMP_STRIDE, sj * SEL_LEN))
    sel_map = np.clip(overlap, 0, None) / CMP_STRIDE
    selmap_t = np.zeros((n_sel, _CMP_ROWS), np.float32)
    selmap_t[:, :n_cmp] = sel_map.T
    jl = np.arange(NSA_KB)[:, None]
    tl = np.arange(NSA_TQ)[None, :]
    band = np.stack([np.where(jl <= tl, 0.0, NEG), np.where(jl > tl, 0.0, NEG)]).astype(np.float32)
    return selmap_t, band


def _nsa_attention(qt, kc, vct, ks, vst, kw, vwt, glt, ng):
    B, _, S = qt.shape
    G, R, d, TQ = NSA_KV_HEADS, NSA_GROUP, HEAD_DIM, NSA_TQ
    selmap_t, band = _nsa_tables(S)
    n_sel = S // SEL_LEN
    assert NSA_KB == TQ and WINDOW % NSA_KB == 0
    assert n_sel == d
    per_b_rows = lambda n: pl.BlockSpec((None, n, G * d), lambda b, g, q: (b, 0, 0))
    per_bg_cols = lambda n: pl.BlockSpec((None, d, n), lambda b, g, q: (b, g, 0))
    return pl.pallas_call(
        _nsa_kernel,
        grid=(B, G, S // TQ),
        in_specs=[pl.BlockSpec((None, R * d, TQ), lambda b, g, q: (b, g, q)),
                  per_b_rows(_CMP_ROWS), per_bg_cols(_CMP_ROWS),
                  pl.BlockSpec((None, None, S, 2 * d), lambda b, g, q: (b, g, 0, 0)),
                  per_bg_cols(S), per_b_rows(S), per_bg_cols(S),
                  pl.BlockSpec((None, GL_ROWS, TQ), lambda b, g, q: (b, g, q)),
                  pl.BlockSpec((None, TQ, R * d), lambda b, g, q: (b, q, g)),
                  pl.BlockSpec(selmap_t.shape, lambda b, g, q: (0, 0)),
                  pl.BlockSpec(band.shape, lambda b, g, q: (0, 0, 0))],
        out_specs=pl.BlockSpec((None, TQ, R * d), lambda b, g, q: (b, q, g)),
        out_shape=jax.ShapeDtypeStruct((B, S, NSA_WIDTH), F32),
        scratch_shapes=[pltpu.VMEM((n_sel, TQ), F32),
                        pltpu.VMEM((2, NSA_KB, R * TQ), F32),
                        pltpu.VMEM((2, NSA_KB, R * TQ), BF16),
                        pltpu.VMEM((d, R * TQ), F32),
                        pltpu.VMEM((8, R * TQ), F32)],
        compiler_params=_cparams(("parallel", "parallel", "arbitrary")),
        name="nsa_attention",
    )(qt, kc, vct, ks, vst, kw, vwt, glt, ng, jnp.asarray(selmap_t, dtype=BF16), jnp.asarray(band))


def _lru_kernel(lru_ref, cw_ref, cb_ref, wa_ref, ba_ref, wx_ref, bx_ref, lam_ref, out_ref,
                xext_ref, h_ref):
    ts, W = LRU_TS, LRU_WIDTH
    i = pl.program_id(1)

    @pl.when(i == 0)
    def _():
        xext_ref[0:8, :] = jnp.zeros((8, W), F32)
        h_ref[...] = jnp.zeros_like(h_ref)

    x = lru_ref[:, 0:W]
    gate = lru_ref[:, W:2 * W]
    xext_ref[8:8 + ts, :] = x
    xc = cb_ref[...] + cw_ref[0:1, :] * xext_ref[pl.ds(8 - (CONV_WIDTH - 1), ts), :]
    for w in range(1, CONV_WIDTH):
        xc = xc + cw_ref[w:w + 1, :] * xext_ref[pl.ds(8 - (CONV_WIDTH - 1) + w, ts), :]
    xext_ref[0:8, :] = x[ts - 8:ts, :]
    xcb = xc.astype(BF16)
    r = jax.nn.sigmoid(jnp.dot(xcb, wa_ref[...], preferred_element_type=F32) + ba_ref[...])
    gi = jax.nn.sigmoid(jnp.dot(xcb, wx_ref[...], preferred_element_type=F32) + bx_ref[...])
    nl = -lam_ref[...]
    softplus = jnp.maximum(nl, 0.0) + jnp.log1p(jnp.exp(-jnp.abs(nl)))
    log_a = (-LRU_C) * r * softplus
    a = jnp.exp(log_a)
    u = jnp.sqrt(-jnp.tanh(log_a) * (a * a + 1.0)) * (gi * xc)
    row = lax.broadcasted_iota(jnp.int32, (ts, 1), 0)
    step = 1
    while step < ts:
        keep = row >= step
        a_sh = jnp.where(keep, pltpu.roll(a, step, 0), 1.0)
        u_sh = jnp.where(keep, pltpu.roll(u, step, 0), 0.0)
        u = a * u_sh + u
        a = a * a_sh
        step *= 2
    h = u + a * h_ref[0:1, :]
    h_ref[0:1, :] = h[ts - 1:ts, :]
    out_ref[...] = h * (gate * jax.nn.sigmoid(gate))


def _block_diag(w):
    n, d, _ = w.shape
    return jnp.einsum('nde,nm->ndme', w, jnp.eye(n, dtype=w.dtype)).reshape(n * d, n * d)


def _rg_lru(lru, conv_w, conv_b, w_a, b_a, w_x, b_x, lam):
    B, S, _ = lru.shape
    ts, W = LRU_TS, LRU_WIDTH
    vec = lambda a: a.reshape(1, W)
    fixed = lambda r: pl.BlockSpec((r, W), lambda b, i: (0, 0))
    return pl.pallas_call(
        _lru_kernel,
        grid=(B, S // ts),
        in_specs=[pl.BlockSpec((None, ts, 2 * W), lambda b, i: (b, i, 0)),
                  fixed(CONV_WIDTH), fixed(1), fixed(W), fixed(1), fixed(W), fixed(1), fixed(1)],
        out_specs=pl.BlockSpec((None, ts, W), lambda b, i: (b, i, 0)),
        out_shape=jax.ShapeDtypeStruct((B, S, W), F32),
        scratch_shapes=[pltpu.VMEM((ts + 8, W), F32), pltpu.VMEM((8, W), F32)],
        compiler_params=_cparams(("parallel", "arbitrary")),
        name="rg_lru",
    )(lru, conv_w, vec(conv_b), _block_diag(w_a).astype(BF16), vec(b_a),
      _block_diag(w_x).astype(BF16), vec(b_x), vec(lam))


def _out_kernel(x_ref, yr_ref, yn_ref, yl_ref, wo_ref, g_ref, b_ref, out_ref):
    o1, o2 = RET_WIDTH, RET_WIDTH + NSA_WIDTH
    y = jnp.dot(yr_ref[...].astype(BF16), wo_ref[0:o1, :], preferred_element_type=F32)
    y = y + jnp.dot(yn_ref[...].astype(BF16), wo_ref[o1:o2, :], preferred_element_type=F32)
    y = y + jnp.dot(yl_ref[...].astype(BF16), wo_ref[o2:D_MODEL, :], preferred_element_type=F32)
    z = DEEPNORM_ALPHA * x_ref[...] + y
    mu = jnp.mean(z, axis=-1, keepdims=True)
    zc = z - mu
    var = jnp.mean(zc * zc, axis=-1, keepdims=True)
    out_ref[...] = zc * lax.rsqrt(var + LN_EPS) * g_ref[...] + b_ref[...]


def _out_proj(x, y_ret, y_nsa, y_lru, w_out, ln_g, ln_b):
    B, S, _ = x.shape
    tm = OUT_TM
    row = lambda w: pl.BlockSpec((None, tm, w), lambda b, i: (b, i, 0))
    fixed = lambda r: pl.BlockSpec((r, D_MODEL), lambda b, i: (0, 0))
    return pl.pallas_call(
        _out_kernel,
        grid=(B, S // tm),
        in_specs=[row(D_MODEL), row(RET_WIDTH), row(NSA_WIDTH), row(LRU_WIDTH),
                  fixed(D_MODEL), fixed(1), fixed(1)],
        out_specs=row(D_MODEL),
        out_shape=jax.ShapeDtypeStruct((B, S, D_MODEL), F32),
        compiler_params=_cparams(("parallel", "parallel")),
        name="out_proj_ln",
    )(x, y_ret, y_nsa, y_lru, w_out.astype(BF16), ln_g.reshape(1, D_MODEL),
      ln_b.reshape(1, D_MODEL))


def _layer(x, w_in, w_out, ln_g, ln_b, cmp_pos, cmp_w, conv_w, conv_b, w_a, b_a, w_x, b_x, lam):
    wn, wt = _prep_in_weights(w_in)
    ret, kvc, ks, kw, ng, lru, qt, vst, vwt, glt = _project(x, wn, wt)
    y_ret = _retention(ret)
    prow, wbig = _prep_cmp_weights(cmp_pos, cmp_w)
    kc, vct = _compress(kvc, prow, wbig)
    y_nsa = _nsa_attention(qt, kc, vct, ks, vst, kw, vwt, glt, ng)
    y_lru = _rg_lru(lru, conv_w, conv_b, w_a, b_a, w_x, b_x, lam)
    return _out_proj(x, y_ret, y_nsa, y_lru, w_out, ln_g, ln_b)


def kernel(x, w_in, w_out, ln_g, ln_b, nsa_cmp_pos, nsa_cmp_w, lru_conv_w, lru_conv_b,
           lru_w_a, lru_b_a, lru_w_x, lru_b_x, lru_lambda):
    assert x.shape[1] % max(PROJ_TM, RET_CHUNK, LRU_TS, OUT_TM, NSA_TQ) == 0
    assert x.shape[1] // CMP_STRIDE == _CMP_ROWS
    for l in range(DEPTH):
        x = _layer(x, w_in[l], w_out[l], ln_g[l], ln_b[l], nsa_cmp_pos[l], nsa_cmp_w[l],
                   lru_conv_w[l], lru_conv_b[l], lru_w_a[l], lru_b_a[l], lru_w_x[l], lru_b_x[l],
                   lru_lambda[l])
    return x
```

```python
import functools
import math

import jax
import jax.numpy as jnp
import numpy as np
from jax import lax
from jax.experimental import pallas as pl
from jax.experimental.pallas import tpu as pltpu

F32 = jnp.float32
BF16 = jnp.bfloat16

D_MODEL = 1024
DEPTH = 2
HEAD_DIM = 64
RET_HEADS = 4
RET_WIDTH = RET_HEADS * HEAD_DIM
ROPE_BASE = 10000.0
NSA_HEADS = 8
NSA_KV_HEADS = 2
NSA_GROUP = NSA_HEADS // NSA_KV_HEADS
NSA_WIDTH = NSA_HEADS * HEAD_DIM
NSA_KV_WIDTH = NSA_KV_HEADS * HEAD_DIM
CMP_LEN = 32
CMP_STRIDE = 16
SEL_LEN = 64
N_SELECT = 16
WINDOW = 512
N_BRANCH = 3
FORCE_SCORE = 1.0e4
LRU_WIDTH = 256
LRU_BLOCKS = 4
LRU_BLOCK_DIM = LRU_WIDTH // LRU_BLOCKS
CONV_WIDTH = 4
LRU_C = 8.0
DEEPNORM_ALPHA = (2.0 * DEPTH) ** 0.25
LN_EPS = 1e-5
NEG = -1e30
LOG2E = math.log2(math.e)
SEL_SHIFT = int(math.log2(SEL_LEN))

IN_SIZES = (RET_WIDTH,) * 4 + (NSA_WIDTH,) + (NSA_KV_WIDTH,) * 6 + (
    NSA_WIDTH, NSA_HEADS * N_BRANCH, LRU_WIDTH, LRU_WIDTH)
IN_OFFS = tuple(int(v) for v in np.cumsum((0,) + IN_SIZES))

V7X_VMEM_LIMIT = 48 * 1024 * 1024
SUBLANES = 8
PROJ_TM = 256
RET_CHUNK = 256
LRU_TS = 256
OUT_TM = 256
NSA_TQ = 256
NSA_KB = 256
GL_ROWS = 16

_CONTRACT_LAST = (((1,), (1,)), ((), ()))
_CONTRACT_FIRST = (((0,), (0,)), ((), ()))


def _cparams(sem):
    return pltpu.CompilerParams(dimension_semantics=sem, vmem_limit_bytes=V7X_VMEM_LIMIT)


_WN_COLS = (4 * RET_WIDTH, 2 * NSA_KV_WIDTH, NSA_KV_WIDTH, NSA_KV_WIDTH, NSA_WIDTH, 2 * LRU_WIDTH)
_WN_OFFS = tuple(int(v) for v in np.cumsum((0,) + _WN_COLS))
_WT_ROWS = (NSA_WIDTH, NSA_KV_WIDTH, NSA_KV_WIDTH, NSA_KV_HEADS * GL_ROWS)
_WT_OFFS = tuple(int(v) for v in np.cumsum((0,) + _WT_ROWS))


def _proj_kernel(x_ref, wn_ref, wt_ref, ret_ref, kvc_ref, ks_ref, kw_ref, ng_ref, lru_ref,
                 qt_ref, vst_ref, vwt_ref, glt_ref):
    xb = x_ref[...].astype(BF16)

    def nat(i):
        return jnp.dot(xb, wn_ref[:, _WN_OFFS[i]:_WN_OFFS[i + 1]], preferred_element_type=F32)

    def tr(i):
        return lax.dot_general(wt_ref[_WT_OFFS[i]:_WT_OFFS[i + 1], :], xb, _CONTRACT_LAST,
                               preferred_element_type=F32)

    ret_ref[...] = nat(0)
    kvc_ref[...] = nat(1)
    ks = nat(2)
    tm = ks.shape[0]
    key = pl.program_id(1) * tm + lax.broadcasted_iota(jnp.int32, (tm, 1), 0)
    blk = lax.broadcasted_iota(jnp.int32, (1, HEAD_DIM), 1)
    onehot = jnp.where(lax.shift_right_logical(key, SEL_SHIFT) == blk, 1.0, 0.0)
    for g in range(NSA_KV_HEADS):
        ks_ref[g] = jnp.concatenate([ks[:, g * HEAD_DIM:(g + 1) * HEAD_DIM], onehot],
                                    axis=1).astype(BF16)
    kw_ref[...] = nat(3).astype(BF16)
    ng_ref[...] = nat(4)
    lru_ref[...] = nat(5)
    qt_ref[...] = (tr(0) * LOG2E).astype(BF16)
    vst_ref[...] = tr(1).astype(BF16)
    vwt_ref[...] = tr(2).astype(BF16)
    glt_ref[...] = tr(3)


def _project(x, wn, wt):
    B, S, _ = x.shape
    tm = PROJ_TM
    nn, nt = wn.shape[1], wt.shape[0]
    G = NSA_KV_HEADS
    row = lambda w: pl.BlockSpec((None, tm, w), lambda b, i: (b, i, 0))
    col = lambda r: pl.BlockSpec((None, r, tm), lambda b, i: (b, 0, i))
    kspec = pl.BlockSpec((None, G, tm, 2 * HEAD_DIM), lambda b, i: (b, 0, i, 0))
    out_shape = (
        jax.ShapeDtypeStruct((B, S, 4 * RET_WIDTH), F32),
        jax.ShapeDtypeStruct((B, S, 2 * NSA_KV_WIDTH), F32),
        jax.ShapeDtypeStruct((B, G, S, 2 * HEAD_DIM), BF16),
        jax.ShapeDtypeStruct((B, S, NSA_KV_WIDTH), BF16),
        jax.ShapeDtypeStruct((B, S, NSA_WIDTH), F32),
        jax.ShapeDtypeStruct((B, S, 2 * LRU_WIDTH), F32),
        jax.ShapeDtypeStruct((B, NSA_WIDTH, S), BF16),
        jax.ShapeDtypeStruct((B, NSA_KV_WIDTH, S), BF16),
        jax.ShapeDtypeStruct((B, NSA_KV_WIDTH, S), BF16),
        jax.ShapeDtypeStruct((B, G * GL_ROWS, S), F32),
    )
    return pl.pallas_call(
        _proj_kernel,
        grid=(B, S // tm),
        in_specs=[row(D_MODEL),
                  pl.BlockSpec((D_MODEL, nn), lambda b, i: (0, 0)),
                  pl.BlockSpec((nt, D_MODEL), lambda b, i: (0, 0))],
        out_specs=(row(4 * RET_WIDTH), row(2 * NSA_KV_WIDTH), kspec, row(NSA_KV_WIDTH), row(NSA_WIDTH),
                   row(2 * LRU_WIDTH), col(NSA_WIDTH), col(NSA_KV_WIDTH), col(NSA_KV_WIDTH),
                   col(G * GL_ROWS)),
        out_shape=out_shape,
        compiler_params=_cparams(("parallel", "parallel")),
        name="in_proj",
    )(x, wn, wt)


def _prep_in_weights(w_in):
    seg = [w_in[:, IN_OFFS[i]:IN_OFFS[i + 1]] for i in range(len(IN_SIZES))]
    (rq, rk, rv, rg, nq, nkc, nvc, nks, nvs, nkw, nvw, ng, ngl, lx, lg) = seg
    wn = jnp.concatenate([rq, rk, rv, rg, nkc, nvc, nks, nkw, ng, lx, lg], axis=1).astype(BF16)
    glt = ngl.T.reshape(NSA_KV_HEADS, NSA_GROUP * N_BRANCH, D_MODEL)
    glt = jnp.pad(glt, ((0, 0), (0, GL_ROWS - NSA_GROUP * N_BRANCH), (0, 0)))
    wt = jnp.concatenate([nq.T * (HEAD_DIM ** -0.5), nvs.T, nvw.T,
                          glt.reshape(NSA_KV_HEADS * GL_ROWS, D_MODEL)], axis=0).astype(BF16)
    return wn, wt


def _ret_kernel(ret_ref, cos_ref, sa_ref, sb_ref, qd_ref, kd_ref, dmat_ref, out_ref, state_ref,
                *, chunk_decay):
    c = pl.program_id(1)

    @pl.when(c == 0)
    def _():
        state_ref[...] = jnp.zeros_like(state_ref)

    W = RET_WIDTH
    q = ret_ref[:, 0:W]
    k = ret_ref[:, W:2 * W]
    v = ret_ref[:, 2 * W:3 * W]
    gate = ret_ref[:, 3 * W:4 * W]
    cos, sa, sb = cos_ref[...], sa_ref[...], sb_ref[...]
    half = HEAD_DIM // 2

    def rope(t):
        return t * cos + pltpu.roll(t, W - half, 1) * sa + pltpu.roll(t, half, 1) * sb

    qr = rope(q)
    kr = rope(k) * (HEAD_DIM ** -0.5)
    qd = qr * qd_ref[...]
    kd = kr * kd_ref[...]
    ys = []
    for h in range(RET_HEADS):
        sl = slice(h * HEAD_DIM, (h + 1) * HEAD_DIM)
        qb, kb, vb = qr[:, sl].astype(BF16), kr[:, sl].astype(BF16), v[:, sl].astype(BF16)
        s = lax.dot_general(qb, kb, _CONTRACT_LAST, preferred_element_type=F32) * dmat_ref[h]
        inner = jnp.dot(s.astype(BF16), vb, preferred_element_type=F32)
        st = state_ref[h]
        cross = jnp.dot(qd[:, sl].astype(BF16), st.astype(BF16), preferred_element_type=F32)
        y = inner + cross
        mu = jnp.mean(y, axis=-1, keepdims=True)
        yc = y - mu
        var = jnp.mean(yc * yc, axis=-1, keepdims=True)
        ys.append(yc * lax.rsqrt(var + LN_EPS))
        kv = lax.dot_general(kd[:, sl].astype(BF16), vb, _CONTRACT_FIRST,
                             preferred_element_type=F32)
        state_ref[h] = st * chunk_decay[h] + kv
    y = jnp.concatenate(ys, axis=1)
    out_ref[...] = y * (gate * jax.nn.sigmoid(gate))


def _ret_tables(S):
    C, H, d = RET_CHUNK, RET_HEADS, HEAD_DIM
    half = d // 2
    inv = 1.0 / (ROPE_BASE ** (np.arange(half, dtype=np.float64) / half))
    ang = np.arange(S, dtype=np.float64)[:, None] * inv[None, :]
    cos_h = np.concatenate([np.cos(ang), np.cos(ang)], axis=1)
    sin_lo = np.concatenate([-np.sin(ang), np.zeros_like(ang)], axis=1)
    sin_hi = np.concatenate([np.zeros_like(ang), np.sin(ang)], axis=1)
    tile = lambda a: np.tile(a, (1, H)).astype(np.float32)
    log_g = np.log(1.0 - 2.0 ** (-5.0 - np.arange(H, dtype=np.float64)))
    idx = np.arange(C, dtype=np.float64)
    diff = idx[:, None] - idx[None, :]
    dmat = np.where(diff >= 0, np.exp(np.maximum(diff, 0.0)[None] * log_g[:, None, None]), 0.0)
    kd = np.exp((C - 1.0 - idx)[:, None] * log_g[None, :])
    qd = np.exp((idx + 1.0)[:, None] * log_g[None, :])
    rep = lambda a: np.repeat(a, d, axis=1).astype(np.float32)
    chunk_decay = tuple(float(v) for v in np.exp(C * log_g))
    return (tile(cos_h), tile(sin_lo), tile(sin_hi), rep(qd), rep(kd),
            dmat.astype(np.float32), chunk_decay)


def _retention(ret):
    B, S, _ = ret.shape
    C, W = RET_CHUNK, RET_WIDTH
    cos, sa, sb, qd, kd, dmat, chunk_decay = _ret_tables(S)
    pos = pl.BlockSpec((C, W), lambda b, c: (c, 0))
    fixed = pl.BlockSpec((C, W), lambda b, c: (0, 0))
    return pl.pallas_call(
        functools.partial(_ret_kernel, chunk_decay=chunk_decay),
        grid=(B, S // C),
        in_specs=[pl.BlockSpec((None, C, 4 * W), lambda b, c: (b, c, 0)), pos, pos, pos,
                  fixed, fixed, pl.BlockSpec((RET_HEADS, C, C), lambda b, c: (0, 0, 0))],
        out_specs=pl.BlockSpec((None, C, W), lambda b, c: (b, c, 0)),
        out_shape=jax.ShapeDtypeStruct((B, S, W), F32),
        scratch_shapes=[pltpu.VMEM((RET_HEADS, HEAD_DIM, HEAD_DIM), F32)],
        compiler_params=_cparams(("parallel", "arbitrary")),
        name="retention",
    )(ret, jnp.asarray(cos), jnp.asarray(sa), jnp.asarray(sb), jnp.asarray(qd), jnp.asarray(kd),
      jnp.asarray(dmat))


_CMP_ROWS = 256


def _cmp_kernel(kr_ref, pos_ref, wbig_ref, kc_ref, vct_ref):
    n = _CMP_ROWS
    half = 2 * NSA_KV_WIDTH
    lhs = jnp.concatenate([kr_ref[...], pos_ref[...]], axis=0).astype(BF16)
    out = jnp.dot(lhs, wbig_ref[...], preferred_element_type=F32)
    lo = out[0:n, 0:half]
    hi = out[0:n, half:2 * half]
    const = out[n:n + 1, 0:half] + out[n + 1:n + 2, half:2 * half]
    kv = lo + pltpu.roll(hi, n - 1, 0) + const
    kc_ref[...] = kv[:, 0:NSA_KV_WIDTH].astype(BF16)
    vct_ref[...] = kv[:, NSA_KV_WIDTH:2 * NSA_KV_WIDTH].T.astype(BF16)


def _prep_cmp_weights(cmp_pos, cmp_w):
    L2, d, G = CMP_STRIDE, HEAD_DIM, NSA_KV_HEADS
    w5 = cmp_w.reshape(2, 2, L2, d, d)
    eye_kv = jnp.eye(2, dtype=F32)
    eye_g = jnp.eye(G, dtype=F32)
    wbig = jnp.einsum('khlde,kK,gG->lkgdhKGe', w5, eye_kv, eye_g)
    wbig = wbig.reshape(L2 * 2 * G * d, 2 * 2 * G * d).astype(BF16)
    p4 = cmp_pos.reshape(2, 2, L2, d)
    prow = jnp.broadcast_to(p4.transpose(1, 2, 0, 3)[:, :, :, None, :], (2, L2, 2, G, d))
    prow = prow.reshape(2, L2 * 2 * G * d)
    prow = jnp.pad(prow, ((0, 6), (0, 0)))
    return prow, wbig


def _compress(kvc, prow, wbig):
    B, S, _ = kvc.shape
    n, G = _CMP_ROWS, NSA_KV_HEADS
    kr = kvc.reshape(B, n, CMP_STRIDE * 2 * NSA_KV_WIDTH)
    kw = kr.shape[-1]
    return pl.pallas_call(
        _cmp_kernel,
        grid=(B,),
        in_specs=[pl.BlockSpec((None, n, kw), lambda b: (b, 0, 0)),
                  pl.BlockSpec((8, kw), lambda b: (0, 0)),
                  pl.BlockSpec(wbig.shape, lambda b: (0, 0))],
        out_specs=(pl.BlockSpec((None, n, NSA_KV_WIDTH), lambda b: (b, 0, 0)),
                   pl.BlockSpec((None, NSA_KV_WIDTH, n), lambda b: (b, 0, 0))),
        out_shape=(jax.ShapeDtypeStruct((B, n, NSA_KV_WIDTH), BF16),
                   jax.ShapeDtypeStruct((B, NSA_KV_WIDTH, n), BF16)),
        compiler_params=_cparams(("parallel",)),
        name="nsa_compress",
    )(kr, prow, wbig)


def _nsa_kernel(qt_ref, kc_ref, vct_ref, ks_ref, vst_ref, kw_ref, vwt_ref, glt_ref, ng_ref,
                selmap_ref, band_ref, cmpmask_ref, out_ref, val_ref, cnt_ref, s_ref, p_ref, acc_ref,
                st_ref):
    TQ, KB, R, d = NSA_TQ, NSA_KB, NSA_GROUP, HEAD_DIM
    LW = R * TQ
    n_sel = selmap_ref.shape[0]
    g = pl.program_id(1)
    qi = pl.program_id(2)
    t0 = qi * TQ
    qs = jnp.concatenate([qt_ref[r * d:(r + 1) * d, :] for r in range(R)], axis=1)
    zq = jnp.zeros_like(qs)
    qs_nat = jnp.concatenate([jnp.where(g == gg, qs, zq) for gg in range(NSA_KV_HEADS)], axis=0)

    causal_bias = jnp.concatenate([band_ref[0]] * R, axis=1)
    start_bias = jnp.concatenate([band_ref[1]] * R, axis=1)

    def key_off(kb):
        return pl.multiple_of(kb * KB, KB)

    n_wb = WINDOW // KB + 1
    w_scores, w_vt = [], []
    for w in range(n_wb):
        kb = qi - (n_wb - 1) + w
        off = key_off(jnp.maximum(kb, 0))
        s = jnp.dot(kw_ref[pl.ds(off, KB), :], qs_nat, preferred_element_type=F32)
        if w == 0:
            s = s + start_bias
        elif w == n_wb - 1:
            s = s + causal_bias
        if w < n_wb - 1:
            s = s + jnp.where(kb >= 0, 0.0, NEG)
        w_scores.append(s)
        w_vt.append(vwt_ref[:, pl.ds(off, KB)])
    m_w = functools.reduce(jnp.maximum, [jnp.max(s, axis=0, keepdims=True) for s in w_scores])
    w_p = [jnp.exp2(s - m_w) for s in w_scores]
    l_w = functools.reduce(jnp.add, [jnp.sum(p, axis=0, keepdims=True) for p in w_p])
    acc_w = functools.reduce(jnp.add, [jnp.dot(vt, p.astype(BF16), preferred_element_type=F32)
                                       for vt, p in zip(w_vt, w_p)])

    n_cmp = kc_ref.shape[0]
    cmp_off = pl.multiple_of(n_cmp - qi * (TQ // CMP_STRIDE), TQ // CMP_STRIDE)
    cmp_bias = jnp.concatenate([cmpmask_ref[pl.ds(cmp_off, n_cmp), :]] * R, axis=1)
    s = jnp.dot(kc_ref[...], qs_nat, preferred_element_type=F32) + cmp_bias
    p = jnp.exp2(s - jnp.max(s, axis=0, keepdims=True))
    l = jnp.sum(p, axis=0, keepdims=True)
    tq_w = t0 + lax.rem(lax.broadcasted_iota(jnp.int32, (1, LW), 1), TQ)
    p = p * jnp.where(tq_w >= CMP_LEN - 1, 1.0 / l, 0.0)
    o_c = jnp.dot(vct_ref[...], p.astype(BF16), preferred_element_type=F32)

    psum = p[:, 0:TQ]
    for r in range(1, R):
        psum = psum + p[:, r * TQ:(r + 1) * TQ]
    p_hi = psum.astype(BF16)
    p_lo = (psum - p_hi.astype(F32)).astype(BF16)
    imp = (jnp.dot(selmap_ref[...], p_hi, preferred_element_type=F32)
           + jnp.dot(selmap_ref[...], p_lo, preferred_element_type=F32))

    tq = t0 + lax.broadcasted_iota(jnp.int32, (1, TQ), 1)
    jrow = lax.broadcasted_iota(jnp.int32, (n_sel, 1), 0)
    cur = lax.shift_right_logical(tq, SEL_SHIFT)
    forced = (jrow == 0) | (jrow == cur) | (jrow == cur - 1)
    causal = jrow * SEL_LEN <= tq
    val_ref[...] = jnp.where(forced, FORCE_SCORE, jnp.where(causal, imp, -FORCE_SCORE))
    cnt_ref[...] = jnp.zeros_like(cnt_ref)
    n_live = (qi + 1) * (TQ // SEL_LEN)
    for c in range(n_sel // SUBLANES):

        @pl.when(c * SUBLANES < n_live)
        def _():
            for k in range(n_sel // SUBLANES):
                rows = slice(k * SUBLANES, (k + 1) * SUBLANES)
                v = val_ref[rows, :]
                cnt = cnt_ref[rows, :]
                for jp in range(c * SUBLANES, (c + 1) * SUBLANES):
                    cand = val_ref[jp:jp + 1, :]
                    gt = jnp.where(cand > v, 1.0, 0.0)
                    ge = jnp.where(cand >= v, 1.0, 0.0)
                    if k < c:
                        cnt = cnt + gt
                    elif k > c:
                        cnt = cnt + ge
                    else:
                        own = lax.broadcasted_iota(jnp.int32, (SUBLANES, 1), 0) + k * SUBLANES
                        cnt = cnt + jnp.where(own > jp, ge, gt)
                cnt_ref[rows, :] = cnt

    bias = jnp.where(cnt_ref[...] < float(N_SELECT), 0.0, NEG)
    qs_sel = jnp.concatenate([qs, jnp.concatenate([bias] * R, axis=1).astype(BF16)], axis=0)

    heads = [slice(r * TQ, (r + 1) * TQ) for r in range(R)]

    def sel_scores(kb, hs):
        return jnp.dot(ks_ref[pl.ds(key_off(kb), KB), :], qs_sel[:, hs],
                       preferred_element_type=F32)

    def sel_values(kb, p):
        return jnp.dot(vst_ref[:, pl.ds(key_off(kb), KB)], p, preferred_element_type=F32)

    ROW_M, ROW_L, ROW_A = 0, 1, 2

    def sel_stage(i, cur, *, diag):
        nxt = 1 - cur
        for hs in heads:
            if not diag:
                s_ref[nxt, :, hs] = sel_scores(i + 1, hs)
            pv_prev = sel_values(jnp.maximum(i - 1, 0), p_ref[nxt, :, hs])
            s = s_ref[cur, :, hs]
            if diag:
                s = s + band_ref[0]
            m = st_ref[ROW_M:ROW_M + 1, hs]
            m_new = jnp.maximum(m, jnp.max(s, axis=0, keepdims=True))
            alpha = jnp.exp2(m - m_new)
            p = jnp.exp2(s - m_new)
            p_ref[cur, :, hs] = p.astype(BF16)
            acc_ref[:, hs] = st_ref[ROW_A:ROW_A + 1, hs] * acc_ref[:, hs] + pv_prev
            st_ref[ROW_M:ROW_M + 1, hs] = m_new
            st_ref[ROW_L:ROW_L + 1, hs] = (alpha * st_ref[ROW_L:ROW_L + 1, hs]
                                           + jnp.sum(p, axis=0, keepdims=True))
            st_ref[ROW_A:ROW_A + 1, hs] = alpha

    def sel_finish(i, cur):
        for hs in heads:
            acc_ref[:, hs] = (st_ref[ROW_A:ROW_A + 1, hs] * acc_ref[:, hs]
                              + sel_values(i, p_ref[cur, :, hs]))

    for hs in heads:
        s_ref[0, :, hs] = sel_scores(0, hs)
    p_ref[1] = jnp.zeros((KB, LW), BF16)
    acc_ref[...] = jnp.zeros((d, LW), F32)
    st_ref[ROW_M:ROW_M + 1, :] = jnp.full((1, LW), NEG, F32)
    st_ref[ROW_L:ROW_L + 1, :] = jnp.zeros((1, LW), F32)
    st_ref[ROW_A:ROW_A + 1, :] = jnp.ones((1, LW), F32)

    def sel_pair(j, c):
        sel_stage(2 * j, 0, diag=False)
        sel_stage(2 * j + 1, 1, diag=False)
        return c

    lax.fori_loop(0, lax.shift_right_logical(qi, 1), sel_pair, 0)
    odd = lax.rem(qi, 2) == 1

    @pl.when(odd)
    def _():
        sel_stage(qi - 1, 0, diag=False)
        sel_stage(qi, 1, diag=True)
        sel_finish(qi, 1)

    @pl.when(jnp.logical_not(odd))
    def _():
        sel_stage(qi, 0, diag=True)
        sel_finish(qi, 0)

    acc_s = acc_ref[...]
    l_s = st_ref[ROW_L:ROW_L + 1, :]

    gl = jax.nn.sigmoid(glt_ref[...])

    def gate(br):
        return jnp.concatenate([gl[r * N_BRANCH + br:r * N_BRANCH + br + 1, :] for r in range(R)],
                               axis=1)

    res = gate(0) * o_c + gate(1) * (acc_s / l_s) + gate(2) * (acc_w / l_w)
    res = jnp.concatenate([res[:, r * TQ:(r + 1) * TQ] for r in range(R)], axis=0)
    ng = ng_ref[...]
    out_ref[...] = res.T * (ng * jax.nn.sigmoid(ng))


def _nsa_tables(S):
    n_cmp = (S - CMP_LEN) // CMP_STRIDE + 1
    n_sel = S // SEL_LEN
    ci = np.arange(n_cmp)[:, None]
    sj = np.arange(n_sel)[None, :]
    overlap = (np.minimum(ci * CMP_STRIDE + CMP_LEN, sj * SEL_LEN + SEL_LEN)
               - np.maximum(ci * CMP_STRIDE, sj * SEL_LEN))
    sel_map = np.clip(overlap, 0, None) / CMP_STRIDE
    selmap_t = np.zeros((n_sel, _CMP_ROWS), np.float32)
    selmap_t[:, :n_cmp] = sel_map.T
    jl = np.arange(NSA_KB)[:, None]
    tl = np.arange(NSA_TQ)[None, :]
    band = np.stack([np.where(jl <= tl, 0.0, NEG), np.where(jl > tl, 0.0, NEG)]).astype(np.float32)
    rel = (np.arange(2 * _CMP_ROWS)[:, None] - _CMP_ROWS) * CMP_STRIDE + (CMP_LEN - 1)
    cmpmask = np.where(rel <= tl, 0.0, NEG).astype(np.float32)
    return selmap_t, band, cmpmask


def _nsa_attention(qt, kc, vct, ks, vst, kw, vwt, glt, ng):
    B, _, S = qt.shape
    G, R, d, TQ = NSA_KV_HEADS, NSA_GROUP, HEAD_DIM, NSA_TQ
    selmap_t, band, cmpmask = _nsa_tables(S)
    n_sel = S // SEL_LEN
    assert NSA_KB == TQ and WINDOW % NSA_KB == 0
    assert n_sel == d
    per_b_rows = lambda n: pl.BlockSpec((None, n, G * d), lambda b, g, q: (b, 0, 0))
    per_bg_cols = lambda n: pl.BlockSpec((None, d, n), lambda b, g, q: (b, g, 0))
    return pl.pallas_call(
        _nsa_kernel,
        grid=(B, G, S // TQ),
        in_specs=[pl.BlockSpec((None, R * d, TQ), lambda b, g, q: (b, g, q)),
                  per_b_rows(_CMP_ROWS), per_bg_cols(_CMP_ROWS),
                  pl.BlockSpec((None, None, S, 2 * d), lambda b, g, q: (b, g, 0, 0)),
                  per_bg_cols(S), per_b_rows(S), per_bg_cols(S),
                  pl.BlockSpec((None, GL_ROWS, TQ), lambda b, g, q: (b, g, q)),
                  pl.BlockSpec((None, TQ, R * d), lambda b, g, q: (b, q, g)),
                  pl.BlockSpec(selmap_t.shape, lambda b, g, q: (0, 0)),
                  pl.BlockSpec(band.shape, lambda b, g, q: (0, 0, 0)),
                  pl.BlockSpec(cmpmask.shape, lambda b, g, q: (0, 0))],
        out_specs=pl.BlockSpec((None, TQ, R * d), lambda b, g, q: (b, q, g)),
        out_shape=jax.ShapeDtypeStruct((B, S, NSA_WIDTH), F32),
        scratch_shapes=[pltpu.VMEM((n_sel, TQ), F32), pltpu.VMEM((n_sel, TQ), F32),
                        pltpu.VMEM((2, NSA_KB, R * TQ), F32),
                        pltpu.VMEM((2, NSA_KB, R * TQ), BF16),
                        pltpu.VMEM((d, R * TQ), F32),
                        pltpu.VMEM((8, R * TQ), F32)],
        compiler_params=_cparams(("parallel", "parallel", "arbitrary")),
        name="nsa_attention",
    )(qt, kc, vct, ks, vst, kw, vwt, glt, ng, jnp.asarray(selmap_t, dtype=BF16), jnp.asarray(band),
      jnp.asarray(cmpmask))


def _lru_kernel(lru_ref, cw_ref, cb_ref, wa_ref, ba_ref, wx_ref, bx_ref, lam_ref, out_ref,
                xext_ref, h_ref):
    ts, W = LRU_TS, LRU_WIDTH
    i = pl.program_id(1)

    @pl.when(i == 0)
    def _():
        xext_ref[0:8, :] = jnp.zeros((8, W), F32)
        h_ref[...] = jnp.zeros_like(h_ref)

    x = lru_ref[:, 0:W]
    gate = lru_ref[:, W:2 * W]
    xext_ref[8:8 + ts, :] = x
    xc = cb_ref[...] + cw_ref[0:1, :] * xext_ref[pl.ds(8 - (CONV_WIDTH - 1), ts), :]
    for w in range(1, CONV_WIDTH):
        xc = xc + cw_ref[w:w + 1, :] * xext_ref[pl.ds(8 - (CONV_WIDTH - 1) + w, ts), :]
    xext_ref[0:8, :] = x[ts - 8:ts, :]
    xcb = xc.astype(BF16)
    r = jax.nn.sigmoid(jnp.dot(xcb, wa_ref[...], preferred_element_type=F32) + ba_ref[...])
    gi = jax.nn.sigmoid(jnp.dot(xcb, wx_ref[...], preferred_element_type=F32) + bx_ref[...])
    nl = -lam_ref[...]
    softplus = jnp.maximum(nl, 0.0) + jnp.log1p(jnp.exp(-jnp.abs(nl)))
    log_a = (-LRU_C) * r * softplus
    a = jnp.exp(log_a)
    u = jnp.sqrt(-jnp.tanh(log_a) * (a * a + 1.0)) * (gi * xc)
    row = lax.broadcasted_iota(jnp.int32, (ts, 1), 0)
    step = 1
    while step < ts:
        keep = row >= step
        a_sh = jnp.where(keep, pltpu.roll(a, step, 0), 1.0)
        u_sh = jnp.where(keep, pltpu.roll(u, step, 0), 0.0)
        u = a * u_sh + u
        a = a * a_sh
        step *= 2
    h = u + a * h_ref[0:1, :]
    h_ref[0:1, :] = h[ts - 1:ts, :]
    out_ref[...] = h * (gate * jax.nn.sigmoid(gate))


def _block_diag(w):
    n, d, _ = w.shape
    return jnp.einsum('nde,nm->ndme', w, jnp.eye(n, dtype=w.dtype)).reshape(n * d, n * d)


def _rg_lru(lru, conv_w, conv_b, w_a, b_a, w_x, b_x, lam):
    B, S, _ = lru.shape
    ts, W = LRU_TS, LRU_WIDTH
    vec = lambda a: a.reshape(1, W)
    fixed = lambda r: pl.BlockSpec((r, W), lambda b, i: (0, 0))
    return pl.pallas_call(
        _lru_kernel,
        grid=(B, S // ts),
        in_specs=[pl.BlockSpec((None, ts, 2 * W), lambda b, i: (b, i, 0)),
                  fixed(CONV_WIDTH), fixed(1), fixed(W), fixed(1), fixed(W), fixed(1), fixed(1)],
        out_specs=pl.BlockSpec((None, ts, W), lambda b, i: (b, i, 0)),
        out_shape=jax.ShapeDtypeStruct((B, S, W), F32),
        scratch_shapes=[pltpu.VMEM((ts + 8, W), F32), pltpu.VMEM((8, W), F32)],
        compiler_params=_cparams(("parallel", "arbitrary")),
        name="rg_lru",
    )(lru, conv_w, vec(conv_b), _block_diag(w_a).astype(BF16), vec(b_a),
      _block_diag(w_x).astype(BF16), vec(b_x), vec(lam))


def _out_kernel(x_ref, yr_ref, yn_ref, yl_ref, wo_ref, g_ref, b_ref, out_ref):
    o1, o2 = RET_WIDTH, RET_WIDTH + NSA_WIDTH
    y = jnp.dot(yr_ref[...].astype(BF16), wo_ref[0:o1, :], preferred_element_type=F32)
    y = y + jnp.dot(yn_ref[...].astype(BF16), wo_ref[o1:o2, :], preferred_element_type=F32)
    y = y + jnp.dot(yl_ref[...].astype(BF16), wo_ref[o2:D_MODEL, :], preferred_element_type=F32)
    z = DEEPNORM_ALPHA * x_ref[...] + y
    mu = jnp.mean(z, axis=-1, keepdims=True)
    zc = z - mu
    var = jnp.mean(zc * zc, axis=-1, keepdims=True)
    out_ref[...] = zc * lax.rsqrt(var + LN_EPS) * g_ref[...] + b_ref[...]


def _out_proj(x, y_ret, y_nsa, y_lru, w_out, ln_g, ln_b):
    B, S, _ = x.shape
    tm = OUT_TM
    row = lambda w: pl.BlockSpec((None, tm, w), lambda b, i: (b, i, 0))
    fixed = lambda r: pl.BlockSpec((r, D_MODEL), lambda b, i: (0, 0))
    return pl.pallas_call(
        _out_kernel,
        grid=(B, S // tm),
        in_specs=[row(D_MODEL), row(RET_WIDTH), row(NSA_WIDTH), row(LRU_WIDTH),
                  fixed(D_MODEL), fixed(1), fixed(1)],
        out_specs=row(D_MODEL),
        out_shape=jax.ShapeDtypeStruct((B, S, D_MODEL), F32),
        compiler_params=_cparams(("parallel", "parallel")),
        name="out_proj_ln",
    )(x, y_ret, y_nsa, y_lru, w_out.astype(BF16), ln_g.reshape(1, D_MODEL),
      ln_b.reshape(1, D_MODEL))


def _layer(x, w_in, w_out, ln_g, ln_b, cmp_pos, cmp_w, conv_w, conv_b, w_a, b_a, w_x, b_x, lam):
    wn, wt = _prep_in_weights(w_in)
    ret, kvc, ks, kw, ng, lru, qt, vst, vwt, glt = _project(x, wn, wt)
    y_ret = _retention(ret)
    prow, wbig = _prep_cmp_weights(cmp_pos, cmp_w)
    kc, vct = _compress(kvc, prow, wbig)
    y_nsa = _nsa_attention(qt, kc, vct, ks, vst, kw, vwt, glt, ng)
    y_lru = _rg_lru(lru, conv_w, conv_b, w_a, b_a, w_x, b_x, lam)
    return _out_proj(x, y_ret, y_nsa, y_lru, w_out, ln_g, ln_b)


def kernel(x, w_in, w_out, ln_g, ln_b, nsa_cmp_pos, nsa_cmp_w, lru_conv_w, lru_conv_b,
           lru_w_a, lru_b_a, lru_w_x, lru_b_x, lru_lambda):
    assert x.shape[1] % max(PROJ_TM, RET_CHUNK, LRU_TS, OUT_TM, NSA_TQ) == 0
    assert x.shape[1] // CMP_STRIDE == _CMP_ROWS
    for l in range(DEPTH):
        x = _layer(x, w_in[l], w_out[l], ln_g[l], ln_b[l], nsa_cmp_pos[l], nsa_cmp_w[l],
                   lru_conv_w[l], lru_conv_b[l], lru_w_a[l], lru_b_a[l], lru_w_x[l], lru_b_x[l],
                   lru_lambda[l])
    return x
```

```python
import functools
import math

import jax
import jax.numpy as jnp
import numpy as np
from jax import lax
from jax.experimental import pallas as pl
from jax.experimental.pallas import tpu as pltpu

F32 = jnp.float32
BF16 = jnp.bfloat16

D_MODEL = 1024
DEPTH = 2
HEAD_DIM = 64
RET_HEADS = 4
RET_WIDTH = RET_HEADS * HEAD_DIM
ROPE_BASE = 10000.0
NSA_HEADS = 8
NSA_KV_HEADS = 2
NSA_GROUP = NSA_HEADS // NSA_KV_HEADS
NSA_WIDTH = NSA_HEADS * HEAD_DIM
NSA_KV_WIDTH = NSA_KV_HEADS * HEAD_DIM
CMP_LEN = 32
CMP_STRIDE = 16
SEL_LEN = 64
N_SELECT = 16
WINDOW = 512
N_BRANCH = 3
FORCE_SCORE = 1.0e4
LRU_WIDTH = 256
LRU_BLOCKS = 4
LRU_BLOCK_DIM = LRU_WIDTH // LRU_BLOCKS
CONV_WIDTH = 4
LRU_C = 8.0
DEEPNORM_ALPHA = (2.0 * DEPTH) ** 0.25
LN_EPS = 1e-5
NEG = -1e30
LOG2E = math.log2(math.e)
SEL_SHIFT = int(math.log2(SEL_LEN))

IN_SIZES = (RET_WIDTH,) * 4 + (NSA_WIDTH,) + (NSA_KV_WIDTH,) * 6 + (
    NSA_WIDTH, NSA_HEADS * N_BRANCH, LRU_WIDTH, LRU_WIDTH)
IN_OFFS = tuple(int(v) for v in np.cumsum((0,) + IN_SIZES))

V7X_VMEM_LIMIT = 48 * 1024 * 1024
SUBLANES = 8
BF16_SUBLANES = 16
PROJ_TM = 256
RET_CHUNK = 256
LRU_TS = 256
OUT_TM = 512
NSA_TQ = 256
NSA_KB = 256
GL_ROWS = 16
V_ROWS = HEAD_DIM + BF16_SUBLANES

_CONTRACT_LAST = (((1,), (1,)), ((), ()))
_CONTRACT_FIRST = (((0,), (0,)), ((), ()))


def _cparams(sem):
    return pltpu.CompilerParams(dimension_semantics=sem, vmem_limit_bytes=V7X_VMEM_LIMIT)


_WN_COLS = (4 * RET_WIDTH, 2 * NSA_KV_WIDTH, NSA_KV_WIDTH, NSA_KV_WIDTH, NSA_WIDTH, 2 * LRU_WIDTH)
_WN_OFFS = tuple(int(v) for v in np.cumsum((0,) + _WN_COLS))
_WT_ROWS = (NSA_WIDTH, NSA_KV_WIDTH, NSA_KV_WIDTH, NSA_KV_HEADS * GL_ROWS)
_WT_OFFS = tuple(int(v) for v in np.cumsum((0,) + _WT_ROWS))


def _proj_kernel(x_ref, wn_ref, wt_ref, ret_ref, kcn_ref, vcn_ref, ks_ref, kw_ref, ng_ref, lru_ref,
                 qt_ref, vst_ref, vwt_ref, glt_ref):
    xb = x_ref[...].astype(BF16)

    def nat(i):
        return jnp.dot(xb, wn_ref[:, _WN_OFFS[i]:_WN_OFFS[i + 1]], preferred_element_type=F32)

    def tr(i):
        return lax.dot_general(wt_ref[_WT_OFFS[i]:_WT_OFFS[i + 1], :], xb, _CONTRACT_LAST,
                               preferred_element_type=F32)

    ret_ref[...] = nat(0)
    kvc = nat(1)
    kcn_ref[...] = kvc[:, 0:NSA_KV_WIDTH]
    vcn_ref[...] = kvc[:, NSA_KV_WIDTH:2 * NSA_KV_WIDTH]
    ks = nat(2)
    tm = ks.shape[0]
    key = pl.program_id(1) * tm + lax.broadcasted_iota(jnp.int32, (tm, 1), 0)
    blk = lax.broadcasted_iota(jnp.int32, (1, HEAD_DIM), 1)
    onehot = jnp.where(lax.shift_right_logical(key, SEL_SHIFT) == blk, 1.0, 0.0)
    for g in range(NSA_KV_HEADS):
        ks_ref[g] = jnp.concatenate([ks[:, g * HEAD_DIM:(g + 1) * HEAD_DIM], onehot],
                                    axis=1).astype(BF16)
    kw_ref[...] = nat(3).astype(BF16)
    ng_ref[...] = nat(4)
    lru_ref[...] = nat(5)
    qt_ref[...] = (tr(0) * LOG2E).astype(BF16)
    ones_rows = jnp.where(lax.broadcasted_iota(jnp.int32, (BF16_SUBLANES, tm), 0) == 0,
                          1.0, 0.0).astype(BF16)
    for ref, i in ((vst_ref, 1), (vwt_ref, 2)):
        vt = tr(i)
        for g in range(NSA_KV_HEADS):
            ref[g * V_ROWS:g * V_ROWS + HEAD_DIM, :] = (
                vt[g * HEAD_DIM:(g + 1) * HEAD_DIM, :].astype(BF16))
            ref[g * V_ROWS + HEAD_DIM:(g + 1) * V_ROWS, :] = ones_rows
    glt_ref[...] = tr(3)


def _project(x, wn, wt):
    B, S, _ = x.shape
    tm = PROJ_TM
    nn, nt = wn.shape[1], wt.shape[0]
    G = NSA_KV_HEADS
    row = lambda w: pl.BlockSpec((None, tm, w), lambda b, i: (b, i, 0))
    col = lambda r: pl.BlockSpec((None, r, tm), lambda b, i: (b, 0, i))
    kspec = pl.BlockSpec((None, G, tm, 2 * HEAD_DIM), lambda b, i: (b, 0, i, 0))
    out_shape = (
        jax.ShapeDtypeStruct((B, S, 4 * RET_WIDTH), F32),
        jax.ShapeDtypeStruct((B, S, NSA_KV_WIDTH), F32),
        jax.ShapeDtypeStruct((B, S, NSA_KV_WIDTH), F32),
        jax.ShapeDtypeStruct((B, G, S, 2 * HEAD_DIM), BF16),
        jax.ShapeDtypeStruct((B, S, NSA_KV_WIDTH), BF16),
        jax.ShapeDtypeStruct((B, S, NSA_WIDTH), F32),
        jax.ShapeDtypeStruct((B, S, 2 * LRU_WIDTH), F32),
        jax.ShapeDtypeStruct((B, NSA_WIDTH, S), BF16),
        jax.ShapeDtypeStruct((B, G * V_ROWS, S), BF16),
        jax.ShapeDtypeStruct((B, G * V_ROWS, S), BF16),
        jax.ShapeDtypeStruct((B, G * GL_ROWS, S), F32),
    )
    return pl.pallas_call(
        _proj_kernel,
        grid=(B, S // tm),
        in_specs=[row(D_MODEL),
                  pl.BlockSpec((D_MODEL, nn), lambda b, i: (0, 0)),
                  pl.BlockSpec((nt, D_MODEL), lambda b, i: (0, 0))],
        out_specs=(row(4 * RET_WIDTH), row(NSA_KV_WIDTH), row(NSA_KV_WIDTH), kspec,
                   row(NSA_KV_WIDTH), row(NSA_WIDTH),
                   row(2 * LRU_WIDTH), col(NSA_WIDTH), col(G * V_ROWS), col(G * V_ROWS),
                   col(G * GL_ROWS)),
        out_shape=out_shape,
        compiler_params=_cparams(("parallel", "parallel")),
        name="in_proj",
    )(x, wn, wt)


def _prep_in_weights(w_in):
    seg = [w_in[:, IN_OFFS[i]:IN_OFFS[i + 1]] for i in range(len(IN_SIZES))]
    (rq, rk, rv, rg, nq, nkc, nvc, nks, nvs, nkw, nvw, ng, ngl, lx, lg) = seg
    wn = jnp.concatenate([rq, rk, rv, rg, nkc, nvc, nks, nkw, ng, lx, lg], axis=1).astype(BF16)
    glt = ngl.T.reshape(NSA_KV_HEADS, NSA_GROUP * N_BRANCH, D_MODEL)
    glt = jnp.pad(glt, ((0, 0), (0, GL_ROWS - NSA_GROUP * N_BRANCH), (0, 0)))
    wt = jnp.concatenate([nq.T * (HEAD_DIM ** -0.5), nvs.T, nvw.T,
                          glt.reshape(NSA_KV_HEADS * GL_ROWS, D_MODEL)], axis=0).astype(BF16)
    return wn, wt


def _ret_kernel(ret_ref, cos_ref, sa_ref, sb_ref, qd_ref, kd_ref, dmat_ref, out_ref, state_ref,
                *, chunk_decay):
    c = pl.program_id(1)

    @pl.when(c == 0)
    def _():
        state_ref[...] = jnp.zeros_like(state_ref)

    W = RET_WIDTH
    q = ret_ref[:, 0:W]
    k = ret_ref[:, W:2 * W]
    v = ret_ref[:, 2 * W:3 * W]
    gate = ret_ref[:, 3 * W:4 * W]
    cos, sa, sb = cos_ref[...], sa_ref[...], sb_ref[...]
    half = HEAD_DIM // 2

    def rope(t):
        return t * cos + pltpu.roll(t, W - half, 1) * sa + pltpu.roll(t, half, 1) * sb

    qr = rope(q)
    kr = rope(k) * (HEAD_DIM ** -0.5)
    qd = qr * qd_ref[...]
    kd = kr * kd_ref[...]
    ys = []
    for h in range(RET_HEADS):
        sl = slice(h * HEAD_DIM, (h + 1) * HEAD_DIM)
        qb, kb, vb = qr[:, sl].astype(BF16), kr[:, sl].astype(BF16), v[:, sl].astype(BF16)
        s = lax.dot_general(qb, kb, _CONTRACT_LAST, preferred_element_type=F32) * dmat_ref[h]
        inner = jnp.dot(s.astype(BF16), vb, preferred_element_type=F32)
        st = state_ref[h]
        cross = jnp.dot(qd[:, sl].astype(BF16), st.astype(BF16), preferred_element_type=F32)
        y = inner + cross
        mu = jnp.mean(y, axis=-1, keepdims=True)
        yc = y - mu
        var = jnp.mean(yc * yc, axis=-1, keepdims=True)
        ys.append(yc * lax.rsqrt(var + LN_EPS))
        kv = lax.dot_general(kd[:, sl].astype(BF16), vb, _CONTRACT_FIRST,
                             preferred_element_type=F32)
        state_ref[h] = st * chunk_decay[h] + kv
    y = jnp.concatenate(ys, axis=1)
    out_ref[...] = y * (gate * jax.nn.sigmoid(gate))


def _ret_tables(S):
    C, H, d = RET_CHUNK, RET_HEADS, HEAD_DIM
    half = d // 2
    inv = 1.0 / (ROPE_BASE ** (np.arange(half, dtype=np.float64) / half))
    ang = np.arange(S, dtype=np.float64)[:, None] * inv[None, :]
    cos_h = np.concatenate([np.cos(ang), np.cos(ang)], axis=1)
    sin_lo = np.concatenate([-np.sin(ang), np.zeros_like(ang)], axis=1)
    sin_hi = np.concatenate([np.zeros_like(ang), np.sin(ang)], axis=1)
    tile = lambda a: np.tile(a, (1, H)).astype(np.float32)
    log_g = np.log(1.0 - 2.0 ** (-5.0 - np.arange(H, dtype=np.float64)))
    idx = np.arange(C, dtype=np.float64)
    diff = idx[:, None] - idx[None, :]
    dmat = np.where(diff >= 0, np.exp(np.maximum(diff, 0.0)[None] * log_g[:, None, None]), 0.0)
    kd = np.exp((C - 1.0 - idx)[:, None] * log_g[None, :])
    qd = np.exp((idx + 1.0)[:, None] * log_g[None, :])
    rep = lambda a: np.repeat(a, d, axis=1).astype(np.float32)
    chunk_decay = tuple(float(v) for v in np.exp(C * log_g))
    return (tile(cos_h), tile(sin_lo), tile(sin_hi), rep(qd), rep(kd),
            dmat.astype(np.float32), chunk_decay)


def _retention(ret):
    B, S, _ = ret.shape
    C, W = RET_CHUNK, RET_WIDTH
    cos, sa, sb, qd, kd, dmat, chunk_decay = _ret_tables(S)
    pos = pl.BlockSpec((C, W), lambda b, c: (c, 0))
    fixed = pl.BlockSpec((C, W), lambda b, c: (0, 0))
    return pl.pallas_call(
        functools.partial(_ret_kernel, chunk_decay=chunk_decay),
        grid=(B, S // C),
        in_specs=[pl.BlockSpec((None, C, 4 * W), lambda b, c: (b, c, 0)), pos, pos, pos,
                  fixed, fixed, pl.BlockSpec((RET_HEADS, C, C), lambda b, c: (0, 0, 0))],
        out_specs=pl.BlockSpec((None, C, W), lambda b, c: (b, c, 0)),
        out_shape=jax.ShapeDtypeStruct((B, S, W), F32),
        scratch_shapes=[pltpu.VMEM((RET_HEADS, HEAD_DIM, HEAD_DIM), F32)],
        compiler_params=_cparams(("parallel", "arbitrary")),
        name="retention",
    )(ret, jnp.asarray(cos), jnp.asarray(sa), jnp.asarray(sb), jnp.asarray(qd), jnp.asarray(kd),
      jnp.asarray(dmat))


_CMP_ROWS = 256


def _cmp_kernel(kcn_ref, vcn_ref, pos_ref, wl_ref, kc_ref, vct_ref):
    n = _CMP_ROWS
    half = 2 * NSA_KV_WIDTH
    out = jnp.zeros((n, 2 * half), F32)
    cst = jnp.zeros((SUBLANES, 2 * half), F32)
    for l in range(CMP_STRIDE):
        for i, ref in enumerate((kcn_ref, vcn_ref)):
            x = ref[pl.ds(l, n, stride=CMP_STRIDE), :].astype(BF16)
            w = wl_ref[l, i * NSA_KV_WIDTH:(i + 1) * NSA_KV_WIDTH, :]
            out = out + jnp.dot(x, w, preferred_element_type=F32)
        cst = cst + jnp.dot(pos_ref[l].astype(BF16), wl_ref[l], preferred_element_type=F32)
    lo = out[:, 0:half]
    hi = out[:, half:2 * half]
    const = cst[0:1, 0:half] + cst[1:2, half:2 * half]
    kv = lo + pltpu.roll(hi, n - 1, 0) + const
    kc_ref[...] = kv[:, 0:NSA_KV_WIDTH].astype(BF16)
    vct_ref[...] = kv[:, NSA_KV_WIDTH:2 * NSA_KV_WIDTH].T.astype(BF16)


def _prep_cmp_weights(cmp_pos, cmp_w):
    L2, d, G = CMP_STRIDE, HEAD_DIM, NSA_KV_HEADS
    w5 = cmp_w.reshape(2, 2, L2, d, d)
    eye_kv = jnp.eye(2, dtype=F32)
    eye_g = jnp.eye(G, dtype=F32)
    wl = jnp.einsum('khlde,kK,gG->lkgdhKGe', w5, eye_kv, eye_g)
    wl = wl.reshape(L2, 2 * G * d, 2 * 2 * G * d).astype(BF16)
    p4 = cmp_pos.reshape(2, 2, L2, d)
    prow = jnp.broadcast_to(p4.transpose(2, 1, 0, 3)[:, :, :, None, :], (L2, 2, 2, G, d))
    prow = prow.reshape(L2, 2, 2 * G * d)
    prow = jnp.pad(prow, ((0, 0), (0, SUBLANES - 2), (0, 0)))
    return prow, wl


def _compress(kcn, vcn, prow, wl):
    B, S, W = kcn.shape
    n = _CMP_ROWS
    return pl.pallas_call(
        _cmp_kernel,
        grid=(B,),
        in_specs=[pl.BlockSpec((None, S, W), lambda b: (b, 0, 0)),
                  pl.BlockSpec((None, S, W), lambda b: (b, 0, 0)),
                  pl.BlockSpec(prow.shape, lambda b: (0, 0, 0)),
                  pl.BlockSpec(wl.shape, lambda b: (0, 0, 0))],
        out_specs=(pl.BlockSpec((None, n, NSA_KV_WIDTH), lambda b: (b, 0, 0)),
                   pl.BlockSpec((None, NSA_KV_WIDTH, n), lambda b: (b, 0, 0))),
        out_shape=(jax.ShapeDtypeStruct((B, n, NSA_KV_WIDTH), BF16),
                   jax.ShapeDtypeStruct((B, NSA_KV_WIDTH, n), BF16)),
        compiler_params=_cparams(("parallel",)),
        name="nsa_compress",
    )(kcn, vcn, prow, wl)


def _nsa_kernel(qt_ref, kc_ref, vct_ref, ks_ref, vst_ref, kw_ref, vwt_ref, glt_ref, ng_ref,
                selmap_ref, band_ref, cmpmask_ref, out_ref, val_ref, cnt_ref, s_ref, p_ref, acc_ref,
                st_ref):
    TQ, KB, R, d = NSA_TQ, NSA_KB, NSA_GROUP, HEAD_DIM
    n_sel = selmap_ref.shape[0]
    g = pl.program_id(1)
    qi = pl.program_id(2)
    t0 = qi * TQ
    heads = [slice(r * TQ, (r + 1) * TQ) for r in range(R)]
    qs = [qt_ref[r * d:(r + 1) * d, :] for r in range(R)]
    zq = jnp.zeros_like(qs[0])
    qs_nat = [jnp.concatenate([jnp.where(g == gg, q, zq) for gg in range(NSA_KV_HEADS)], axis=0)
              for q in qs]

    def key_off(kb):
        return pl.multiple_of(kb * KB, KB)

    n_wb = WINDOW // KB + 1
    w_kb = [qi - (n_wb - 1) + w for w in range(n_wb)]
    w_off = [key_off(jnp.maximum(kb, 0)) for kb in w_kb]
    w_bias = [band_ref[1]] + [None] * (n_wb - 2) + [band_ref[0]]
    qs_nat_all = jnp.concatenate(qs_nat, axis=1)
    w_s = []
    for w in range(n_wb):
        s = jnp.dot(kw_ref[pl.ds(w_off[w], KB), :], qs_nat_all, preferred_element_type=F32)
        if w_bias[w] is not None:
            s = s + jnp.concatenate([w_bias[w]] * R, axis=1)
        if w < n_wb - 1:
            s = s + jnp.where(w_kb[w] >= 0, 0.0, NEG)
        w_s.append(s)
    m_w = functools.reduce(jnp.maximum, [jnp.max(s, axis=0, keepdims=True) for s in w_s])
    acc_w_all = functools.reduce(jnp.add, [
        jnp.dot(vwt_ref[:, pl.ds(w_off[w], KB)], jnp.exp2(w_s[w] - m_w).astype(BF16),
                preferred_element_type=F32) for w in range(n_wb)])
    acc_w = [acc_w_all[:, hs] for hs in heads]

    n_cmp = kc_ref.shape[0]
    cmp_off = pl.multiple_of(n_cmp - qi * (TQ // CMP_STRIDE), TQ // CMP_STRIDE)
    cmp_bias = jnp.concatenate([cmpmask_ref[pl.ds(cmp_off, n_cmp), :]] * R, axis=1)
    tq = t0 + lax.broadcasted_iota(jnp.int32, (1, TQ), 1)
    seen = jnp.concatenate([tq >= CMP_LEN - 1] * R, axis=1)
    s = jnp.dot(kc_ref[...], qs_nat_all, preferred_element_type=F32) + cmp_bias
    p = jnp.exp2(s - jnp.max(s, axis=0, keepdims=True))
    p = p * jnp.where(seen, 1.0 / jnp.sum(p, axis=0, keepdims=True), 0.0)
    o_c_all = jnp.dot(vct_ref[...], p.astype(BF16), preferred_element_type=F32)
    o_c = [o_c_all[:, hs] for hs in heads]
    psum = functools.reduce(jnp.add, [p[:, hs] for hs in heads])

    p_hi = psum.astype(BF16)
    p_lo = (psum - p_hi.astype(F32)).astype(BF16)
    imp = (jnp.dot(selmap_ref[...], p_hi, preferred_element_type=F32)
           + jnp.dot(selmap_ref[...], p_lo, preferred_element_type=F32))

    jrow = lax.broadcasted_iota(jnp.int32, (n_sel, 1), 0)
    cur = lax.shift_right_logical(tq, SEL_SHIFT)
    forced = (jrow == 0) | (jrow == cur) | (jrow == cur - 1)
    causal = jrow * SEL_LEN <= tq
    val_ref[...] = jnp.where(forced, FORCE_SCORE, jnp.where(causal, imp, -FORCE_SCORE))
    cnt_ref[...] = jnp.zeros_like(cnt_ref)
    n_live = (qi + 1) * (TQ // SEL_LEN)
    n_grp = n_sel // SUBLANES
    for c in range(n_grp):

        @pl.when(c * SUBLANES < n_live)
        def _():
            for k in range(n_grp):
                rows = slice(k * SUBLANES, (k + 1) * SUBLANES)
                v = val_ref[rows, :]
                cnt = cnt_ref[rows, :]
                for jp in range(c * SUBLANES, (c + 1) * SUBLANES):
                    cand = val_ref[jp:jp + 1, :]
                    if k < c:
                        cnt = cnt + jnp.where(cand > v, 1.0, 0.0)
                    elif k > c:
                        cnt = cnt + jnp.where(cand >= v, 1.0, 0.0)
                    else:
                        own = lax.broadcasted_iota(jnp.int32, (SUBLANES, 1), 0) + k * SUBLANES
                        cnt = cnt + jnp.where(own > jp, jnp.where(cand >= v, 1.0, 0.0),
                                              jnp.where(cand > v, 1.0, 0.0))
                cnt_ref[rows, :] = cnt

    bias = jnp.where(cnt_ref[...] < float(N_SELECT), 0.0, NEG).astype(BF16)
    qs_sel = [jnp.concatenate([q, bias], axis=0) for q in qs]

    def sel_scores(kb, r):
        return jnp.dot(ks_ref[pl.ds(key_off(kb), KB), :], qs_sel[r], preferred_element_type=F32)

    def sel_values(kb, p):
        return jnp.dot(vst_ref[:, pl.ds(key_off(kb), KB)], p, preferred_element_type=F32)

    ROW_M, ROW_A = 0, 1

    def sel_stage(i, cur_slot, *, diag):
        nxt = 1 - cur_slot
        for r, hs in enumerate(heads):
            if not diag:
                s_ref[nxt, :, hs] = sel_scores(i + 1, r)
            pv_prev = sel_values(jnp.maximum(i - 1, 0), p_ref[nxt, :, hs])
            s = s_ref[cur_slot, :, hs]
            if diag:
                s = s + band_ref[0]
            m = st_ref[ROW_M:ROW_M + 1, hs]
            m_new = jnp.maximum(m, jnp.max(s, axis=0, keepdims=True))
            p_ref[cur_slot, :, hs] = jnp.exp2(s - m_new).astype(BF16)
            acc_ref[:, hs] = st_ref[ROW_A:ROW_A + 1, hs] * acc_ref[:, hs] + pv_prev
            st_ref[ROW_M:ROW_M + 1, hs] = m_new
            st_ref[ROW_A:ROW_A + 1, hs] = jnp.exp2(m - m_new)

    def sel_finish(i, cur_slot):
        for hs in heads:
            acc_ref[:, hs] = (st_ref[ROW_A:ROW_A + 1, hs] * acc_ref[:, hs]
                              + sel_values(i, p_ref[cur_slot, :, hs]))

    for r, hs in enumerate(heads):
        s_ref[0, :, hs] = sel_scores(0, r)
    p_ref[1] = jnp.zeros(p_ref.shape[1:], BF16)
    acc_ref[...] = jnp.zeros_like(acc_ref)
    st_ref[ROW_M:ROW_M + 1, :] = jnp.full((1, R * TQ), NEG, F32)
    st_ref[ROW_A:ROW_A + 1, :] = jnp.ones((1, R * TQ), F32)

    def sel_pair(j, c):
        sel_stage(2 * j, 0, diag=False)
        sel_stage(2 * j + 1, 1, diag=False)
        return c

    lax.fori_loop(0, lax.shift_right_logical(qi, 1), sel_pair, 0)
    odd = lax.rem(qi, 2) == 1

    @pl.when(odd)
    def _():
        sel_stage(qi - 1, 0, diag=False)
        sel_stage(qi, 1, diag=True)
        sel_finish(qi, 1)

    @pl.when(jnp.logical_not(odd))
    def _():
        sel_stage(qi, 0, diag=True)
        sel_finish(qi, 0)

    gl = jax.nn.sigmoid(glt_ref[...])
    res = []
    for r, hs in enumerate(heads):
        a_s = acc_ref[:, hs]
        a_w = acc_w[r]
        gate = [gl[r * N_BRANCH + br:r * N_BRANCH + br + 1, :] for br in range(N_BRANCH)]
        res.append(gate[0] * o_c[r]
                   + (gate[1] / a_s[d:d + 1, :]) * a_s[0:d, :]
                   + (gate[2] / a_w[d:d + 1, :]) * a_w[0:d, :])
    ng = ng_ref[...]
    out_ref[...] = jnp.concatenate(res, axis=0).T * (ng * jax.nn.sigmoid(ng))


def _nsa_tables(S):
    n_cmp = (S - CMP_LEN) // CMP_STRIDE + 1
    n_sel = S // SEL_LEN
    ci = np.arange(n_cmp)[:, None]
    sj = np.arange(n_sel)[None, :]
    overlap = (np.minimum(ci * CMP_STRIDE + CMP_LEN, sj * SEL_LEN + SEL_LEN)
               - np.maximum(ci * CMP_STRIDE, sj * SEL_LEN))
    sel_map = np.clip(overlap, 0, None) / CMP_STRIDE
    selmap_t = np.zeros((n_sel, _CMP_ROWS), np.float32)
    selmap_t[:, :n_cmp] = sel_map.T
    jl = np.arange(NSA_KB)[:, None]
    tl = np.arange(NSA_TQ)[None, :]
    band = np.stack([np.where(jl <= tl, 0.0, NEG), np.where(jl > tl, 0.0, NEG)]).astype(np.float32)
    rel = (np.arange(2 * _CMP_ROWS)[:, None] - _CMP_ROWS) * CMP_STRIDE + (CMP_LEN - 1)
    cmpmask = np.where(rel <= tl, 0.0, NEG).astype(np.float32)
    return selmap_t, band, cmpmask


def _nsa_attention(qt, kc, vct, ks, vst, kw, vwt, glt, ng):
    B, _, S = qt.shape
    G, R, d, TQ = NSA_KV_HEADS, NSA_GROUP, HEAD_DIM, NSA_TQ
    selmap_t, band, cmpmask = _nsa_tables(S)
    n_sel = S // SEL_LEN
    assert NSA_KB == TQ and WINDOW % NSA_KB == 0
    assert n_sel == d
    per_b_rows = lambda n: pl.BlockSpec((None, n, G * d), lambda b, g, q: (b, 0, 0))
    per_bg_cols = lambda rows, n: pl.BlockSpec((None, rows, n), lambda b, g, q: (b, g, 0))
    return pl.pallas_call(
        _nsa_kernel,
        grid=(B, G, S // TQ),
        in_specs=[pl.BlockSpec((None, R * d, TQ), lambda b, g, q: (b, g, q)),
                  per_b_rows(_CMP_ROWS), per_bg_cols(d, _CMP_ROWS),
                  pl.BlockSpec((None, None, S, 2 * d), lambda b, g, q: (b, g, 0, 0)),
                  per_bg_cols(V_ROWS, S), per_b_rows(S), per_bg_cols(V_ROWS, S),
                  pl.BlockSpec((None, GL_ROWS, TQ), lambda b, g, q: (b, g, q)),
                  pl.BlockSpec((None, TQ, R * d), lambda b, g, q: (b, q, g)),
                  pl.BlockSpec(selmap_t.shape, lambda b, g, q: (0, 0)),
                  pl.BlockSpec(band.shape, lambda b, g, q: (0, 0, 0)),
                  pl.BlockSpec(cmpmask.shape, lambda b, g, q: (0, 0))],
        out_specs=pl.BlockSpec((None, TQ, R * d), lambda b, g, q: (b, q, g)),
        out_shape=jax.ShapeDtypeStruct((B, S, NSA_WIDTH), F32),
        scratch_shapes=[pltpu.VMEM((n_sel, TQ), F32), pltpu.VMEM((n_sel, TQ), F32),
                        pltpu.VMEM((2, NSA_KB, R * TQ), F32),
                        pltpu.VMEM((2, NSA_KB, R * TQ), BF16),
                        pltpu.VMEM((V_ROWS, R * TQ), F32),
                        pltpu.VMEM((SUBLANES, R * TQ), F32)],
        compiler_params=_cparams(("parallel", "parallel", "arbitrary")),
        name="nsa_attention",
    )(qt, kc, vct, ks, vst, kw, vwt, glt, ng, jnp.asarray(selmap_t, dtype=BF16), jnp.asarray(band),
      jnp.asarray(cmpmask))


def _lru_kernel(lru_ref, cw_ref, cb_ref, wa_ref, ba_ref, wx_ref, bx_ref, lam_ref, out_ref,
                xext_ref, h_ref):
    ts, W, pad = LRU_TS, LRU_WIDTH, SUBLANES
    i = pl.program_id(1)

    @pl.when(i == 0)
    def _():
        xext_ref[0:pad, :] = jnp.zeros((pad, W), F32)
        h_ref[...] = jnp.zeros_like(h_ref)

    x = lru_ref[:, 0:W]
    gate = lru_ref[:, W:2 * W]
    xext_ref[pad:pad + ts, :] = x
    first = pad - (CONV_WIDTH - 1)
    xc = cb_ref[...] + cw_ref[0:1, :] * xext_ref[first:first + ts, :]
    for w in range(1, CONV_WIDTH):
        xc = xc + cw_ref[w:w + 1, :] * xext_ref[first + w:first + w + ts, :]
    xext_ref[0:pad, :] = x[ts - pad:ts, :]
    xcb = xc.astype(BF16)
    r = jax.nn.sigmoid(jnp.dot(xcb, wa_ref[...], preferred_element_type=F32) + ba_ref[...])
    gi = jax.nn.sigmoid(jnp.dot(xcb, wx_ref[...], preferred_element_type=F32) + bx_ref[...])
    nl = -lam_ref[...]
    softplus = jnp.maximum(nl, 0.0) + jnp.log1p(jnp.exp(-jnp.abs(nl)))
    log_a = (-LRU_C) * r * softplus
    a = jnp.exp(log_a)
    u = jnp.sqrt(-jnp.tanh(log_a) * (a * a + 1.0)) * (gi * xc)
    row = lax.broadcasted_iota(jnp.int32, (ts, 1), 0)
    step = 1
    while step < ts:
        keep = row >= step
        a_sh = jnp.where(keep, pltpu.roll(a, step, 0), 1.0)
        u_sh = jnp.where(keep, pltpu.roll(u, step, 0), 0.0)
        u = a * u_sh + u
        a = a * a_sh
        step *= 2
    h = u + a * h_ref[0:1, :]
    h_ref[0:1, :] = h[ts - 1:ts, :]
    out_ref[...] = h * (gate * jax.nn.sigmoid(gate))


def _block_diag(w):
    n, d, _ = w.shape
    return jnp.einsum('nde,nm->ndme', w, jnp.eye(n, dtype=w.dtype)).reshape(n * d, n * d)


def _rg_lru(lru, conv_w, conv_b, w_a, b_a, w_x, b_x, lam):
    B, S, _ = lru.shape
    ts, W = LRU_TS, LRU_WIDTH
    vec = lambda a: a.reshape(1, W)
    fixed = lambda r: pl.BlockSpec((r, W), lambda b, i: (0, 0))
    return pl.pallas_call(
        _lru_kernel,
        grid=(B, S // ts),
        in_specs=[pl.BlockSpec((None, ts, 2 * W), lambda b, i: (b, i, 0)),
                  fixed(CONV_WIDTH), fixed(1), fixed(W), fixed(1), fixed(W), fixed(1), fixed(1)],
        out_specs=pl.BlockSpec((None, ts, W), lambda b, i: (b, i, 0)),
        out_shape=jax.ShapeDtypeStruct((B, S, W), F32),
        scratch_shapes=[pltpu.VMEM((ts + SUBLANES, W), F32), pltpu.VMEM((SUBLANES, W), F32)],
        compiler_params=_cparams(("parallel", "arbitrary")),
        name="rg_lru",
    )(lru, conv_w, vec(conv_b), _block_diag(w_a).astype(BF16), vec(b_a),
      _block_diag(w_x).astype(BF16), vec(b_x), vec(lam))


def _out_kernel(x_ref, yr_ref, yn_ref, yl_ref, wo_ref, g_ref, b_ref, out_ref):
    o1, o2 = RET_WIDTH, RET_WIDTH + NSA_WIDTH
    y = jnp.dot(yr_ref[...].astype(BF16), wo_ref[0:o1, :], preferred_element_type=F32)
    y = y + jnp.dot(yn_ref[...].astype(BF16), wo_ref[o1:o2, :], preferred_element_type=F32)
    y = y + jnp.dot(yl_ref[...].astype(BF16), wo_ref[o2:D_MODEL, :], preferred_element_type=F32)
    z = DEEPNORM_ALPHA * x_ref[...] + y
    mu = jnp.mean(z, axis=-1, keepdims=True)
    zc = z - mu
    var = jnp.mean(zc * zc, axis=-1, keepdims=True)
    out_ref[...] = zc * lax.rsqrt(var + LN_EPS) * g_ref[...] + b_ref[...]


def _out_proj(x, y_ret, y_nsa, y_lru, w_out, ln_g, ln_b):
    B, S, _ = x.shape
    tm = OUT_TM
    row = lambda w: pl.BlockSpec((None, tm, w), lambda b, i: (b, i, 0))
    fixed = lambda r: pl.BlockSpec((r, D_MODEL), lambda b, i: (0, 0))
    return pl.pallas_call(
        _out_kernel,
        grid=(B, S // tm),
        in_specs=[row(D_MODEL), row(RET_WIDTH), row(NSA_WIDTH), row(LRU_WIDTH),
                  fixed(D_MODEL), fixed(1), fixed(1)],
        out_specs=row(D_MODEL),
        out_shape=jax.ShapeDtypeStruct((B, S, D_MODEL), F32),
        compiler_params=_cparams(("parallel", "parallel")),
        name="out_proj_ln",
    )(x, y_ret, y_nsa, y_lru, w_out.astype(BF16), ln_g.reshape(1, D_MODEL),
      ln_b.reshape(1, D_MODEL))


def _layer(x, w_in, w_out, ln_g, ln_b, cmp_pos, cmp_w, conv_w, conv_b, w_a, b_a, w_x, b_x, lam):
    wn, wt = _prep_in_weights(w_in)
    ret, kcn, vcn, ks, kw, ng, lru, qt, vst, vwt, glt = _project(x, wn, wt)
    y_ret = _retention(ret)
    prow, wl = _prep_cmp_weights(cmp_pos, cmp_w)
    kc, vct = _compress(kcn, vcn, prow, wl)
    y_nsa = _nsa_attention(qt, kc, vct, ks, vst, kw, vwt, glt, ng)
    y_lru = _rg_lru(lru, conv_w, conv_b, w_a, b_a, w_x, b_x, lam)
    return _out_proj(x, y_ret, y_nsa, y_lru, w_out, ln_g, ln_b)


def kernel(x, w_in, w_out, ln_g, ln_b, nsa_cmp_pos, nsa_cmp_w, lru_conv_w, lru_conv_b,
           lru_w_a, lru_b_a, lru_w_x, lru_b_x, lru_lambda):
    assert x.shape[1] % max(PROJ_TM, RET_CHUNK, LRU_TS, OUT_TM, NSA_TQ) == 0
    assert x.shape[1] // CMP_STRIDE == _CMP_ROWS
    for l in range(DEPTH):
        x = _layer(x, w_in[l], w_out[l], ln_g[l], ln_b[l], nsa_cmp_pos[l], nsa_cmp_w[l],
                   lru_conv_w[l], lru_conv_b[l], lru_w_a[l], lru_b_a[l], lru_w_x[l], lru_b_x[l],
                   lru_lambda[l])
    return x
```

```python
import functools
import math

import jax
import jax.numpy as jnp
import numpy as np
from jax import lax
from jax.experimental import pallas as pl
from jax.experimental.pallas import tpu as pltpu

F32 = jnp.float32
BF16 = jnp.bfloat16

D_MODEL = 1024
DEPTH = 2
HEAD_DIM = 64
RET_HEADS = 4
RET_WIDTH = RET_HEADS * HEAD_DIM
ROPE_BASE = 10000.0
NSA_HEADS = 8
NSA_KV_HEADS = 2
NSA_GROUP = NSA_HEADS // NSA_KV_HEADS
NSA_WIDTH = NSA_HEADS * HEAD_DIM
NSA_KV_WIDTH = NSA_KV_HEADS * HEAD_DIM
CMP_LEN = 32
CMP_STRIDE = 16
SEL_LEN = 64
N_SELECT = 16
WINDOW = 512
N_BRANCH = 3
FORCE_SCORE = 1.0e4
LRU_WIDTH = 256
LRU_BLOCKS = 4
LRU_BLOCK_DIM = LRU_WIDTH // LRU_BLOCKS
CONV_WIDTH = 4
LRU_C = 8.0
DEEPNORM_ALPHA = (2.0 * DEPTH) ** 0.25
LN_EPS = 1e-5
NEG = -1e30
LOG2E = math.log2(math.e)
SEL_SHIFT = int(math.log2(SEL_LEN))

IN_SIZES = (RET_WIDTH,) * 4 + (NSA_WIDTH,) + (NSA_KV_WIDTH,) * 6 + (
    NSA_WIDTH, NSA_HEADS * N_BRANCH, LRU_WIDTH, LRU_WIDTH)
IN_OFFS = tuple(int(v) for v in np.cumsum((0,) + IN_SIZES))

V7X_VMEM_LIMIT = 48 * 1024 * 1024
SUBLANES = 8
BF16_SUBLANES = 16
PROJ_TM = 256
RET_CHUNK = 256
LRU_TS = 256
OUT_TM = 512
NSA_TQ = 256
NSA_KB = 256
GL_ROWS = 16
V_ROWS = HEAD_DIM + BF16_SUBLANES

_CONTRACT_LAST = (((1,), (1,)), ((), ()))
_CONTRACT_FIRST = (((0,), (0,)), ((), ()))


def _cparams(sem):
    return pltpu.CompilerParams(dimension_semantics=sem, vmem_limit_bytes=V7X_VMEM_LIMIT)


_WN_COLS = (4 * RET_WIDTH, 2 * NSA_KV_WIDTH, NSA_KV_WIDTH, NSA_KV_WIDTH, NSA_WIDTH, 2 * LRU_WIDTH)
_WN_OFFS = tuple(int(v) for v in np.cumsum((0,) + _WN_COLS))
_WT_ROWS = (NSA_WIDTH, NSA_KV_WIDTH, NSA_KV_WIDTH, NSA_KV_HEADS * GL_ROWS)
_WT_OFFS = tuple(int(v) for v in np.cumsum((0,) + _WT_ROWS))


def _proj_kernel(x_ref, wn_ref, wt_ref, ret_ref, kcn_ref, vcn_ref, ks_ref, kw_ref, ng_ref, lru_ref,
                 qt_ref, vst_ref, vwt_ref, glt_ref):
    xb = x_ref[...].astype(BF16)

    def nat(i):
        return jnp.dot(xb, wn_ref[:, _WN_OFFS[i]:_WN_OFFS[i + 1]], preferred_element_type=F32)

    def tr(i):
        return lax.dot_general(wt_ref[_WT_OFFS[i]:_WT_OFFS[i + 1], :], xb, _CONTRACT_LAST,
                               preferred_element_type=F32)

    ret_ref[...] = nat(0)
    kvc = nat(1)
    kcn_ref[...] = kvc[:, 0:NSA_KV_WIDTH]
    vcn_ref[...] = kvc[:, NSA_KV_WIDTH:2 * NSA_KV_WIDTH]
    ks = nat(2)
    tm = ks.shape[0]
    key = pl.program_id(1) * tm + lax.broadcasted_iota(jnp.int32, (tm, 1), 0)
    blk = lax.broadcasted_iota(jnp.int32, (1, HEAD_DIM), 1)
    onehot = jnp.where(lax.shift_right_logical(key, SEL_SHIFT) == blk, 1.0, 0.0)
    for g in range(NSA_KV_HEADS):
        ks_ref[g] = jnp.concatenate([ks[:, g * HEAD_DIM:(g + 1) * HEAD_DIM], onehot],
                                    axis=1).astype(BF16)
    kw_ref[...] = nat(3).astype(BF16)
    ng_ref[...] = nat(4)
    lru_ref[...] = nat(5)
    qt_ref[...] = (tr(0) * LOG2E).astype(BF16)
    ones_rows = jnp.where(lax.broadcasted_iota(jnp.int32, (BF16_SUBLANES, tm), 0) == 0,
                          1.0, 0.0).astype(BF16)
    for ref, i in ((vst_ref, 1), (vwt_ref, 2)):
        vt = tr(i)
        for g in range(NSA_KV_HEADS):
            ref[g * V_ROWS:g * V_ROWS + HEAD_DIM, :] = (
                vt[g * HEAD_DIM:(g + 1) * HEAD_DIM, :].astype(BF16))
            ref[g * V_ROWS + HEAD_DIM:(g + 1) * V_ROWS, :] = ones_rows
    glt_ref[...] = tr(3)


def _project(x, wn, wt):
    B, S, _ = x.shape
    tm = PROJ_TM
    nn, nt = wn.shape[1], wt.shape[0]
    G = NSA_KV_HEADS
    row = lambda w: pl.BlockSpec((None, tm, w), lambda b, i: (b, i, 0))
    col = lambda r: pl.BlockSpec((None, r, tm), lambda b, i: (b, 0, i))
    kspec = pl.BlockSpec((None, G, tm, 2 * HEAD_DIM), lambda b, i: (b, 0, i, 0))
    out_shape = (
        jax.ShapeDtypeStruct((B, S, 4 * RET_WIDTH), F32),
        jax.ShapeDtypeStruct((B, S, NSA_KV_WIDTH), F32),
        jax.ShapeDtypeStruct((B, S, NSA_KV_WIDTH), F32),
        jax.ShapeDtypeStruct((B, G, S, 2 * HEAD_DIM), BF16),
        jax.ShapeDtypeStruct((B, S, NSA_KV_WIDTH), BF16),
        jax.ShapeDtypeStruct((B, S, NSA_WIDTH), F32),
        jax.ShapeDtypeStruct((B, S, 2 * LRU_WIDTH), F32),
        jax.ShapeDtypeStruct((B, NSA_WIDTH, S), BF16),
        jax.ShapeDtypeStruct((B, G * V_ROWS, S), BF16),
        jax.ShapeDtypeStruct((B, G * V_ROWS, S), BF16),
        jax.ShapeDtypeStruct((B, G * GL_ROWS, S), F32),
    )
    return pl.pallas_call(
        _proj_kernel,
        grid=(B, S // tm),
        in_specs=[row(D_MODEL),
                  pl.BlockSpec((D_MODEL, nn), lambda b, i: (0, 0)),
                  pl.BlockSpec((nt, D_MODEL), lambda b, i: (0, 0))],
        out_specs=(row(4 * RET_WIDTH), row(NSA_KV_WIDTH), row(NSA_KV_WIDTH), kspec,
                   row(NSA_KV_WIDTH), row(NSA_WIDTH),
                   row(2 * LRU_WIDTH), col(NSA_WIDTH), col(G * V_ROWS), col(G * V_ROWS),
                   col(G * GL_ROWS)),
        out_shape=out_shape,
        compiler_params=_cparams(("parallel", "parallel")),
        name="in_proj",
    )(x, wn, wt)


def _prep_in_weights(w_in):
    seg = [w_in[:, IN_OFFS[i]:IN_OFFS[i + 1]] for i in range(len(IN_SIZES))]
    (rq, rk, rv, rg, nq, nkc, nvc, nks, nvs, nkw, nvw, ng, ngl, lx, lg) = seg
    wn = jnp.concatenate([rq, rk, rv, rg, nkc, nvc, nks, nkw, ng, lx, lg], axis=1).astype(BF16)
    glt = ngl.T.reshape(NSA_KV_HEADS, NSA_GROUP * N_BRANCH, D_MODEL)
    glt = jnp.pad(glt, ((0, 0), (0, GL_ROWS - NSA_GROUP * N_BRANCH), (0, 0)))
    wt = jnp.concatenate([nq.T * (HEAD_DIM ** -0.5), nvs.T, nvw.T,
                          glt.reshape(NSA_KV_HEADS * GL_ROWS, D_MODEL)], axis=0).astype(BF16)
    return wn, wt


def _ret_kernel(ret_ref, cos_ref, sa_ref, sb_ref, qd_ref, kd_ref, dmat_ref, out_ref, state_ref,
                *, chunk_decay):
    c = pl.program_id(1)

    @pl.when(c == 0)
    def _():
        state_ref[...] = jnp.zeros_like(state_ref)

    W = RET_WIDTH
    q = ret_ref[:, 0:W]
    k = ret_ref[:, W:2 * W]
    v = ret_ref[:, 2 * W:3 * W]
    gate = ret_ref[:, 3 * W:4 * W]
    cos, sa, sb = cos_ref[...], sa_ref[...], sb_ref[...]
    half = HEAD_DIM // 2

    def rope(t):
        return t * cos + pltpu.roll(t, W - half, 1) * sa + pltpu.roll(t, half, 1) * sb

    qr = rope(q)
    kr = rope(k) * (HEAD_DIM ** -0.5)
    qd = qr * qd_ref[...]
    kd = kr * kd_ref[...]
    ys = []
    for h in range(RET_HEADS):
        sl = slice(h * HEAD_DIM, (h + 1) * HEAD_DIM)
        qb, kb, vb = qr[:, sl].astype(BF16), kr[:, sl].astype(BF16), v[:, sl].astype(BF16)
        s = lax.dot_general(qb, kb, _CONTRACT_LAST, preferred_element_type=F32) * dmat_ref[h]
        inner = jnp.dot(s.astype(BF16), vb, preferred_element_type=F32)
        st = state_ref[h]
        cross = jnp.dot(qd[:, sl].astype(BF16), st.astype(BF16), preferred_element_type=F32)
        y = inner + cross
        mu = jnp.mean(y, axis=-1, keepdims=True)
        yc = y - mu
        var = jnp.mean(yc * yc, axis=-1, keepdims=True)
        ys.append(yc * lax.rsqrt(var + LN_EPS))
        kv = lax.dot_general(kd[:, sl].astype(BF16), vb, _CONTRACT_FIRST,
                             preferred_element_type=F32)
        state_ref[h] = st * chunk_decay[h] + kv
    y = jnp.concatenate(ys, axis=1)
    out_ref[...] = (y * (gate * jax.nn.sigmoid(gate))).astype(out_ref.dtype)


def _ret_tables(S):
    C, H, d = RET_CHUNK, RET_HEADS, HEAD_DIM
    half = d // 2
    inv = 1.0 / (ROPE_BASE ** (np.arange(half, dtype=np.float64) / half))
    ang = np.arange(S, dtype=np.float64)[:, None] * inv[None, :]
    cos_h = np.concatenate([np.cos(ang), np.cos(ang)], axis=1)
    sin_lo = np.concatenate([-np.sin(ang), np.zeros_like(ang)], axis=1)
    sin_hi = np.concatenate([np.zeros_like(ang), np.sin(ang)], axis=1)
    tile = lambda a: np.tile(a, (1, H)).astype(np.float32)
    log_g = np.log(1.0 - 2.0 ** (-5.0 - np.arange(H, dtype=np.float64)))
    idx = np.arange(C, dtype=np.float64)
    diff = idx[:, None] - idx[None, :]
    dmat = np.where(diff >= 0, np.exp(np.maximum(diff, 0.0)[None] * log_g[:, None, None]), 0.0)
    kd = np.exp((C - 1.0 - idx)[:, None] * log_g[None, :])
    qd = np.exp((idx + 1.0)[:, None] * log_g[None, :])
    rep = lambda a: np.repeat(a, d, axis=1).astype(np.float32)
    chunk_decay = tuple(float(v) for v in np.exp(C * log_g))
    return (tile(cos_h), tile(sin_lo), tile(sin_hi), rep(qd), rep(kd),
            dmat.astype(np.float32), chunk_decay)


def _retention(ret):
    B, S, _ = ret.shape
    C, W = RET_CHUNK, RET_WIDTH
    cos, sa, sb, qd, kd, dmat, chunk_decay = _ret_tables(S)
    pos = pl.BlockSpec((C, W), lambda b, c: (c, 0))
    fixed = pl.BlockSpec((C, W), lambda b, c: (0, 0))
    return pl.pallas_call(
        functools.partial(_ret_kernel, chunk_decay=chunk_decay),
        grid=(B, S // C),
        in_specs=[pl.BlockSpec((None, C, 4 * W), lambda b, c: (b, c, 0)), pos, pos, pos,
                  fixed, fixed, pl.BlockSpec((RET_HEADS, C, C), lambda b, c: (0, 0, 0))],
        out_specs=pl.BlockSpec((None, C, W), lambda b, c: (b, c, 0)),
        out_shape=jax.ShapeDtypeStruct((B, S, W), BF16),
        scratch_shapes=[pltpu.VMEM((RET_HEADS, HEAD_DIM, HEAD_DIM), F32)],
        compiler_params=_cparams(("parallel", "arbitrary")),
        name="retention",
    )(ret, jnp.asarray(cos), jnp.asarray(sa), jnp.asarray(sb), jnp.asarray(qd), jnp.asarray(kd),
      jnp.asarray(dmat))


_CMP_ROWS = 256


def _cmp_kernel(kcn_ref, vcn_ref, pos_ref, wl_ref, kc_ref, vct_ref):
    n, W = _CMP_ROWS, NSA_KV_WIDTH
    res = []
    for i, ref in enumerate((kcn_ref, vcn_ref)):
        out = jnp.zeros((n, 2 * W), F32)
        cst = jnp.zeros((SUBLANES, 2 * W), F32)
        for l in range(CMP_STRIDE):
            x = ref[pl.ds(l, n, stride=CMP_STRIDE), :].astype(BF16)
            out = out + jnp.dot(x, wl_ref[i, l], preferred_element_type=F32)
            cst = cst + jnp.dot(pos_ref[i, l].astype(BF16), wl_ref[i, l],
                                preferred_element_type=F32)
        const = cst[0:1, 0:W] + cst[1:2, W:2 * W]
        res.append(out[:, 0:W] + pltpu.roll(out[:, W:2 * W], n - 1, 0) + const)
    kc_ref[...] = res[0].astype(BF16)
    vct_ref[...] = res[1].T.astype(BF16)


def _prep_cmp_weights(cmp_pos, cmp_w):
    L2, d, G = CMP_STRIDE, HEAD_DIM, NSA_KV_HEADS
    assert G == 2
    w5 = cmp_w.reshape(2, 2, L2, d, d)
    z = jnp.zeros((2, L2, d, d), F32)

    def both_groups(w):
        return jnp.concatenate([jnp.concatenate([w, z], axis=-1),
                                jnp.concatenate([z, w], axis=-1)], axis=-2)

    wl = jnp.concatenate([both_groups(w5[:, 0]), both_groups(w5[:, 1])], axis=-1).astype(BF16)
    p4 = cmp_pos.reshape(2, 2, L2, d)
    prow = jnp.concatenate([p4, p4], axis=-1).transpose(0, 2, 1, 3)
    prow = jnp.pad(prow, ((0, 0), (0, 0), (0, SUBLANES - 2), (0, 0)))
    return prow, wl


def _compress(kcn, vcn, prow, wl):
    B, S, W = kcn.shape
    n = _CMP_ROWS
    return pl.pallas_call(
        _cmp_kernel,
        grid=(B,),
        in_specs=[pl.BlockSpec((None, S, W), lambda b: (b, 0, 0)),
                  pl.BlockSpec((None, S, W), lambda b: (b, 0, 0)),
                  pl.BlockSpec(prow.shape, lambda b: (0, 0, 0, 0)),
                  pl.BlockSpec(wl.shape, lambda b: (0, 0, 0, 0))],
        out_specs=(pl.BlockSpec((None, n, NSA_KV_WIDTH), lambda b: (b, 0, 0)),
                   pl.BlockSpec((None, NSA_KV_WIDTH, n), lambda b: (b, 0, 0))),
        out_shape=(jax.ShapeDtypeStruct((B, n, NSA_KV_WIDTH), BF16),
                   jax.ShapeDtypeStruct((B, NSA_KV_WIDTH, n), BF16)),
        compiler_params=_cparams(("parallel",)),
        name="nsa_compress",
    )(kcn, vcn, prow, wl)


def _nsa_kernel(qt_ref, kc_ref, vct_ref, ks_ref, vst_ref, kw_ref, vwt_ref, glt_ref, ng_ref,
                selmap_ref, band_ref, cmpmask_ref, out_ref, val_ref, cnt_ref, s_ref, p_ref, acc_ref,
                st_ref):
    TQ, KB, R, d = NSA_TQ, NSA_KB, NSA_GROUP, HEAD_DIM
    n_sel = selmap_ref.shape[0]
    g = pl.program_id(1)
    qi = pl.program_id(2)
    t0 = qi * TQ
    heads = [slice(r * TQ, (r + 1) * TQ) for r in range(R)]
    qs = [qt_ref[r * d:(r + 1) * d, :] for r in range(R)]
    zq = jnp.zeros_like(qs[0])
    qs_nat = [jnp.concatenate([jnp.where(g == gg, q, zq) for gg in range(NSA_KV_HEADS)], axis=0)
              for q in qs]

    def key_off(kb):
        return pl.multiple_of(kb * KB, KB)

    n_wb = WINDOW // KB + 1
    w_kb = [qi - (n_wb - 1) + w for w in range(n_wb)]
    w_off = [key_off(jnp.maximum(kb, 0)) for kb in w_kb]
    w_bias = [band_ref[1]] + [None] * (n_wb - 2) + [band_ref[0]]
    qs_nat_all = jnp.concatenate(qs_nat, axis=1)

    n_cmp = kc_ref.shape[0]
    cmp_off = pl.multiple_of(n_cmp - qi * (TQ // CMP_STRIDE), TQ // CMP_STRIDE)
    cmp_bias = jnp.concatenate([cmpmask_ref[pl.ds(cmp_off, n_cmp), :]] * R, axis=1)
    tq = t0 + lax.broadcasted_iota(jnp.int32, (1, TQ), 1)
    seen = jnp.concatenate([tq >= CMP_LEN - 1] * R, axis=1)
    s = jnp.dot(kc_ref[...], qs_nat_all, preferred_element_type=F32) + cmp_bias

    w_s = []
    for w in range(n_wb):
        s_w = jnp.dot(kw_ref[pl.ds(w_off[w], KB), :], qs_nat_all, preferred_element_type=F32)
        if w_bias[w] is not None:
            s_w = s_w + jnp.concatenate([w_bias[w]] * R, axis=1)
        if w < n_wb - 1:
            s_w = s_w + jnp.where(w_kb[w] >= 0, 0.0, NEG)
        w_s.append(s_w)

    p = jnp.exp2(s - jnp.max(s, axis=0, keepdims=True))
    p = p * jnp.where(seen, 1.0 / jnp.sum(p, axis=0, keepdims=True), 0.0)
    o_c_all = jnp.dot(vct_ref[...], p.astype(BF16), preferred_element_type=F32)
    o_c = [o_c_all[:, hs] for hs in heads]
    psum = functools.reduce(jnp.add, [p[:, hs] for hs in heads])

    p_hi = psum.astype(BF16)
    p_lo = (psum - p_hi.astype(F32)).astype(BF16)
    imp = (jnp.dot(selmap_ref[...], p_hi, preferred_element_type=F32)
           + jnp.dot(selmap_ref[...], p_lo, preferred_element_type=F32))

    jrow = lax.broadcasted_iota(jnp.int32, (n_sel, 1), 0)
    cur = lax.shift_right_logical(tq, SEL_SHIFT)
    forced = (jrow == 0) | (jrow == cur) | (jrow == cur - 1)
    causal = jrow * SEL_LEN <= tq
    val_ref[...] = jnp.where(forced, FORCE_SCORE, jnp.where(causal, imp, -FORCE_SCORE))
    cnt_ref[...] = jnp.zeros_like(cnt_ref)
    n_live = (qi + 1) * (TQ // SEL_LEN)
    n_grp = n_sel // SUBLANES
    for c in range(n_grp):

        @pl.when(c * SUBLANES < n_live)
        def _():
            groups = [slice(k * SUBLANES, (k + 1) * SUBLANES) for k in range(n_grp)]
            vals = [val_ref[rows, :] for rows in groups]
            cnts = [cnt_ref[rows, :] for rows in groups]
            own = lax.broadcasted_iota(jnp.int32, (SUBLANES, 1), 0) + c * SUBLANES
            for jp in range(c * SUBLANES, (c + 1) * SUBLANES):
                cand = jnp.broadcast_to(val_ref[jp:jp + 1, :], (SUBLANES, TQ))
                for k in range(n_grp):
                    if k < c:
                        hit = jnp.where(cand > vals[k], 1.0, 0.0)
                    elif k > c:
                        hit = jnp.where(cand >= vals[k], 1.0, 0.0)
                    else:
                        hit = jnp.where(own > jp, jnp.where(cand >= vals[k], 1.0, 0.0),
                                        jnp.where(cand > vals[k], 1.0, 0.0))
                    cnts[k] = cnts[k] + hit
            for rows, cnt in zip(groups, cnts):
                cnt_ref[rows, :] = cnt

    bias = jnp.where(cnt_ref[...] < float(N_SELECT), 0.0, NEG).astype(BF16)
    qs_sel = [jnp.concatenate([q, bias], axis=0) for q in qs]

    def sel_scores(kb, r):
        return jnp.dot(ks_ref[pl.ds(key_off(kb), KB), :], qs_sel[r], preferred_element_type=F32)

    def sel_values(kb, p):
        return jnp.dot(vst_ref[:, pl.ds(key_off(kb), KB)], p, preferred_element_type=F32)

    ROW_M, ROW_A = 0, 1

    def sel_stage(i, cur_slot, *, diag):
        nxt = 1 - cur_slot
        for r, hs in enumerate(heads):
            s = s_ref[cur_slot, :, hs]
            if diag:
                s = s + band_ref[0]
            m = st_ref[ROW_M:ROW_M + 1, hs]
            m_new = jnp.maximum(m, jnp.max(s, axis=0, keepdims=True))
            p_ref[cur_slot, :, hs] = jnp.exp2(s - m_new).astype(BF16)
            pv_prev = sel_values(jnp.maximum(i - 1, 0), p_ref[nxt, :, hs])
            if not diag:
                s_ref[nxt, :, hs] = sel_scores(i + 1, r)
            acc_ref[:, hs] = st_ref[ROW_A:ROW_A + 1, hs] * acc_ref[:, hs] + pv_prev
            st_ref[ROW_M:ROW_M + 1, hs] = m_new
            st_ref[ROW_A:ROW_A + 1, hs] = jnp.exp2(m - m_new)

    def sel_finish(i, cur_slot):
        for hs in heads:
            acc_ref[:, hs] = (st_ref[ROW_A:ROW_A + 1, hs] * acc_ref[:, hs]
                              + sel_values(i, p_ref[cur_slot, :, hs]))

    for r, hs in enumerate(heads):
        s_ref[0, :, hs] = sel_scores(0, r)
    p_ref[1] = jnp.zeros(p_ref.shape[1:], BF16)
    acc_ref[...] = jnp.zeros_like(acc_ref)
    st_ref[ROW_M:ROW_M + 1, :] = jnp.full((1, R * TQ), NEG, F32)
    st_ref[ROW_A:ROW_A + 1, :] = jnp.ones((1, R * TQ), F32)

    m_w = functools.reduce(jnp.maximum, [jnp.max(s_w, axis=0, keepdims=True) for s_w in w_s])
    acc_w_all = functools.reduce(jnp.add, [
        jnp.dot(vwt_ref[:, pl.ds(w_off[w], KB)], jnp.exp2(w_s[w] - m_w).astype(BF16),
                preferred_element_type=F32) for w in range(n_wb)])
    acc_w = [acc_w_all[:, hs] for hs in heads]

    def sel_pair(j, c):
        sel_stage(2 * j, 0, diag=False)
        sel_stage(2 * j + 1, 1, diag=False)
        return c

    lax.fori_loop(0, lax.shift_right_logical(qi, 1), sel_pair, 0)
    odd = lax.rem(qi, 2) == 1

    @pl.when(odd)
    def _():
        sel_stage(qi - 1, 0, diag=False)
        sel_stage(qi, 1, diag=True)
        sel_finish(qi, 1)

    @pl.when(jnp.logical_not(odd))
    def _():
        sel_stage(qi, 0, diag=True)
        sel_finish(qi, 0)

    gl = jax.nn.sigmoid(glt_ref[...])
    res = []
    for r, hs in enumerate(heads):
        a_s = acc_ref[:, hs]
        a_w = acc_w[r]
        gate = [gl[r * N_BRANCH + br:r * N_BRANCH + br + 1, :] for br in range(N_BRANCH)]
        res.append(gate[0] * o_c[r]
                   + (gate[1] / a_s[d:d + 1, :]) * a_s[0:d, :]
                   + (gate[2] / a_w[d:d + 1, :]) * a_w[0:d, :])
    ng = ng_ref[...]
    out_ref[...] = (jnp.concatenate(res, axis=0).T
                    * (ng * jax.nn.sigmoid(ng))).astype(out_ref.dtype)


def _nsa_tables(S):
    n_cmp = (S - CMP_LEN) // CMP_STRIDE + 1
    n_sel = S // SEL_LEN
    ci = np.arange(n_cmp)[:, None]
    sj = np.arange(n_sel)[None, :]
    overlap = (np.minimum(ci * CMP_STRIDE + CMP_LEN, sj * SEL_LEN + SEL_LEN)
               - np.maximum(ci * CMP_STRIDE, sj * SEL_LEN))
    sel_map = np.clip(overlap, 0, None) / CMP_STRIDE
    selmap_t = np.zeros((n_sel, _CMP_ROWS), np.float32)
    selmap_t[:, :n_cmp] = sel_map.T
    jl = np.arange(NSA_KB)[:, None]
    tl = np.arange(NSA_TQ)[None, :]
    band = np.stack([np.where(jl <= tl, 0.0, NEG), np.where(jl > tl, 0.0, NEG)]).astype(np.float32)
    rel = (np.arange(2 * _CMP_ROWS)[:, None] - _CMP_ROWS) * CMP_STRIDE + (CMP_LEN - 1)
    cmpmask = np.where(rel <= tl, 0.0, NEG).astype(np.float32)
    return selmap_t, band, cmpmask


def _nsa_attention(qt, kc, vct, ks, vst, kw, vwt, glt, ng):
    B, _, S = qt.shape
    G, R, d, TQ = NSA_KV_HEADS, NSA_GROUP, HEAD_DIM, NSA_TQ
    selmap_t, band, cmpmask = _nsa_tables(S)
    n_sel = S // SEL_LEN
    assert NSA_KB == TQ and WINDOW % NSA_KB == 0
    assert n_sel == d
    per_b_rows = lambda n: pl.BlockSpec((None, n, G * d), lambda b, g, q: (b, 0, 0))
    per_bg_cols = lambda rows, n: pl.BlockSpec((None, rows, n), lambda b, g, q: (b, g, 0))
    return pl.pallas_call(
        _nsa_kernel,
        grid=(B, G, S // TQ),
        in_specs=[pl.BlockSpec((None, R * d, TQ), lambda b, g, q: (b, g, q)),
                  per_b_rows(_CMP_ROWS), per_bg_cols(d, _CMP_ROWS),
                  pl.BlockSpec((None, None, S, 2 * d), lambda b, g, q: (b, g, 0, 0)),
                  per_bg_cols(V_ROWS, S), per_b_rows(S), per_bg_cols(V_ROWS, S),
                  pl.BlockSpec((None, GL_ROWS, TQ), lambda b, g, q: (b, g, q)),
                  pl.BlockSpec((None, TQ, R * d), lambda b, g, q: (b, q, g)),
                  pl.BlockSpec(selmap_t.shape, lambda b, g, q: (0, 0)),
                  pl.BlockSpec(band.shape, lambda b, g, q: (0, 0, 0)),
                  pl.BlockSpec(cmpmask.shape, lambda b, g, q: (0, 0))],
        out_specs=pl.BlockSpec((None, TQ, R * d), lambda b, g, q: (b, q, g)),
        out_shape=jax.ShapeDtypeStruct((B, S, NSA_WIDTH), BF16),
        scratch_shapes=[pltpu.VMEM((n_sel, TQ), F32), pltpu.VMEM((n_sel, TQ), F32),
                        pltpu.VMEM((2, NSA_KB, R * TQ), F32),
                        pltpu.VMEM((2, NSA_KB, R * TQ), BF16),
                        pltpu.VMEM((V_ROWS, R * TQ), F32),
                        pltpu.VMEM((SUBLANES, R * TQ), F32)],
        compiler_params=_cparams(("parallel", "parallel", "arbitrary")),
        name="nsa_attention",
    )(qt, kc, vct, ks, vst, kw, vwt, glt, ng, jnp.asarray(selmap_t, dtype=BF16), jnp.asarray(band),
      jnp.asarray(cmpmask))


def _lru_kernel(lru_ref, cw_ref, cb_ref, wa_ref, ba_ref, wx_ref, bx_ref, lam_ref, out_ref,
                xext_ref, h_ref):
    ts, W, pad = LRU_TS, LRU_WIDTH, SUBLANES
    i = pl.program_id(1)

    @pl.when(i == 0)
    def _():
        xext_ref[0:pad, :] = jnp.zeros((pad, W), F32)
        h_ref[...] = jnp.zeros_like(h_ref)

    x = lru_ref[:, 0:W]
    gate = lru_ref[:, W:2 * W]
    xext_ref[pad:pad + ts, :] = x
    first = pad - (CONV_WIDTH - 1)
    xc = cb_ref[...] + cw_ref[0:1, :] * xext_ref[first:first + ts, :]
    for w in range(1, CONV_WIDTH):
        xc = xc + cw_ref[w:w + 1, :] * xext_ref[first + w:first + w + ts, :]
    xext_ref[0:pad, :] = x[ts - pad:ts, :]
    xcb = xc.astype(BF16)
    r = jax.nn.sigmoid(jnp.dot(xcb, wa_ref[...], preferred_element_type=F32) + ba_ref[...])
    gi = jax.nn.sigmoid(jnp.dot(xcb, wx_ref[...], preferred_element_type=F32) + bx_ref[...])
    nl = -lam_ref[...]
    softplus = jnp.maximum(nl, 0.0) + jnp.log1p(jnp.exp(-jnp.abs(nl)))
    log_a = (-LRU_C) * r * softplus
    a = jnp.exp(log_a)
    u = jnp.sqrt(-jnp.tanh(log_a) * (a * a + 1.0)) * (gi * xc)
    row = lax.broadcasted_iota(jnp.int32, (ts, 1), 0)
    step = 1
    while step < ts:
        keep = row >= step
        a_sh = jnp.where(keep, pltpu.roll(a, step, 0), 1.0)
        u_sh = jnp.where(keep, pltpu.roll(u, step, 0), 0.0)
        u = a * u_sh + u
        a = a * a_sh
        step *= 2
    h = u + a * h_ref[0:1, :]
    h_ref[0:1, :] = h[ts - 1:ts, :]
    out_ref[...] = (h * (gate * jax.nn.sigmoid(gate))).astype(out_ref.dtype)


def _block_diag(w):
    n, d, _ = w.shape
    z = jnp.zeros((d, d), w.dtype)
    return jnp.concatenate(
        [jnp.concatenate([w[i] if j == i else z for j in range(n)], axis=1) for i in range(n)],
        axis=0)


def _rg_lru(lru, conv_w, conv_b, w_a, b_a, w_x, b_x, lam):
    B, S, _ = lru.shape
    ts, W = LRU_TS, LRU_WIDTH
    vec = lambda a: a.reshape(1, W)
    fixed = lambda r: pl.BlockSpec((r, W), lambda b, i: (0, 0))
    return pl.pallas_call(
        _lru_kernel,
        grid=(B, S // ts),
        in_specs=[pl.BlockSpec((None, ts, 2 * W), lambda b, i: (b, i, 0)),
                  fixed(CONV_WIDTH), fixed(1), fixed(W), fixed(1), fixed(W), fixed(1), fixed(1)],
        out_specs=pl.BlockSpec((None, ts, W), lambda b, i: (b, i, 0)),
        out_shape=jax.ShapeDtypeStruct((B, S, W), BF16),
        scratch_shapes=[pltpu.VMEM((ts + SUBLANES, W), F32), pltpu.VMEM((SUBLANES, W), F32)],
        compiler_params=_cparams(("parallel", "arbitrary")),
        name="rg_lru",
    )(lru, conv_w, vec(conv_b), _block_diag(w_a).astype(BF16), vec(b_a),
      _block_diag(w_x).astype(BF16), vec(b_x), vec(lam))


def _out_kernel(x_ref, yr_ref, yn_ref, yl_ref, wo_ref, g_ref, b_ref, out_ref):
    o1, o2 = RET_WIDTH, RET_WIDTH + NSA_WIDTH
    y = jnp.dot(yr_ref[...].astype(BF16), wo_ref[0:o1, :], preferred_element_type=F32)
    y = y + jnp.dot(yn_ref[...].astype(BF16), wo_ref[o1:o2, :], preferred_element_type=F32)
    y = y + jnp.dot(yl_ref[...].astype(BF16), wo_ref[o2:D_MODEL, :], preferred_element_type=F32)
    z = DEEPNORM_ALPHA * x_ref[...] + y
    mu = jnp.mean(z, axis=-1, keepdims=True)
    zc = z - mu
    var = jnp.mean(zc * zc, axis=-1, keepdims=True)
    out_ref[...] = zc * lax.rsqrt(var + LN_EPS) * g_ref[...] + b_ref[...]


def _out_proj(x, y_ret, y_nsa, y_lru, w_out, ln_g, ln_b):
    B, S, _ = x.shape
    tm = OUT_TM
    row = lambda w: pl.BlockSpec((None, tm, w), lambda b, i: (b, i, 0))
    fixed = lambda r: pl.BlockSpec((r, D_MODEL), lambda b, i: (0, 0))
    return pl.pallas_call(
        _out_kernel,
        grid=(B, S // tm),
        in_specs=[row(D_MODEL), row(RET_WIDTH), row(NSA_WIDTH), row(LRU_WIDTH),
                  fixed(D_MODEL), fixed(1), fixed(1)],
        out_specs=row(D_MODEL),
        out_shape=jax.ShapeDtypeStruct((B, S, D_MODEL), F32),
        compiler_params=_cparams(("parallel", "parallel")),
        name="out_proj_ln",
    )(x, y_ret, y_nsa, y_lru, w_out.astype(BF16), ln_g.reshape(1, D_MODEL),
      ln_b.reshape(1, D_MODEL))


def _layer(x, w_in, w_out, ln_g, ln_b, cmp_pos, cmp_w, conv_w, conv_b, w_a, b_a, w_x, b_x, lam):
    wn, wt = _prep_in_weights(w_in)
    ret, kcn, vcn, ks, kw, ng, lru, qt, vst, vwt, glt = _project(x, wn, wt)
    y_ret = _retention(ret)
    prow, wl = _prep_cmp_weights(cmp_pos, cmp_w)
    kc, vct = _compress(kcn, vcn, prow, wl)
    y_nsa = _nsa_attention(qt, kc, vct, ks, vst, kw, vwt, glt, ng)
    y_lru = _rg_lru(lru, conv_w, conv_b, w_a, b_a, w_x, b_x, lam)
    return _out_proj(x, y_ret, y_nsa, y_lru, w_out, ln_g, ln_b)


def kernel(x, w_in, w_out, ln_g, ln_b, nsa_cmp_pos, nsa_cmp_w, lru_conv_w, lru_conv_b,
           lru_w_a, lru_b_a, lru_w_x, lru_b_x, lru_lambda):
    assert x.shape[1] % max(PROJ_TM, RET_CHUNK, LRU_TS, OUT_TM, NSA_TQ) == 0
    assert x.shape[1] // CMP_STRIDE == _CMP_ROWS
    for l in range(DEPTH):
        x = _layer(x, w_in[l], w_out[l], ln_g[l], ln_b[l], nsa_cmp_pos[l], nsa_cmp_w[l],
                   lru_conv_w[l], lru_conv_b[l], lru_w_a[l], lru_b_a[l], lru_w_x[l], lru_b_x[l],
                   lru_lambda[l])
    return x
```

```python
import functools
import math

import jax
import jax.numpy as jnp
import numpy as np
from jax import lax
from jax.experimental import pallas as pl
from jax.experimental.pallas import tpu as pltpu

F32 = jnp.float32
BF16 = jnp.bfloat16

D_MODEL = 1024
DEPTH = 2
HEAD_DIM = 64
RET_HEADS = 4
RET_WIDTH = RET_HEADS * HEAD_DIM
ROPE_BASE = 10000.0
NSA_HEADS = 8
NSA_KV_HEADS = 2
NSA_GROUP = NSA_HEADS // NSA_KV_HEADS
NSA_WIDTH = NSA_HEADS * HEAD_DIM
NSA_KV_WIDTH = NSA_KV_HEADS * HEAD_DIM
CMP_LEN = 32
CMP_STRIDE = 16
SEL_LEN = 64
N_SELECT = 16
WINDOW = 512
N_BRANCH = 3
FORCE_SCORE = 1.0e4
LRU_WIDTH = 256
LRU_BLOCKS = 4
LRU_BLOCK_DIM = LRU_WIDTH // LRU_BLOCKS
CONV_WIDTH = 4
LRU_C = 8.0
DEEPNORM_ALPHA = (2.0 * DEPTH) ** 0.25
LN_EPS = 1e-5
NEG = -1e30
LOG2E = math.log2(math.e)
SEL_SHIFT = int(math.log2(SEL_LEN))

IN_SIZES = (RET_WIDTH,) * 4 + (NSA_WIDTH,) + (NSA_KV_WIDTH,) * 6 + (
    NSA_WIDTH, NSA_HEADS * N_BRANCH, LRU_WIDTH, LRU_WIDTH)
IN_OFFS = tuple(int(v) for v in np.cumsum((0,) + IN_SIZES))

V7X_VMEM_LIMIT = 48 * 1024 * 1024
SUBLANES = 8
LANES = 128
BF16_SUBLANES = 16
PROJ_TM = 512
RET_CHUNK = 256
LRU_TS = 256
OUT_TM = 512
NSA_TQ = 256
NSA_KB = 256
GL_ROWS = 16
V_ROWS = HEAD_DIM + BF16_SUBLANES

_CONTRACT_LAST = (((1,), (1,)), ((), ()))
_CONTRACT_FIRST = (((0,), (0,)), ((), ()))


def _cparams(sem):
    return pltpu.CompilerParams(dimension_semantics=sem, vmem_limit_bytes=V7X_VMEM_LIMIT)


_WN_COLS = (4 * RET_WIDTH, 2 * NSA_KV_WIDTH, NSA_KV_WIDTH, NSA_KV_WIDTH, NSA_WIDTH, 2 * LRU_WIDTH)
_WN_OFFS = tuple(int(v) for v in np.cumsum((0,) + _WN_COLS))
_WT_ROWS = (NSA_WIDTH, NSA_KV_WIDTH, NSA_KV_WIDTH, NSA_KV_HEADS * GL_ROWS)
_WT_OFFS = tuple(int(v) for v in np.cumsum((0,) + _WT_ROWS))


def _proj_kernel(x_ref, wn_ref, wt_ref, ret_ref, kcn_ref, vcn_ref, ks_ref, kw_ref, ng_ref, lru_ref,
                 qt_ref, vst_ref, vwt_ref, glt_ref):
    xb = x_ref[...].astype(BF16)

    def nat(i):
        return jnp.dot(xb, wn_ref[:, _WN_OFFS[i]:_WN_OFFS[i + 1]], preferred_element_type=F32)

    def tr(i):
        return lax.dot_general(wt_ref[_WT_OFFS[i]:_WT_OFFS[i + 1], :], xb, _CONTRACT_LAST,
                               preferred_element_type=F32)

    ret_ref[...] = nat(0)
    kvc = nat(1)
    kcn_ref[...] = kvc[:, 0:NSA_KV_WIDTH]
    vcn_ref[...] = kvc[:, NSA_KV_WIDTH:2 * NSA_KV_WIDTH]
    ks = nat(2)
    tm = ks.shape[0]
    key = pl.program_id(1) * tm + lax.broadcasted_iota(jnp.int32, (tm, 1), 0)
    blk = lax.broadcasted_iota(jnp.int32, (1, HEAD_DIM), 1)
    onehot = jnp.where(lax.shift_right_logical(key, SEL_SHIFT) == blk, 1.0, 0.0)
    for g in range(NSA_KV_HEADS):
        ks_ref[g] = jnp.concatenate([ks[:, g * HEAD_DIM:(g + 1) * HEAD_DIM], onehot],
                                    axis=1).astype(BF16)
    kw_ref[...] = nat(3).astype(BF16)
    ng_ref[...] = nat(4)
    lru_ref[...] = nat(5)
    qt_ref[...] = (tr(0) * LOG2E).astype(BF16)
    ones_rows = jnp.where(lax.broadcasted_iota(jnp.int32, (BF16_SUBLANES, tm), 0) == 0,
                          1.0, 0.0).astype(BF16)
    for ref, i in ((vst_ref, 1), (vwt_ref, 2)):
        vt = tr(i)
        for g in range(NSA_KV_HEADS):
            ref[g * V_ROWS:g * V_ROWS + HEAD_DIM, :] = (
                vt[g * HEAD_DIM:(g + 1) * HEAD_DIM, :].astype(BF16))
            ref[g * V_ROWS + HEAD_DIM:(g + 1) * V_ROWS, :] = ones_rows
    glt_ref[...] = tr(3)


def _project(x, wn, wt):
    B, S, _ = x.shape
    tm = PROJ_TM
    nn, nt = wn.shape[1], wt.shape[0]
    G = NSA_KV_HEADS
    row = lambda w: pl.BlockSpec((None, tm, w), lambda b, i: (b, i, 0))
    col = lambda r: pl.BlockSpec((None, r, tm), lambda b, i: (b, 0, i))
    kspec = pl.BlockSpec((None, G, tm, 2 * HEAD_DIM), lambda b, i: (b, 0, i, 0))
    out_shape = (
        jax.ShapeDtypeStruct((B, S, 4 * RET_WIDTH), F32),
        jax.ShapeDtypeStruct((B, S, NSA_KV_WIDTH), F32),
        jax.ShapeDtypeStruct((B, S, NSA_KV_WIDTH), F32),
        jax.ShapeDtypeStruct((B, G, S, 2 * HEAD_DIM), BF16),
        jax.ShapeDtypeStruct((B, S, NSA_KV_WIDTH), BF16),
        jax.ShapeDtypeStruct((B, S, NSA_WIDTH), F32),
        jax.ShapeDtypeStruct((B, S, 2 * LRU_WIDTH), F32),
        jax.ShapeDtypeStruct((B, NSA_WIDTH, S), BF16),
        jax.ShapeDtypeStruct((B, G * V_ROWS, S), BF16),
        jax.ShapeDtypeStruct((B, G * V_ROWS, S), BF16),
        jax.ShapeDtypeStruct((B, G * GL_ROWS, S), F32),
    )
    return pl.pallas_call(
        _proj_kernel,
        grid=(B, S // tm),
        in_specs=[row(D_MODEL),
                  pl.BlockSpec((D_MODEL, nn), lambda b, i: (0, 0)),
                  pl.BlockSpec((nt, D_MODEL), lambda b, i: (0, 0))],
        out_specs=(row(4 * RET_WIDTH), row(NSA_KV_WIDTH), row(NSA_KV_WIDTH), kspec,
                   row(NSA_KV_WIDTH), row(NSA_WIDTH),
                   row(2 * LRU_WIDTH), col(NSA_WIDTH), col(G * V_ROWS), col(G * V_ROWS),
                   col(G * GL_ROWS)),
        out_shape=out_shape,
        compiler_params=_cparams(("parallel", "parallel")),
        name="in_proj",
    )(x, wn, wt)


def _prep_in_weights(w_in):
    seg = [w_in[:, IN_OFFS[i]:IN_OFFS[i + 1]] for i in range(len(IN_SIZES))]
    (rq, rk, rv, rg, nq, nkc, nvc, nks, nvs, nkw, nvw, ng, ngl, lx, lg) = seg
    wn = jnp.concatenate([rq, rk, rv, rg, nkc, nvc, nks, nkw, ng, lx, lg], axis=1).astype(BF16)
    glt = ngl.T.reshape(NSA_KV_HEADS, NSA_GROUP * N_BRANCH, D_MODEL)
    glt = jnp.pad(glt, ((0, 0), (0, GL_ROWS - NSA_GROUP * N_BRANCH), (0, 0)))
    wt = jnp.concatenate([nq.T * (HEAD_DIM ** -0.5), nvs.T, nvw.T,
                          glt.reshape(NSA_KV_HEADS * GL_ROWS, D_MODEL)], axis=0).astype(BF16)
    return wn, wt


def _ret_kernel(ret_ref, cos_ref, sa_ref, sb_ref, qd_ref, kd_ref, dmat_ref, out_ref, state_ref,
                *, chunk_decay):
    c = pl.program_id(1)

    @pl.when(c == 0)
    def _():
        state_ref[...] = jnp.zeros_like(state_ref)

    W = RET_WIDTH
    q = ret_ref[:, 0:W]
    k = ret_ref[:, W:2 * W]
    v = ret_ref[:, 2 * W:3 * W]
    gate = ret_ref[:, 3 * W:4 * W]
    cos, sa, sb = cos_ref[...], sa_ref[...], sb_ref[...]
    half = HEAD_DIM // 2

    def rope(t):
        cols = []
        for c0 in range(0, W, LANES):
            tc = t[:, c0:c0 + LANES]
            cols.append(tc * cos[:, c0:c0 + LANES]
                        + pltpu.roll(tc, LANES - half, 1) * sa[:, c0:c0 + LANES]
                        + pltpu.roll(tc, half, 1) * sb[:, c0:c0 + LANES])
        return jnp.concatenate(cols, axis=1)

    qr = rope(q)
    kr = rope(k) * (HEAD_DIM ** -0.5)
    qd = qr * qd_ref[...]
    kd = kr * kd_ref[...]
    ys = []
    for h in range(RET_HEADS):
        sl = slice(h * HEAD_DIM, (h + 1) * HEAD_DIM)
        qb, kb, vb = qr[:, sl].astype(BF16), kr[:, sl].astype(BF16), v[:, sl].astype(BF16)
        s = lax.dot_general(qb, kb, _CONTRACT_LAST, preferred_element_type=F32) * dmat_ref[h]
        inner = jnp.dot(s.astype(BF16), vb, preferred_element_type=F32)
        st = state_ref[h]
        cross = jnp.dot(qd[:, sl].astype(BF16), st.astype(BF16), preferred_element_type=F32)
        y = inner + cross
        mu = jnp.mean(y, axis=-1, keepdims=True)
        yc = y - mu
        var = jnp.mean(yc * yc, axis=-1, keepdims=True)
        ys.append(yc * lax.rsqrt(var + LN_EPS))
        kv = lax.dot_general(kd[:, sl].astype(BF16), vb, _CONTRACT_FIRST,
                             preferred_element_type=F32)
        state_ref[h] = st * chunk_decay[h] + kv
    y = jnp.concatenate(ys, axis=1)
    out_ref[...] = (y * (gate * jax.nn.sigmoid(gate))).astype(out_ref.dtype)


def _ret_tables(S):
    C, H, d = RET_CHUNK, RET_HEADS, HEAD_DIM
    half = d // 2
    inv = 1.0 / (ROPE_BASE ** (np.arange(half, dtype=np.float64) / half))
    ang = np.arange(S, dtype=np.float64)[:, None] * inv[None, :]
    cos_h = np.concatenate([np.cos(ang), np.cos(ang)], axis=1)
    sin_lo = np.concatenate([-np.sin(ang), np.zeros_like(ang)], axis=1)
    sin_hi = np.concatenate([np.zeros_like(ang), np.sin(ang)], axis=1)
    tile = lambda a: np.tile(a, (1, H)).astype(np.float32)
    log_g = np.log(1.0 - 2.0 ** (-5.0 - np.arange(H, dtype=np.float64)))
    idx = np.arange(C, dtype=np.float64)
    diff = idx[:, None] - idx[None, :]
    dmat = np.where(diff >= 0, np.exp(np.maximum(diff, 0.0)[None] * log_g[:, None, None]), 0.0)
    kd = np.exp((C - 1.0 - idx)[:, None] * log_g[None, :])
    qd = np.exp((idx + 1.0)[:, None] * log_g[None, :])
    rep = lambda a: np.repeat(a, d, axis=1).astype(np.float32)
    chunk_decay = tuple(float(v) for v in np.exp(C * log_g))
    return (tile(cos_h), tile(sin_lo), tile(sin_hi), rep(qd), rep(kd),
            dmat.astype(np.float32), chunk_decay)


def _retention(ret):
    B, S, _ = ret.shape
    C, W = RET_CHUNK, RET_WIDTH
    cos, sa, sb, qd, kd, dmat, chunk_decay = _ret_tables(S)
    pos = pl.BlockSpec((C, W), lambda b, c: (c, 0))
    fixed = pl.BlockSpec((C, W), lambda b, c: (0, 0))
    return pl.pallas_call(
        functools.partial(_ret_kernel, chunk_decay=chunk_decay),
        grid=(B, S // C),
        in_specs=[pl.BlockSpec((None, C, 4 * W), lambda b, c: (b, c, 0)), pos, pos, pos,
                  fixed, fixed, pl.BlockSpec((RET_HEADS, C, C), lambda b, c: (0, 0, 0))],
        out_specs=pl.BlockSpec((None, C, W), lambda b, c: (b, c, 0)),
        out_shape=jax.ShapeDtypeStruct((B, S, W), BF16),
        scratch_shapes=[pltpu.VMEM((RET_HEADS, HEAD_DIM, HEAD_DIM), F32)],
        compiler_params=_cparams(("parallel", "arbitrary")),
        name="retention",
    )(ret, jnp.asarray(cos), jnp.asarray(sa), jnp.asarray(sb), jnp.asarray(qd), jnp.asarray(kd),
      jnp.asarray(dmat))


_CMP_ROWS = 256


def _cmp_kernel(kcn_ref, vcn_ref, pos_ref, wl_ref, kc_ref, vct_ref):
    n, W = _CMP_ROWS, NSA_KV_WIDTH
    res = []
    for i, ref in enumerate((kcn_ref, vcn_ref)):
        out = jnp.zeros((n, 2 * W), F32)
        cst = jnp.zeros((SUBLANES, 2 * W), F32)
        for l in range(CMP_STRIDE):
            x = ref[pl.ds(l, n, stride=CMP_STRIDE), :].astype(BF16)
            out = out + jnp.dot(x, wl_ref[i, l], preferred_element_type=F32)
            cst = cst + jnp.dot(pos_ref[i, l].astype(BF16), wl_ref[i, l],
                                preferred_element_type=F32)
        const = cst[0:1, 0:W] + cst[1:2, W:2 * W]
        res.append(out[:, 0:W] + pltpu.roll(out[:, W:2 * W], n - 1, 0) + const)
    kc_ref[...] = res[0].astype(BF16)
    vct_ref[...] = res[1].T.astype(BF16)


def _prep_cmp_weights(cmp_pos, cmp_w):
    L2, d, G = CMP_STRIDE, HEAD_DIM, NSA_KV_HEADS
    assert G == 2
    w5 = cmp_w.reshape(2, 2, L2, d, d)
    z = jnp.zeros((2, L2, d, d), F32)

    def both_groups(w):
        return jnp.concatenate([jnp.concatenate([w, z], axis=-1),
                                jnp.concatenate([z, w], axis=-1)], axis=-2)

    wl = jnp.concatenate([both_groups(w5[:, 0]), both_groups(w5[:, 1])], axis=-1).astype(BF16)
    p4 = cmp_pos.reshape(2, 2, L2, d)
    prow = jnp.concatenate([p4, p4], axis=-1).transpose(0, 2, 1, 3)
    prow = jnp.pad(prow, ((0, 0), (0, 0), (0, SUBLANES - 2), (0, 0)))
    return prow, wl


def _compress(kcn, vcn, prow, wl):
    B, S, W = kcn.shape
    n = _CMP_ROWS
    return pl.pallas_call(
        _cmp_kernel,
        grid=(B,),
        in_specs=[pl.BlockSpec((None, S, W), lambda b: (b, 0, 0)),
                  pl.BlockSpec((None, S, W), lambda b: (b, 0, 0)),
                  pl.BlockSpec(prow.shape, lambda b: (0, 0, 0, 0)),
                  pl.BlockSpec(wl.shape, lambda b: (0, 0, 0, 0))],
        out_specs=(pl.BlockSpec((None, n, NSA_KV_WIDTH), lambda b: (b, 0, 0)),
                   pl.BlockSpec((None, NSA_KV_WIDTH, n), lambda b: (b, 0, 0))),
        out_shape=(jax.ShapeDtypeStruct((B, n, NSA_KV_WIDTH), BF16),
                   jax.ShapeDtypeStruct((B, NSA_KV_WIDTH, n), BF16)),
        compiler_params=_cparams(("parallel",)),
        name="nsa_compress",
    )(kcn, vcn, prow, wl)


def _nsa_kernel(qt_ref, kc_ref, vct_ref, ks_ref, vst_ref, kw_ref, vwt_ref, glt_ref, ng_ref,
                selmap_ref, band_ref, cmpmask_ref, out_ref, val_ref, cnt_ref, s_ref, p_ref, acc_ref,
                st_ref):
    TQ, KB, R, d = NSA_TQ, NSA_KB, NSA_GROUP, HEAD_DIM
    n_sel = selmap_ref.shape[0]
    g = pl.program_id(1)
    qi = pl.program_id(2)
    t0 = qi * TQ
    heads = [slice(r * TQ, (r + 1) * TQ) for r in range(R)]
    qs = [qt_ref[r * d:(r + 1) * d, :] for r in range(R)]
    zq = jnp.zeros_like(qs[0])
    qs_nat = [jnp.concatenate([jnp.where(g == gg, q, zq) for gg in range(NSA_KV_HEADS)], axis=0)
              for q in qs]

    def key_off(kb):
        return pl.multiple_of(kb * KB, KB)

    n_wb = WINDOW // KB + 1
    w_kb = [qi - (n_wb - 1) + w for w in range(n_wb)]
    w_off = [key_off(jnp.maximum(kb, 0)) for kb in w_kb]
    w_bias = [band_ref[1]] + [None] * (n_wb - 2) + [band_ref[0]]
    qs_nat_all = jnp.concatenate(qs_nat, axis=1)

    n_cmp = kc_ref.shape[0]
    cmp_off = pl.multiple_of(n_cmp - qi * (TQ // CMP_STRIDE), TQ // CMP_STRIDE)
    cmp_bias = jnp.concatenate([cmpmask_ref[pl.ds(cmp_off, n_cmp), :]] * R, axis=1)
    tq = t0 + lax.broadcasted_iota(jnp.int32, (1, TQ), 1)
    seen = jnp.concatenate([tq >= CMP_LEN - 1] * R, axis=1)
    s = jnp.dot(kc_ref[...], qs_nat_all, preferred_element_type=F32) + cmp_bias

    w_s = []
    for w in range(n_wb):
        s_w = jnp.dot(kw_ref[pl.ds(w_off[w], KB), :], qs_nat_all, preferred_element_type=F32)
        if w_bias[w] is not None:
            s_w = s_w + jnp.concatenate([w_bias[w]] * R, axis=1)
        if w < n_wb - 1:
            s_w = s_w + jnp.where(w_kb[w] >= 0, 0.0, NEG)
        w_s.append(s_w)

    p = jnp.exp2(s - jnp.max(s, axis=0, keepdims=True))
    p = p * jnp.where(seen, 1.0 / jnp.sum(p, axis=0, keepdims=True), 0.0)
    o_c_all = jnp.dot(vct_ref[...], p.astype(BF16), preferred_element_type=F32)
    o_c = [o_c_all[:, hs] for hs in heads]
    psum = functools.reduce(jnp.add, [p[:, hs] for hs in heads])

    p_hi = psum.astype(BF16)
    p_lo = (psum - p_hi.astype(F32)).astype(BF16)
    imp = (jnp.dot(selmap_ref[...], p_hi, preferred_element_type=F32)
           + jnp.dot(selmap_ref[...], p_lo, preferred_element_type=F32))

    jrow = lax.broadcasted_iota(jnp.int32, (n_sel, 1), 0)
    cur = lax.shift_right_logical(tq, SEL_SHIFT)
    forced = (jrow == 0) | (jrow == cur) | (jrow == cur - 1)
    causal = jrow * SEL_LEN <= tq
    val_ref[...] = jnp.where(forced, FORCE_SCORE, jnp.where(causal, imp, -FORCE_SCORE))
    cnt_ref[...] = jnp.zeros_like(cnt_ref)
    n_live = (qi + 1) * (TQ // SEL_LEN)
    n_grp = n_sel // SUBLANES
    for c in range(n_grp):

        @pl.when(c * SUBLANES < n_live)
        def _():
            groups = [slice(k * SUBLANES, (k + 1) * SUBLANES) for k in range(n_grp)]
            vals = [val_ref[rows, :] for rows in groups]
            cnts = [cnt_ref[rows, :] for rows in groups]
            own = lax.broadcasted_iota(jnp.int32, (SUBLANES, 1), 0) + c * SUBLANES
            for jp in range(c * SUBLANES, (c + 1) * SUBLANES):
                cand = jnp.broadcast_to(val_ref[jp:jp + 1, :], (SUBLANES, TQ))
                for k in range(n_grp):
                    if k < c:
                        hit = jnp.where(cand > vals[k], 1.0, 0.0)
                    elif k > c:
                        hit = jnp.where(cand >= vals[k], 1.0, 0.0)
                    else:
                        hit = jnp.where(own > jp, jnp.where(cand >= vals[k], 1.0, 0.0),
                                        jnp.where(cand > vals[k], 1.0, 0.0))
                    cnts[k] = cnts[k] + hit
            for rows, cnt in zip(groups, cnts):
                cnt_ref[rows, :] = cnt

    bias = jnp.where(cnt_ref[...] < float(N_SELECT), 0.0, NEG).astype(BF16)
    qs_sel = [jnp.concatenate([q, bias], axis=0) for q in qs]

    def sel_scores(kb, r):
        return jnp.dot(ks_ref[pl.ds(key_off(kb), KB), :], qs_sel[r], preferred_element_type=F32)

    def sel_values(kb, p):
        return jnp.dot(vst_ref[:, pl.ds(key_off(kb), KB)], p, preferred_element_type=F32)

    ROW_M, ROW_A = 0, 1

    def sel_stage(i, cur_slot, *, diag, mxu_first=False):
        nxt = 1 - cur_slot
        for r, hs in enumerate(heads):
            if mxu_first and r == 0:
                pvs = [sel_values(jnp.maximum(i - 1, 0), p_ref[nxt, :, h2]) for h2 in heads]
                for r2, h2 in enumerate(heads):
                    s_ref[nxt, :, h2] = sel_scores(i + 1, r2)
            s = s_ref[cur_slot, :, hs]
            if diag:
                s = s + band_ref[0]
            m = st_ref[ROW_M:ROW_M + 1, hs]
            m_new = jnp.maximum(m, jnp.max(s, axis=0, keepdims=True))
            p_ref[cur_slot, :, hs] = jnp.exp2(s - m_new).astype(BF16)
            if mxu_first:
                pv_prev = pvs[r]
            else:
                pv_prev = sel_values(jnp.maximum(i - 1, 0), p_ref[nxt, :, hs])
                if not diag:
                    s_ref[nxt, :, hs] = sel_scores(i + 1, r)
            acc_ref[:, hs] = st_ref[ROW_A:ROW_A + 1, hs] * acc_ref[:, hs] + pv_prev
            st_ref[ROW_M:ROW_M + 1, hs] = m_new
            st_ref[ROW_A:ROW_A + 1, hs] = jnp.exp2(m - m_new)

    def sel_finish(i, cur_slot):
        for hs in heads:
            acc_ref[:, hs] = (st_ref[ROW_A:ROW_A + 1, hs] * acc_ref[:, hs]
                              + sel_values(i, p_ref[cur_slot, :, hs]))

    for r, hs in enumerate(heads):
        s_ref[0, :, hs] = sel_scores(0, r)
    p_ref[1] = jnp.zeros(p_ref.shape[1:], BF16)
    acc_ref[...] = jnp.zeros_like(acc_ref)
    st_ref[ROW_M:ROW_M + 1, :] = jnp.full((1, R * TQ), NEG, F32)
    st_ref[ROW_A:ROW_A + 1, :] = jnp.ones((1, R * TQ), F32)

    m_w = functools.reduce(jnp.maximum, [jnp.max(s_w, axis=0, keepdims=True) for s_w in w_s])
    acc_w_all = functools.reduce(jnp.add, [
        jnp.dot(vwt_ref[:, pl.ds(w_off[w], KB)], jnp.exp2(w_s[w] - m_w).astype(BF16),
                preferred_element_type=F32) for w in range(n_wb)])
    acc_w = [acc_w_all[:, hs] for hs in heads]

    def sel_pair(j, c):
        sel_stage(2 * j, 0, diag=False)
        sel_stage(2 * j + 1, 1, diag=False, mxu_first=True)
        return c

    lax.fori_loop(0, lax.shift_right_logical(qi, 1), sel_pair, 0)
    odd = lax.rem(qi, 2) == 1

    @pl.when(odd)
    def _():
        sel_stage(qi - 1, 0, diag=False)
        sel_stage(qi, 1, diag=True)
        sel_finish(qi, 1)

    @pl.when(jnp.logical_not(odd))
    def _():
        sel_stage(qi, 0, diag=True)
        sel_finish(qi, 0)

    gl = jax.nn.sigmoid(glt_ref[...])
    res = []
    for r, hs in enumerate(heads):
        a_s = acc_ref[:, hs]
        a_w = acc_w[r]
        gate = [gl[r * N_BRANCH + br:r * N_BRANCH + br + 1, :] for br in range(N_BRANCH)]
        res.append(gate[0] * o_c[r]
                   + (gate[1] / a_s[d:d + 1, :]) * a_s[0:d, :]
                   + (gate[2] / a_w[d:d + 1, :]) * a_w[0:d, :])
    ng = ng_ref[...]
    out_ref[...] = (jnp.concatenate(res, axis=0).T
                    * (ng * jax.nn.sigmoid(ng))).astype(out_ref.dtype)


def _nsa_tables(S):
    n_cmp = (S - CMP_LEN) // CMP_STRIDE + 1
    n_sel = S // SEL_LEN
    ci = np.arange(n_cmp)[:, None]
    sj = np.arange(n_sel)[None, :]
    overlap = (np.minimum(ci * CMP_STRIDE + CMP_LEN, sj * SEL_LEN + SEL_LEN)
               - np.maximum(ci * CMP_STRIDE, sj * SEL_LEN))
    sel_map = np.clip(overlap, 0, None) / CMP_STRIDE
    selmap_t = np.zeros((n_sel, _CMP_ROWS), np.float32)
    selmap_t[:, :n_cmp] = sel_map.T
    jl = np.arange(NSA_KB)[:, None]
    tl = np.arange(NSA_TQ)[None, :]
    band = np.stack([np.where(jl <= tl, 0.0, NEG), np.where(jl > tl, 0.0, NEG)]).astype(np.float32)
    rel = (np.arange(2 * _CMP_ROWS)[:, None] - _CMP_ROWS) * CMP_STRIDE + (CMP_LEN - 1)
    cmpmask = np.where(rel <= tl, 0.0, NEG).astype(np.float32)
    return selmap_t, band, cmpmask


def _nsa_attention(qt, kc, vct, ks, vst, kw, vwt, glt, ng):
    B, _, S = qt.shape
    G, R, d, TQ = NSA_KV_HEADS, NSA_GROUP, HEAD_DIM, NSA_TQ
    selmap_t, band, cmpmask = _nsa_tables(S)
    n_sel = S // SEL_LEN
    assert NSA_KB == TQ and WINDOW % NSA_KB == 0
    assert n_sel == d
    per_b_rows = lambda n: pl.BlockSpec((None, n, G * d), lambda b, g, q: (b, 0, 0))
    per_bg_cols = lambda rows, n: pl.BlockSpec((None, rows, n), lambda b, g, q: (b, g, 0))
    return pl.pallas_call(
        _nsa_kernel,
        grid=(B, G, S // TQ),
        in_specs=[pl.BlockSpec((None, R * d, TQ), lambda b, g, q: (b, g, q)),
                  per_b_rows(_CMP_ROWS), per_bg_cols(d, _CMP_ROWS),
                  pl.BlockSpec((None, None, S, 2 * d), lambda b, g, q: (b, g, 0, 0)),
                  per_bg_cols(V_ROWS, S), per_b_rows(S), per_bg_cols(V_ROWS, S),
                  pl.BlockSpec((None, GL_ROWS, TQ), lambda b, g, q: (b, g, q)),
                  pl.BlockSpec((None, TQ, R * d), lambda b, g, q: (b, q, g)),
                  pl.BlockSpec(selmap_t.shape, lambda b, g, q: (0, 0)),
                  pl.BlockSpec(band.shape, lambda b, g, q: (0, 0, 0)),
                  pl.BlockSpec(cmpmask.shape, lambda b, g, q: (0, 0))],
        out_specs=pl.BlockSpec((None, TQ, R * d), lambda b, g, q: (b, q, g)),
        out_shape=jax.ShapeDtypeStruct((B, S, NSA_WIDTH), BF16),
        scratch_shapes=[pltpu.VMEM((n_sel, TQ), F32), pltpu.VMEM((n_sel, TQ), F32),
                        pltpu.VMEM((2, NSA_KB, R * TQ), F32),
                        pltpu.VMEM((2, NSA_KB, R * TQ), BF16),
                        pltpu.VMEM((V_ROWS, R * TQ), F32),
                        pltpu.VMEM((SUBLANES, R * TQ), F32)],
        compiler_params=_cparams(("parallel", "parallel", "arbitrary")),
        name="nsa_attention",
    )(qt, kc, vct, ks, vst, kw, vwt, glt, ng, jnp.asarray(selmap_t, dtype=BF16), jnp.asarray(band),
      jnp.asarray(cmpmask))


def _lru_kernel(lru_ref, cw_ref, cb_ref, wa_ref, ba_ref, wx_ref, bx_ref, lam_ref, out_ref,
                xext_ref, h_ref):
    ts, W, pad = LRU_TS, LRU_WIDTH, SUBLANES
    i = pl.program_id(1)

    @pl.when(i == 0)
    def _():
        xext_ref[0:pad, :] = jnp.zeros((pad, W), F32)
        h_ref[...] = jnp.zeros_like(h_ref)

    x = lru_ref[:, 0:W]
    gate = lru_ref[:, W:2 * W]
    xext_ref[pad:pad + ts, :] = x
    first = pad - (CONV_WIDTH - 1)
    xc = cb_ref[...] + cw_ref[0:1, :] * xext_ref[first:first + ts, :]
    for w in range(1, CONV_WIDTH):
        xc = xc + cw_ref[w:w + 1, :] * xext_ref[first + w:first + w + ts, :]
    xext_ref[0:pad, :] = x[ts - pad:ts, :]
    xcb = xc.astype(BF16)
    r = jax.nn.sigmoid(jnp.dot(xcb, wa_ref[...], preferred_element_type=F32) + ba_ref[...])
    gi = jax.nn.sigmoid(jnp.dot(xcb, wx_ref[...], preferred_element_type=F32) + bx_ref[...])
    nl = -lam_ref[...]
    softplus = jnp.maximum(nl, 0.0) + jnp.log1p(jnp.exp(-jnp.abs(nl)))
    log_a = (-LRU_C) * r * softplus
    a = jnp.exp(log_a)
    u = jnp.sqrt(-jnp.tanh(log_a) * (a * a + 1.0)) * (gi * xc)
    row = lax.broadcasted_iota(jnp.int32, (ts, 1), 0)
    step = 1
    while step < ts:
        if step < SUBLANES:
            keep = row >= step
            a_sh = jnp.where(keep, pltpu.roll(a, step, 0), 1.0)
            u_sh = jnp.where(keep, pltpu.roll(u, step, 0), 0.0)
        else:
            a_sh = jnp.concatenate([jnp.ones((step, W), F32), a[:ts - step, :]], axis=0)
            u_sh = jnp.concatenate([jnp.zeros((step, W), F32), u[:ts - step, :]], axis=0)
        u = a * u_sh + u
        a = a * a_sh
        step *= 2
    h = u + a * h_ref[0:1, :]
    h_ref[0:1, :] = h[ts - 1:ts, :]
    out_ref[...] = (h * (gate * jax.nn.sigmoid(gate))).astype(out_ref.dtype)


def _block_diag(w):
    n, d, _ = w.shape
    z = jnp.zeros((d, d), w.dtype)
    return jnp.concatenate(
        [jnp.concatenate([w[i] if j == i else z for j in range(n)], axis=1) for i in range(n)],
        axis=0)


def _rg_lru(lru, conv_w, conv_b, w_a, b_a, w_x, b_x, lam):
    B, S, _ = lru.shape
    ts, W = LRU_TS, LRU_WIDTH
    vec = lambda a: a.reshape(1, W)
    fixed = lambda r: pl.BlockSpec((r, W), lambda b, i: (0, 0))
    return pl.pallas_call(
        _lru_kernel,
        grid=(B, S // ts),
        in_specs=[pl.BlockSpec((None, ts, 2 * W), lambda b, i: (b, i, 0)),
                  fixed(CONV_WIDTH), fixed(1), fixed(W), fixed(1), fixed(W), fixed(1), fixed(1)],
        out_specs=pl.BlockSpec((None, ts, W), lambda b, i: (b, i, 0)),
        out_shape=jax.ShapeDtypeStruct((B, S, W), BF16),
        scratch_shapes=[pltpu.VMEM((ts + SUBLANES, W), F32), pltpu.VMEM((SUBLANES, W), F32)],
        compiler_params=_cparams(("parallel", "arbitrary")),
        name="rg_lru",
    )(lru, conv_w, vec(conv_b), _block_diag(w_a).astype(BF16), vec(b_a),
      _block_diag(w_x).astype(BF16), vec(b_x), vec(lam))


def _out_kernel(x_ref, yr_ref, yn_ref, yl_ref, wo_ref, g_ref, b_ref, out_ref):
    o1, o2 = RET_WIDTH, RET_WIDTH + NSA_WIDTH
    y = jnp.dot(yr_ref[...].astype(BF16), wo_ref[0:o1, :], preferred_element_type=F32)
    y = y + jnp.dot(yn_ref[...].astype(BF16), wo_ref[o1:o2, :], preferred_element_type=F32)
    y = y + jnp.dot(yl_ref[...].astype(BF16), wo_ref[o2:D_MODEL, :], preferred_element_type=F32)
    z = DEEPNORM_ALPHA * x_ref[...] + y
    mu = jnp.mean(z, axis=-1, keepdims=True)
    zc = z - mu
    var = jnp.mean(zc * zc, axis=-1, keepdims=True)
    out_ref[...] = zc * lax.rsqrt(var + LN_EPS) * g_ref[...] + b_ref[...]


def _out_proj(x, y_ret, y_nsa, y_lru, w_out, ln_g, ln_b):
    B, S, _ = x.shape
    tm = OUT_TM
    row = lambda w: pl.BlockSpec((None, tm, w), lambda b, i: (b, i, 0))
    fixed = lambda r: pl.BlockSpec((r, D_MODEL), lambda b, i: (0, 0))
    return pl.pallas_call(
        _out_kernel,
        grid=(B, S // tm),
        in_specs=[row(D_MODEL), row(RET_WIDTH), row(NSA_WIDTH), row(LRU_WIDTH),
                  fixed(D_MODEL), fixed(1), fixed(1)],
        out_specs=row(D_MODEL),
        out_shape=jax.ShapeDtypeStruct((B, S, D_MODEL), F32),
        compiler_params=_cparams(("parallel", "parallel")),
        name="out_proj_ln",
    )(x, y_ret, y_nsa, y_lru, w_out.astype(BF16), ln_g.reshape(1, D_MODEL),
      ln_b.reshape(1, D_MODEL))


def _layer(x, w_in, w_out, ln_g, ln_b, cmp_pos, cmp_w, conv_w, conv_b, w_a, b_a, w_x, b_x, lam):
    wn, wt = _prep_in_weights(w_in)
    ret, kcn, vcn, ks, kw, ng, lru, qt, vst, vwt, glt = _project(x, wn, wt)
    y_ret = _retention(ret)
    prow, wl = _prep_cmp_weights(cmp_pos, cmp_w)
    kc, vct = _compress(kcn, vcn, prow, wl)
    y_nsa = _nsa_attention(qt, kc, vct, ks, vst, kw, vwt, glt, ng)
    y_lru = _rg_lru(lru, conv_w, conv_b, w_a, b_a, w_x, b_x, lam)
    return _out_proj(x, y_ret, y_nsa, y_lru, w_out, ln_g, ln_b)


def kernel(x, w_in, w_out, ln_g, ln_b, nsa_cmp_pos, nsa_cmp_w, lru_conv_w, lru_conv_b,
           lru_w_a, lru_b_a, lru_w_x, lru_b_x, lru_lambda):
    assert x.shape[1] % max(PROJ_TM, RET_CHUNK, LRU_TS, OUT_TM, NSA_TQ) == 0
    assert x.shape[1] // CMP_STRIDE == _CMP_ROWS
    for l in range(DEPTH):
        x = _layer(x, w_in[l], w_out[l], ln_g[l], ln_b[l], nsa_cmp_pos[l], nsa_cmp_w[l],
                   lru_conv_w[l], lru_conv_b[l], lru_w_a[l], lru_b_a[l], lru_w_x[l], lru_b_x[l],
                   lru_lambda[l])
    return x
```

```python
import functools
import math

import jax
import jax.numpy as jnp
import numpy as np
from jax import lax
from jax.experimental import pallas as pl
from jax.experimental.pallas import tpu as pltpu

F32 = jnp.float32
BF16 = jnp.bfloat16

D_MODEL = 1024
DEPTH = 2
HEAD_DIM = 64
RET_HEADS = 4
RET_WIDTH = RET_HEADS * HEAD_DIM
ROPE_BASE = 10000.0
NSA_HEADS = 8
NSA_KV_HEADS = 2
NSA_GROUP = NSA_HEADS // NSA_KV_HEADS
NSA_WIDTH = NSA_HEADS * HEAD_DIM
NSA_KV_WIDTH = NSA_KV_HEADS * HEAD_DIM
CMP_LEN = 32
CMP_STRIDE = 16
SEL_LEN = 64
N_SELECT = 16
WINDOW = 512
N_BRANCH = 3
FORCE_SCORE = 1.0e4
LRU_WIDTH = 256
LRU_BLOCKS = 4
LRU_BLOCK_DIM = LRU_WIDTH // LRU_BLOCKS
CONV_WIDTH = 4
LRU_C = 8.0
DEEPNORM_ALPHA = (2.0 * DEPTH) ** 0.25
LN_EPS = 1e-5
NEG = -1e30
LOG2E = math.log2(math.e)
SEL_SHIFT = int(math.log2(SEL_LEN))

IN_SIZES = (RET_WIDTH,) * 4 + (NSA_WIDTH,) + (NSA_KV_WIDTH,) * 6 + (
    NSA_WIDTH, NSA_HEADS * N_BRANCH, LRU_WIDTH, LRU_WIDTH)
IN_OFFS = tuple(int(v) for v in np.cumsum((0,) + IN_SIZES))

V7X_VMEM_LIMIT = 48 * 1024 * 1024
SUBLANES = 8
LANES = 128
BF16_SUBLANES = 16
PROJ_TM = 512
RET_CHUNK = 256
LRU_TS = 256
OUT_TM = 512
NSA_TQ = 256
NSA_KB = 256
GL_ROWS = 16
V_ROWS = HEAD_DIM + BF16_SUBLANES

_CONTRACT_LAST = (((1,), (1,)), ((), ()))
_CONTRACT_FIRST = (((0,), (0,)), ((), ()))


def _cparams(sem):
    return pltpu.CompilerParams(dimension_semantics=sem, vmem_limit_bytes=V7X_VMEM_LIMIT)


_WN_COLS = (4 * RET_WIDTH, 2 * NSA_KV_WIDTH, 2 * NSA_KV_WIDTH, NSA_WIDTH, 2 * LRU_WIDTH)
_WN_OFFS = tuple(int(v) for v in np.cumsum((0,) + _WN_COLS))
_WT_ROWS = (NSA_WIDTH, NSA_KV_WIDTH, NSA_KV_WIDTH, NSA_KV_HEADS * GL_ROWS)
_WT_OFFS = tuple(int(v) for v in np.cumsum((0,) + _WT_ROWS))


def _proj_kernel(x_ref, wn_ref, wt_ref,
                 cos_ref, sa_ref, sb_ref, qd_ref, kd_ref, dmat_ref,
                 cw_ref, cb_ref, wa_ref, ba_ref, wx_ref, bx_ref, lam_ref,
                 yret_ref, ylru_ref, kcn_ref, vcn_ref, ks_ref, kw_ref, ng_ref,
                 qt_ref, vst_ref, vwt_ref, glt_ref,
                 state_ref, xext_ref, h_ref, *, chunk_decay):
    tm = x_ref.shape[0]

    @pl.when(pl.program_id(1) == 0)
    def _():
        state_ref[...] = jnp.zeros_like(state_ref)
        xext_ref[0:SUBLANES, :] = jnp.zeros((SUBLANES, LRU_WIDTH), F32)
        h_ref[...] = jnp.zeros_like(h_ref)

    xb = x_ref[...].astype(BF16)

    def nat(i):
        return jnp.dot(xb, wn_ref[:, _WN_OFFS[i]:_WN_OFFS[i + 1]], preferred_element_type=F32)

    def tr(i):
        return lax.dot_general(wt_ref[_WT_OFFS[i]:_WT_OFFS[i + 1], :], xb, _CONTRACT_LAST,
                               preferred_element_type=F32)

    ret = nat(0)
    lru = nat(4)
    for c0 in range(0, tm, RET_CHUNK):
        rows = slice(c0, c0 + RET_CHUNK)
        yret_ref[rows, :] = _retention_chunk(
            ret[rows, :], cos_ref[rows, :], sa_ref[rows, :], sb_ref[rows, :], qd_ref[...],
            kd_ref[...], dmat_ref, state_ref, chunk_decay).astype(yret_ref.dtype)
    for c0 in range(0, tm, LRU_TS):
        rows = slice(c0, c0 + LRU_TS)
        ylru_ref[rows, :] = _lru_tile(lru[rows, :], cw_ref, cb_ref, wa_ref, ba_ref, wx_ref, bx_ref,
                                      lam_ref, xext_ref, h_ref).astype(ylru_ref.dtype)

    kvc = nat(1)
    kcn_ref[...] = kvc[:, 0:NSA_KV_WIDTH]
    vcn_ref[...] = kvc[:, NSA_KV_WIDTH:2 * NSA_KV_WIDTH]
    ksw = nat(2)
    ks = ksw[:, 0:NSA_KV_WIDTH]
    key = pl.program_id(1) * tm + lax.broadcasted_iota(jnp.int32, (tm, 1), 0)
    blk = lax.broadcasted_iota(jnp.int32, (1, HEAD_DIM), 1)
    onehot = jnp.where(lax.shift_right_logical(key, SEL_SHIFT) == blk, 1.0, 0.0)
    for g in range(NSA_KV_HEADS):
        ks_ref[g] = jnp.concatenate([ks[:, g * HEAD_DIM:(g + 1) * HEAD_DIM], onehot],
                                    axis=1).astype(BF16)
    kw_ref[...] = ksw[:, NSA_KV_WIDTH:2 * NSA_KV_WIDTH].astype(BF16)
    ng_ref[...] = nat(3)
    qt_ref[...] = (tr(0) * LOG2E).astype(BF16)
    ones_rows = jnp.where(lax.broadcasted_iota(jnp.int32, (BF16_SUBLANES, tm), 0) == 0,
                          1.0, 0.0).astype(BF16)
    for ref, i in ((vst_ref, 1), (vwt_ref, 2)):
        vt = tr(i)
        for g in range(NSA_KV_HEADS):
            ref[g * V_ROWS:g * V_ROWS + HEAD_DIM, :] = (
                vt[g * HEAD_DIM:(g + 1) * HEAD_DIM, :].astype(BF16))
            ref[g * V_ROWS + HEAD_DIM:(g + 1) * V_ROWS, :] = ones_rows
    glt_ref[...] = tr(3)


def _project_and_mix(x, wn, wt, conv_w, conv_b, w_a, b_a, w_x, b_x, lam):
    B, S, _ = x.shape
    tm = PROJ_TM
    nn, nt = wn.shape[1], wt.shape[0]
    G, C, W, LW = NSA_KV_HEADS, RET_CHUNK, RET_WIDTH, LRU_WIDTH
    cos, sa, sb, qd, kd, dmat, chunk_decay = _ret_tables(S)
    row = lambda w: pl.BlockSpec((None, tm, w), lambda b, i: (b, i, 0))
    col = lambda r: pl.BlockSpec((None, r, tm), lambda b, i: (b, 0, i))
    kspec = pl.BlockSpec((None, G, tm, 2 * HEAD_DIM), lambda b, i: (b, 0, i, 0))
    fixed = lambda *shape: pl.BlockSpec(shape, lambda b, i: (0,) * len(shape))
    pos = pl.BlockSpec((tm, W), lambda b, i: (i, 0))
    vec = lambda a: a.reshape(1, LW)
    out_shape = (
        jax.ShapeDtypeStruct((B, S, W), BF16),
        jax.ShapeDtypeStruct((B, S, LW), BF16),
        jax.ShapeDtypeStruct((B, S, NSA_KV_WIDTH), F32),
        jax.ShapeDtypeStruct((B, S, NSA_KV_WIDTH), F32),
        jax.ShapeDtypeStruct((B, G, S, 2 * HEAD_DIM), BF16),
        jax.ShapeDtypeStruct((B, S, NSA_KV_WIDTH), BF16),
        jax.ShapeDtypeStruct((B, S, NSA_WIDTH), F32),
        jax.ShapeDtypeStruct((B, NSA_WIDTH, S), BF16),
        jax.ShapeDtypeStruct((B, G * V_ROWS, S), BF16),
        jax.ShapeDtypeStruct((B, G * V_ROWS, S), BF16),
        jax.ShapeDtypeStruct((B, G * GL_ROWS, S), F32),
    )
    return pl.pallas_call(
        functools.partial(_proj_kernel, chunk_decay=chunk_decay),
        grid=(B, S // tm),
        in_specs=[row(D_MODEL), fixed(D_MODEL, nn), fixed(nt, D_MODEL),
                  pos, pos, pos, fixed(C, W), fixed(C, W), fixed(RET_HEADS, C, C),
                  fixed(CONV_WIDTH, LW), fixed(1, LW), fixed(LW, LW), fixed(1, LW), fixed(LW, LW),
                  fixed(1, LW), fixed(1, LW)],
        out_specs=(row(W), row(LW), row(NSA_KV_WIDTH), row(NSA_KV_WIDTH), kspec,
                   row(NSA_KV_WIDTH), row(NSA_WIDTH), col(NSA_WIDTH), col(G * V_ROWS),
                   col(G * V_ROWS), col(G * GL_ROWS)),
        out_shape=out_shape,
        scratch_shapes=[pltpu.VMEM((RET_HEADS, HEAD_DIM, HEAD_DIM), F32),
                        pltpu.VMEM((LRU_TS + SUBLANES, LW), F32), pltpu.VMEM((SUBLANES, LW), F32)],
        compiler_params=_cparams(("parallel", "arbitrary")),
        name="in_proj_mix",
    )(x, wn, wt, jnp.asarray(cos), jnp.asarray(sa), jnp.asarray(sb), jnp.asarray(qd),
      jnp.asarray(kd), jnp.asarray(dmat), conv_w, vec(conv_b), _block_diag(w_a).astype(BF16),
      vec(b_a), _block_diag(w_x).astype(BF16), vec(b_x), vec(lam))


def _prep_in_weights(w_in):
    seg = [w_in[:, IN_OFFS[i]:IN_OFFS[i + 1]] for i in range(len(IN_SIZES))]
    (rq, rk, rv, rg, nq, nkc, nvc, nks, nvs, nkw, nvw, ng, ngl, lx, lg) = seg
    wn = jnp.concatenate([rq, rk, rv, rg, nkc, nvc, nks, nkw, ng, lx, lg], axis=1).astype(BF16)
    glt = ngl.T.reshape(NSA_KV_HEADS, NSA_GROUP * N_BRANCH, D_MODEL)
    glt = jnp.pad(glt, ((0, 0), (0, GL_ROWS - NSA_GROUP * N_BRANCH), (0, 0)))
    wt = jnp.concatenate([nq.T * (HEAD_DIM ** -0.5), nvs.T, nvw.T,
                          glt.reshape(NSA_KV_HEADS * GL_ROWS, D_MODEL)], axis=0).astype(BF16)
    return wn, wt


def _retention_chunk(ret, cos, sa, sb, qdec, kdec, dmat_ref, state_ref, chunk_decay):
    W = RET_WIDTH
    q = ret[:, 0:W]
    k = ret[:, W:2 * W]
    v = ret[:, 2 * W:3 * W]
    gate = ret[:, 3 * W:4 * W]
    half = HEAD_DIM // 2

    def rope(t):
        cols = []
        for c0 in range(0, W, LANES):
            tc = t[:, c0:c0 + LANES]
            cols.append(tc * cos[:, c0:c0 + LANES]
                        + pltpu.roll(tc, LANES - half, 1) * sa[:, c0:c0 + LANES]
                        + pltpu.roll(tc, half, 1) * sb[:, c0:c0 + LANES])
        return jnp.concatenate(cols, axis=1)

    qr = rope(q)
    kr = rope(k) * (HEAD_DIM ** -0.5)
    qd = qr * qdec
    kd = kr * kdec
    ys = []
    for h in range(RET_HEADS):
        sl = slice(h * HEAD_DIM, (h + 1) * HEAD_DIM)
        qb, kb, vb = qr[:, sl].astype(BF16), kr[:, sl].astype(BF16), v[:, sl].astype(BF16)
        s = lax.dot_general(qb, kb, _CONTRACT_LAST, preferred_element_type=F32) * dmat_ref[h]
        inner = jnp.dot(s.astype(BF16), vb, preferred_element_type=F32)
        st = state_ref[h]
        cross = jnp.dot(qd[:, sl].astype(BF16), st.astype(BF16), preferred_element_type=F32)
        y = inner + cross
        mu = jnp.mean(y, axis=-1, keepdims=True)
        yc = y - mu
        var = jnp.mean(yc * yc, axis=-1, keepdims=True)
        ys.append(yc * lax.rsqrt(var + LN_EPS))
        kv = lax.dot_general(kd[:, sl].astype(BF16), vb, _CONTRACT_FIRST,
                             preferred_element_type=F32)
        state_ref[h] = st * chunk_decay[h] + kv
    y = jnp.concatenate(ys, axis=1)
    return y * (gate * jax.nn.sigmoid(gate))


def _ret_tables(S):
    C, H, d = RET_CHUNK, RET_HEADS, HEAD_DIM
    half = d // 2
    inv = 1.0 / (ROPE_BASE ** (np.arange(half, dtype=np.float64) / half))
    ang = np.arange(S, dtype=np.float64)[:, None] * inv[None, :]
    cos_h = np.concatenate([np.cos(ang), np.cos(ang)], axis=1)
    sin_lo = np.concatenate([-np.sin(ang), np.zeros_like(ang)], axis=1)
    sin_hi = np.concatenate([np.zeros_like(ang), np.sin(ang)], axis=1)
    tile = lambda a: np.tile(a, (1, H)).astype(np.float32)
    log_g = np.log(1.0 - 2.0 ** (-5.0 - np.arange(H, dtype=np.float64)))
    idx = np.arange(C, dtype=np.float64)
    diff = idx[:, None] - idx[None, :]
    dmat = np.where(diff >= 0, np.exp(np.maximum(diff, 0.0)[None] * log_g[:, None, None]), 0.0)
    kd = np.exp((C - 1.0 - idx)[:, None] * log_g[None, :])
    qd = np.exp((idx + 1.0)[:, None] * log_g[None, :])
    rep = lambda a: np.repeat(a, d, axis=1).astype(np.float32)
    chunk_decay = tuple(float(v) for v in np.exp(C * log_g))
    return (tile(cos_h), tile(sin_lo), tile(sin_hi), rep(qd), rep(kd),
            dmat.astype(np.float32), chunk_decay)


_CMP_ROWS = 256


def _cmp_kernel(kcn_ref, vcn_ref, pos_ref, wl_ref, kc_ref, vct_ref):
    n, W = _CMP_ROWS, NSA_KV_WIDTH
    res = []
    for i, ref in enumerate((kcn_ref, vcn_ref)):
        out = jnp.zeros((n, 2 * W), F32)
        cst = jnp.zeros((SUBLANES, 2 * W), F32)
        for l in range(CMP_STRIDE):
            x = ref[pl.ds(l, n, stride=CMP_STRIDE), :].astype(BF16)
            out = out + jnp.dot(x, wl_ref[i, l], preferred_element_type=F32)
            cst = cst + jnp.dot(pos_ref[i, l].astype(BF16), wl_ref[i, l],
                                preferred_element_type=F32)
        const = cst[0:1, 0:W] + cst[1:2, W:2 * W]
        res.append(out[:, 0:W] + pltpu.roll(out[:, W:2 * W], n - 1, 0) + const)
    kc_ref[...] = res[0].astype(BF16)
    vct_ref[...] = res[1].T.astype(BF16)


def _prep_cmp_weights(cmp_pos, cmp_w):
    L2, d, G = CMP_STRIDE, HEAD_DIM, NSA_KV_HEADS
    assert G == 2
    w5 = cmp_w.reshape(2, 2, L2, d, d)
    z = jnp.zeros((2, L2, d, d), F32)

    def both_groups(w):
        return jnp.concatenate([jnp.concatenate([w, z], axis=-1),
                                jnp.concatenate([z, w], axis=-1)], axis=-2)

    wl = jnp.concatenate([both_groups(w5[:, 0]), both_groups(w5[:, 1])], axis=-1).astype(BF16)
    p4 = cmp_pos.reshape(2, 2, L2, d)
    prow = jnp.concatenate([p4, p4], axis=-1).transpose(0, 2, 1, 3)
    prow = jnp.pad(prow, ((0, 0), (0, 0), (0, SUBLANES - 2), (0, 0)))
    return prow, wl


def _compress(kcn, vcn, prow, wl):
    B, S, W = kcn.shape
    n = _CMP_ROWS
    return pl.pallas_call(
        _cmp_kernel,
        grid=(B,),
        in_specs=[pl.BlockSpec((None, S, W), lambda b: (b, 0, 0)),
                  pl.BlockSpec((None, S, W), lambda b: (b, 0, 0)),
                  pl.BlockSpec(prow.shape, lambda b: (0, 0, 0, 0)),
                  pl.BlockSpec(wl.shape, lambda b: (0, 0, 0, 0))],
        out_specs=(pl.BlockSpec((None, n, NSA_KV_WIDTH), lambda b: (b, 0, 0)),
                   pl.BlockSpec((None, NSA_KV_WIDTH, n), lambda b: (b, 0, 0))),
        out_shape=(jax.ShapeDtypeStruct((B, n, NSA_KV_WIDTH), BF16),
                   jax.ShapeDtypeStruct((B, NSA_KV_WIDTH, n), BF16)),
        compiler_params=_cparams(("parallel",)),
        name="nsa_compress",
    )(kcn, vcn, prow, wl)


def _nsa_kernel(qt_ref, kc_ref, vct_ref, ks_ref, vst_ref, kw_ref, vwt_ref, glt_ref, ng_ref,
                selmap_ref, band_ref, cmpmask_ref, out_ref, val_ref, cnt_ref, s_ref, p_ref, acc_ref,
                st_ref):
    TQ, KB, R, d = NSA_TQ, NSA_KB, NSA_GROUP, HEAD_DIM
    n_sel = selmap_ref.shape[0]
    g = pl.program_id(1)
    qi = pl.program_id(2)
    t0 = qi * TQ
    heads = [slice(r * TQ, (r + 1) * TQ) for r in range(R)]
    qs = [qt_ref[r * d:(r + 1) * d, :] for r in range(R)]
    zq = jnp.zeros_like(qs[0])
    qs_nat = [jnp.concatenate([jnp.where(g == gg, q, zq) for gg in range(NSA_KV_HEADS)], axis=0)
              for q in qs]

    def key_off(kb):
        return pl.multiple_of(kb * KB, KB)

    n_wb = WINDOW // KB + 1
    w_kb = [qi - (n_wb - 1) + w for w in range(n_wb)]
    w_off = [key_off(jnp.maximum(kb, 0)) for kb in w_kb]
    w_bias = [band_ref[1]] + [None] * (n_wb - 2) + [band_ref[0]]
    qs_nat_all = jnp.concatenate(qs_nat, axis=1)

    n_cmp = kc_ref.shape[0]
    cmp_off = pl.multiple_of(n_cmp - qi * (TQ // CMP_STRIDE), TQ // CMP_STRIDE)
    cmp_bias = jnp.concatenate([cmpmask_ref[pl.ds(cmp_off, n_cmp), :]] * R, axis=1)
    tq = t0 + lax.broadcasted_iota(jnp.int32, (1, TQ), 1)
    seen = jnp.concatenate([tq >= CMP_LEN - 1] * R, axis=1)
    s = jnp.dot(kc_ref[...], qs_nat_all, preferred_element_type=F32) + cmp_bias

    w_s = []
    for w in range(n_wb):
        s_w = jnp.dot(kw_ref[pl.ds(w_off[w], KB), :], qs_nat_all, preferred_element_type=F32)
        if w_bias[w] is not None:
            s_w = s_w + jnp.concatenate([w_bias[w]] * R, axis=1)
        if w < n_wb - 1:
            s_w = s_w + jnp.where(w_kb[w] >= 0, 0.0, NEG)
        w_s.append(s_w)

    p = jnp.exp2(s - jnp.max(s, axis=0, keepdims=True))
    p = p * jnp.where(seen, 1.0 / jnp.sum(p, axis=0, keepdims=True), 0.0)
    o_c_all = jnp.dot(vct_ref[...], p.astype(BF16), preferred_element_type=F32)
    o_c = [o_c_all[:, hs] for hs in heads]
    psum = functools.reduce(jnp.add, [p[:, hs] for hs in heads])

    p_hi = psum.astype(BF16)
    p_lo = (psum - p_hi.astype(F32)).astype(BF16)
    imp = (jnp.dot(selmap_ref[...], p_hi, preferred_element_type=F32)
           + jnp.dot(selmap_ref[...], p_lo, preferred_element_type=F32))

    jrow = lax.broadcasted_iota(jnp.int32, (n_sel, 1), 0)
    cur = lax.shift_right_logical(tq, SEL_SHIFT)
    forced = (jrow == 0) | (jrow == cur) | (jrow == cur - 1)
    causal = jrow * SEL_LEN <= tq
    val_ref[...] = jnp.where(forced, FORCE_SCORE, jnp.where(causal, imp, -FORCE_SCORE))
    cnt_ref[...] = jnp.zeros_like(cnt_ref)
    n_live = (qi + 1) * (TQ // SEL_LEN)
    n_grp = n_sel // SUBLANES
    for c in range(n_grp):

        @pl.when(c * SUBLANES < n_live)
        def _():
            groups = [slice(k * SUBLANES, (k + 1) * SUBLANES) for k in range(n_grp)]
            vals = [val_ref[rows, :] for rows in groups]
            cnts = [cnt_ref[rows, :] for rows in groups]
            own = lax.broadcasted_iota(jnp.int32, (SUBLANES, 1), 0) + c * SUBLANES
            for jp in range(c * SUBLANES, (c + 1) * SUBLANES):
                cand = jnp.broadcast_to(val_ref[jp:jp + 1, :], (SUBLANES, TQ))
                for k in range(n_grp):
                    if k < c:
                        hit = jnp.where(cand > vals[k], 1.0, 0.0)
                    elif k > c:
                        hit = jnp.where(cand >= vals[k], 1.0, 0.0)
                    else:
                        hit = jnp.where(own > jp, jnp.where(cand >= vals[k], 1.0, 0.0),
                                        jnp.where(cand > vals[k], 1.0, 0.0))
                    cnts[k] = cnts[k] + hit
            for rows, cnt in zip(groups, cnts):
                cnt_ref[rows, :] = cnt

    bias = jnp.where(cnt_ref[...] < float(N_SELECT), 0.0, NEG).astype(BF16)
    qs_sel = [jnp.concatenate([q, bias], axis=0) for q in qs]

    def sel_scores(kb, r):
        return jnp.dot(ks_ref[pl.ds(key_off(kb), KB), :], qs_sel[r], preferred_element_type=F32)

    def sel_values(kb, p):
        return jnp.dot(vst_ref[:, pl.ds(key_off(kb), KB)], p, preferred_element_type=F32)

    ROW_M, ROW_A = 0, 1

    def sel_stage(i, cur_slot, *, diag, mxu_first=False):
        nxt = 1 - cur_slot
        for r, hs in enumerate(heads):
            if mxu_first and r == 0:
                pvs = [sel_values(jnp.maximum(i - 1, 0), p_ref[nxt, :, h2]) for h2 in heads]
                for r2, h2 in enumerate(heads):
                    s_ref[nxt, :, h2] = sel_scores(i + 1, r2)
            s = s_ref[cur_slot, :, hs]
            if diag:
                s = s + band_ref[0]
            m = st_ref[ROW_M:ROW_M + 1, hs]
            m_new = jnp.maximum(m, jnp.max(s, axis=0, keepdims=True))
            p_ref[cur_slot, :, hs] = jnp.exp2(s - m_new).astype(BF16)
            if mxu_first:
                pv_prev = pvs[r]
            else:
                pv_prev = sel_values(jnp.maximum(i - 1, 0), p_ref[nxt, :, hs])
                if not diag:
                    s_ref[nxt, :, hs] = sel_scores(i + 1, r)
            acc_ref[:, hs] = st_ref[ROW_A:ROW_A + 1, hs] * acc_ref[:, hs] + pv_prev
            st_ref[ROW_M:ROW_M + 1, hs] = m_new
            st_ref[ROW_A:ROW_A + 1, hs] = jnp.exp2(m - m_new)

    def sel_finish(i, cur_slot):
        for hs in heads:
            acc_ref[:, hs] = (st_ref[ROW_A:ROW_A + 1, hs] * acc_ref[:, hs]
                              + sel_values(i, p_ref[cur_slot, :, hs]))

    for r, hs in enumerate(heads):
        s_ref[0, :, hs] = sel_scores(0, r)
    p_ref[1] = jnp.zeros(p_ref.shape[1:], BF16)
    acc_ref[...] = jnp.zeros_like(acc_ref)
    st_ref[ROW_M:ROW_M + 1, :] = jnp.full((1, R * TQ), NEG, F32)
    st_ref[ROW_A:ROW_A + 1, :] = jnp.ones((1, R * TQ), F32)

    m_w = functools.reduce(jnp.maximum, [jnp.max(s_w, axis=0, keepdims=True) for s_w in w_s])
    acc_w_all = functools.reduce(jnp.add, [
        jnp.dot(vwt_ref[:, pl.ds(w_off[w], KB)], jnp.exp2(w_s[w] - m_w).astype(BF16),
                preferred_element_type=F32) for w in range(n_wb)])
    acc_w = [acc_w_all[:, hs] for hs in heads]

    def sel_pair(j, c):
        sel_stage(2 * j, 0, diag=False)
        sel_stage(2 * j + 1, 1, diag=False, mxu_first=True)
        return c

    lax.fori_loop(0, lax.shift_right_logical(qi, 1), sel_pair, 0)
    odd = lax.rem(qi, 2) == 1

    @pl.when(odd)
    def _():
        sel_stage(qi - 1, 0, diag=False)
        sel_stage(qi, 1, diag=True)
        sel_finish(qi, 1)

    @pl.when(jnp.logical_not(odd))
    def _():
        sel_stage(qi, 0, diag=True)
        sel_finish(qi, 0)

    gl = jax.nn.sigmoid(glt_ref[...])
    res = []
    for r, hs in enumerate(heads):
        a_s = acc_ref[:, hs]
        a_w = acc_w[r]
        gate = [gl[r * N_BRANCH + br:r * N_BRANCH + br + 1, :] for br in range(N_BRANCH)]
        res.append(gate[0] * o_c[r]
                   + (gate[1] / a_s[d:d + 1, :]) * a_s[0:d, :]
                   + (gate[2] / a_w[d:d + 1, :]) * a_w[0:d, :])
    ng = ng_ref[...]
    out_ref[...] = (jnp.concatenate(res, axis=0).T
                    * (ng * jax.nn.sigmoid(ng))).astype(out_ref.dtype)


def _nsa_tables(S):
    n_cmp = (S - CMP_LEN) // CMP_STRIDE + 1
    n_sel = S // SEL_LEN
    ci = np.arange(n_cmp)[:, None]
    sj = np.arange(n_sel)[None, :]
    overlap = (np.minimum(ci * CMP_STRIDE + CMP_LEN, sj * SEL_LEN + SEL_LEN)
               - np.maximum(ci * CMP_STRIDE, sj * SEL_LEN))
    sel_map = np.clip(overlap, 0, None) / CMP_STRIDE
    selmap_t = np.zeros((n_sel, _CMP_ROWS), np.float32)
    selmap_t[:, :n_cmp] = sel_map.T
    jl = np.arange(NSA_KB)[:, None]
    tl = np.arange(NSA_TQ)[None, :]
    band = np.stack([np.where(jl <= tl, 0.0, NEG), np.where(jl > tl, 0.0, NEG)]).astype(np.float32)
    rel = (np.arange(2 * _CMP_ROWS)[:, None] - _CMP_ROWS) * CMP_STRIDE + (CMP_LEN - 1)
    cmpmask = np.where(rel <= tl, 0.0, NEG).astype(np.float32)
    return selmap_t, band, cmpmask


def _nsa_attention(qt, kc, vct, ks, vst, kw, vwt, glt, ng):
    B, _, S = qt.shape
    G, R, d, TQ = NSA_KV_HEADS, NSA_GROUP, HEAD_DIM, NSA_TQ
    selmap_t, band, cmpmask = _nsa_tables(S)
    n_sel = S // SEL_LEN
    assert NSA_KB == TQ and WINDOW % NSA_KB == 0
    assert n_sel == d
    per_b_rows = lambda n: pl.BlockSpec((None, n, G * d), lambda b, g, q: (b, 0, 0))
    per_bg_cols = lambda rows, n: pl.BlockSpec((None, rows, n), lambda b, g, q: (b, g, 0))
    return pl.pallas_call(
        _nsa_kernel,
        grid=(B, G, S // TQ),
        in_specs=[pl.BlockSpec((None, R * d, TQ), lambda b, g, q: (b, g, q)),
                  per_b_rows(_CMP_ROWS), per_bg_cols(d, _CMP_ROWS),
                  pl.BlockSpec((None, None, S, 2 * d), lambda b, g, q: (b, g, 0, 0)),
                  per_bg_cols(V_ROWS, S), per_b_rows(S), per_bg_cols(V_ROWS, S),
                  pl.BlockSpec((None, GL_ROWS, TQ), lambda b, g, q: (b, g, q)),
                  pl.BlockSpec((None, TQ, R * d), lambda b, g, q: (b, q, g)),
                  pl.BlockSpec(selmap_t.shape, lambda b, g, q: (0, 0)),
                  pl.BlockSpec(band.shape, lambda b, g, q: (0, 0, 0)),
                  pl.BlockSpec(cmpmask.shape, lambda b, g, q: (0, 0))],
        out_specs=pl.BlockSpec((None, TQ, R * d), lambda b, g, q: (b, q, g)),
        out_shape=jax.ShapeDtypeStruct((B, S, NSA_WIDTH), BF16),
        scratch_shapes=[pltpu.VMEM((n_sel, TQ), F32), pltpu.VMEM((n_sel, TQ), F32),
                        pltpu.VMEM((2, NSA_KB, R * TQ), F32),
                        pltpu.VMEM((2, NSA_KB, R * TQ), BF16),
                        pltpu.VMEM((V_ROWS, R * TQ), F32),
                        pltpu.VMEM((SUBLANES, R * TQ), F32)],
        compiler_params=_cparams(("parallel", "parallel", "arbitrary")),
        name="nsa_attention",
    )(qt, kc, vct, ks, vst, kw, vwt, glt, ng, jnp.asarray(selmap_t, dtype=BF16), jnp.asarray(band),
      jnp.asarray(cmpmask))


def _lru_tile(lru, cw_ref, cb_ref, wa_ref, ba_ref, wx_ref, bx_ref, lam_ref, xext_ref, h_ref):
    ts, W, pad = LRU_TS, LRU_WIDTH, SUBLANES
    x = lru[:, 0:W]
    gate = lru[:, W:2 * W]
    xext_ref[pad:pad + ts, :] = x
    first = pad - (CONV_WIDTH - 1)
    xc = cb_ref[...] + cw_ref[0:1, :] * xext_ref[first:first + ts, :]
    for w in range(1, CONV_WIDTH):
        xc = xc + cw_ref[w:w + 1, :] * xext_ref[first + w:first + w + ts, :]
    xext_ref[0:pad, :] = x[ts - pad:ts, :]
    xcb = xc.astype(BF16)
    r = jax.nn.sigmoid(jnp.dot(xcb, wa_ref[...], preferred_element_type=F32) + ba_ref[...])
    gi = jax.nn.sigmoid(jnp.dot(xcb, wx_ref[...], preferred_element_type=F32) + bx_ref[...])
    nl = -lam_ref[...]
    softplus = jnp.maximum(nl, 0.0) + jnp.log1p(jnp.exp(-jnp.abs(nl)))
    log_a = (-LRU_C) * r * softplus
    a = jnp.exp(log_a)
    u = jnp.sqrt(-jnp.tanh(log_a) * (a * a + 1.0)) * (gi * xc)
    row = lax.broadcasted_iota(jnp.int32, (ts, 1), 0)
    step = 1
    while step < ts:
        if step < SUBLANES:
            keep = row >= step
            a_sh = jnp.where(keep, pltpu.roll(a, step, 0), 1.0)
            u_sh = jnp.where(keep, pltpu.roll(u, step, 0), 0.0)
        else:
            a_sh = jnp.concatenate([jnp.ones((step, W), F32), a[:ts - step, :]], axis=0)
            u_sh = jnp.concatenate([jnp.zeros((step, W), F32), u[:ts - step, :]], axis=0)
        u = a * u_sh + u
        a = a * a_sh
        step *= 2
    h = u + a * h_ref[0:1, :]
    h_ref[0:1, :] = h[ts - 1:ts, :]
    return h * (gate * jax.nn.sigmoid(gate))


def _block_diag(w):
    n, d, _ = w.shape
    z = jnp.zeros((d, d), w.dtype)
    return jnp.concatenate(
        [jnp.concatenate([w[i] if j == i else z for j in range(n)], axis=1) for i in range(n)],
        axis=0)


def _out_kernel(x_ref, yr_ref, yn_ref, yl_ref, wo_ref, g_ref, b_ref, out_ref):
    o1, o2 = RET_WIDTH, RET_WIDTH + NSA_WIDTH
    y = jnp.dot(yr_ref[...].astype(BF16), wo_ref[0:o1, :], preferred_element_type=F32)
    y = y + jnp.dot(yn_ref[...].astype(BF16), wo_ref[o1:o2, :], preferred_element_type=F32)
    y = y + jnp.dot(yl_ref[...].astype(BF16), wo_ref[o2:D_MODEL, :], preferred_element_type=F32)
    z = DEEPNORM_ALPHA * x_ref[...] + y
    mu = jnp.mean(z, axis=-1, keepdims=True)
    zc = z - mu
    var = jnp.mean(zc * zc, axis=-1, keepdims=True)
    out_ref[...] = zc * lax.rsqrt(var + LN_EPS) * g_ref[...] + b_ref[...]


def _out_proj(x, y_ret, y_nsa, y_lru, w_out, ln_g, ln_b):
    B, S, _ = x.shape
    tm = OUT_TM
    row = lambda w: pl.BlockSpec((None, tm, w), lambda b, i: (b, i, 0))
    fixed = lambda r: pl.BlockSpec((r, D_MODEL), lambda b, i: (0, 0))
    return pl.pallas_call(
        _out_kernel,
        grid=(B, S // tm),
        in_specs=[row(D_MODEL), row(RET_WIDTH), row(NSA_WIDTH), row(LRU_WIDTH),
                  fixed(D_MODEL), fixed(1), fixed(1)],
        out_specs=row(D_MODEL),
        out_shape=jax.ShapeDtypeStruct((B, S, D_MODEL), F32),
        compiler_params=_cparams(("parallel", "parallel")),
        name="out_proj_ln",
    )(x, y_ret, y_nsa, y_lru, w_out.astype(BF16), ln_g.reshape(1, D_MODEL),
      ln_b.reshape(1, D_MODEL))


def _layer(x, w_in, w_out, ln_g, ln_b, cmp_pos, cmp_w, conv_w, conv_b, w_a, b_a, w_x, b_x, lam):
    wn, wt = _prep_in_weights(w_in)
    y_ret, y_lru, kcn, vcn, ks, kw, ng, qt, vst, vwt, glt = _project_and_mix(
        x, wn, wt, conv_w, conv_b, w_a, b_a, w_x, b_x, lam)
    prow, wl = _prep_cmp_weights(cmp_pos, cmp_w)
    kc, vct = _compress(kcn, vcn, prow, wl)
    y_nsa = _nsa_attention(qt, kc, vct, ks, vst, kw, vwt, glt, ng)
    return _out_proj(x, y_ret, y_nsa, y_lru, w_out, ln_g, ln_b)


def kernel(x, w_in, w_out, ln_g, ln_b, nsa_cmp_pos, nsa_cmp_w, lru_conv_w, lru_conv_b,
           lru_w_a, lru_b_a, lru_w_x, lru_b_x, lru_lambda):
    assert x.shape[1] % max(PROJ_TM, RET_CHUNK, LRU_TS, OUT_TM, NSA_TQ) == 0
    assert x.shape[1] // CMP_STRIDE == _CMP_ROWS
    for l in range(DEPTH):
        x = _layer(x, w_in[l], w_out[l], ln_g[l], ln_b[l], nsa_cmp_pos[l], nsa_cmp_w[l],
                   lru_conv_w[l], lru_conv_b[l], lru_w_a[l], lru_b_a[l], lru_w_x[l], lru_b_x[l],
                   lru_lambda[l])
    return x
```

```python
import functools
import math

import jax
import jax.numpy as jnp
import numpy as np
from jax import lax
from jax.experimental import pallas as pl
from jax.experimental.pallas import tpu as pltpu

F32 = jnp.float32
BF16 = jnp.bfloat16

D_MODEL = 1024
DEPTH = 2
HEAD_DIM = 64
RET_HEADS = 4
RET_WIDTH = RET_HEADS * HEAD_DIM
ROPE_BASE = 10000.0
NSA_HEADS = 8
NSA_KV_HEADS = 2
NSA_GROUP = NSA_HEADS // NSA_KV_HEADS
NSA_WIDTH = NSA_HEADS * HEAD_DIM
NSA_KV_WIDTH = NSA_KV_HEADS * HEAD_DIM
CMP_LEN = 32
CMP_STRIDE = 16
SEL_LEN = 64
N_SELECT = 16
WINDOW = 512
N_BRANCH = 3
FORCE_SCORE = 1.0e4
LRU_WIDTH = 256
LRU_BLOCKS = 4
LRU_BLOCK_DIM = LRU_WIDTH // LRU_BLOCKS
CONV_WIDTH = 4
LRU_C = 8.0
DEEPNORM_ALPHA = (2.0 * DEPTH) ** 0.25
LN_EPS = 1e-5
NEG = -1e30
LOG2E = math.log2(math.e)
SEL_SHIFT = int(math.log2(SEL_LEN))

IN_SIZES = (RET_WIDTH,) * 4 + (NSA_WIDTH,) + (NSA_KV_WIDTH,) * 6 + (
    NSA_WIDTH, NSA_HEADS * N_BRANCH, LRU_WIDTH, LRU_WIDTH)
IN_OFFS = tuple(int(v) for v in np.cumsum((0,) + IN_SIZES))

V7X_VMEM_LIMIT = 48 * 1024 * 1024
SUBLANES = 8
LANES = 128
BF16_SUBLANES = 16
PROJ_TM = 512
RET_CHUNK = 256
LRU_TS = 256
OUT_TM = 512
NSA_TQ = 256
NSA_KB = 256
GL_ROWS = 16
V_ROWS = HEAD_DIM + BF16_SUBLANES

_CONTRACT_LAST = (((1,), (1,)), ((), ()))
_CONTRACT_FIRST = (((0,), (0,)), ((), ()))


def _cparams(sem):
    return pltpu.CompilerParams(dimension_semantics=sem, vmem_limit_bytes=V7X_VMEM_LIMIT)


_WN_COLS = (4 * RET_WIDTH, 2 * NSA_KV_WIDTH, 2 * NSA_KV_WIDTH, NSA_WIDTH, 2 * LRU_WIDTH)
_WN_OFFS = tuple(int(v) for v in np.cumsum((0,) + _WN_COLS))
_WT_ROWS = (NSA_WIDTH, NSA_KV_WIDTH, NSA_KV_WIDTH, NSA_KV_HEADS * GL_ROWS)
_WT_OFFS = tuple(int(v) for v in np.cumsum((0,) + _WT_ROWS))


def _proj_kernel(x_ref, wn_ref, wt_ref,
                 cos_ref, sa_ref, sb_ref, qd_ref, kd_ref, dmat_ref,
                 cw_ref, cb_ref, wa_ref, ba_ref, wx_ref, bx_ref, lam_ref,
                 yret_ref, ylru_ref, kcn_ref, vcn_ref, ks_ref, kw_ref, ng_ref,
                 qt_ref, vst_ref, vwt_ref, glt_ref,
                 state_ref, xext_ref, h_ref, *, chunk_decay):
    tm = x_ref.shape[0]

    @pl.when(pl.program_id(1) == 0)
    def _():
        state_ref[...] = jnp.zeros_like(state_ref)
        xext_ref[0:SUBLANES, :] = jnp.zeros((SUBLANES, LRU_WIDTH), F32)
        h_ref[...] = jnp.zeros_like(h_ref)

    xb = x_ref[...].astype(BF16)

    def nat(i):
        return jnp.dot(xb, wn_ref[:, _WN_OFFS[i]:_WN_OFFS[i + 1]], preferred_element_type=F32)

    def tr(i):
        return lax.dot_general(wt_ref[_WT_OFFS[i]:_WT_OFFS[i + 1], :], xb, _CONTRACT_LAST,
                               preferred_element_type=F32)

    ret = nat(0)
    lru = nat(4)
    for c0 in range(0, tm, RET_CHUNK):
        rows = slice(c0, c0 + RET_CHUNK)
        yret_ref[rows, :] = _retention_chunk(
            ret[rows, :], cos_ref[rows, :], sa_ref[rows, :], sb_ref[rows, :], qd_ref[...],
            kd_ref[...], dmat_ref, state_ref, chunk_decay).astype(yret_ref.dtype)
    for c0 in range(0, tm, LRU_TS):
        rows = slice(c0, c0 + LRU_TS)
        ylru_ref[rows, :] = _lru_tile(lru[rows, :], cw_ref, cb_ref, wa_ref, ba_ref, wx_ref, bx_ref,
                                      lam_ref, xext_ref, h_ref).astype(ylru_ref.dtype)

    kvc = nat(1)
    kcn_ref[...] = kvc[:, 0:NSA_KV_WIDTH]
    vcn_ref[...] = kvc[:, NSA_KV_WIDTH:2 * NSA_KV_WIDTH]
    ksw = nat(2)
    ks = ksw[:, 0:NSA_KV_WIDTH]
    key = pl.program_id(1) * tm + lax.broadcasted_iota(jnp.int32, (tm, 1), 0)
    blk = lax.broadcasted_iota(jnp.int32, (1, HEAD_DIM), 1)
    onehot = jnp.where(lax.shift_right_logical(key, SEL_SHIFT) == blk, 1.0, 0.0)
    for g in range(NSA_KV_HEADS):
        ks_ref[g] = jnp.concatenate([ks[:, g * HEAD_DIM:(g + 1) * HEAD_DIM], onehot],
                                    axis=1).astype(BF16)
    kw_ref[...] = ksw[:, NSA_KV_WIDTH:2 * NSA_KV_WIDTH].astype(BF16)
    ng_ref[...] = nat(3)
    qt_ref[...] = (tr(0) * LOG2E).astype(BF16)
    ones_rows = jnp.where(lax.broadcasted_iota(jnp.int32, (BF16_SUBLANES, tm), 0) == 0,
                          1.0, 0.0).astype(BF16)
    for ref, i in ((vst_ref, 1), (vwt_ref, 2)):
        vt = tr(i)
        for g in range(NSA_KV_HEADS):
            ref[g * V_ROWS:g * V_ROWS + HEAD_DIM, :] = (
                vt[g * HEAD_DIM:(g + 1) * HEAD_DIM, :].astype(BF16))
            ref[g * V_ROWS + HEAD_DIM:(g + 1) * V_ROWS, :] = ones_rows
    glt_ref[...] = tr(3)


def _project_and_mix(x, wn, wt, conv_w, conv_b, w_a, b_a, w_x, b_x, lam):
    B, S, _ = x.shape
    tm = PROJ_TM
    nn, nt = wn.shape[1], wt.shape[0]
    G, C, W, LW = NSA_KV_HEADS, RET_CHUNK, RET_WIDTH, LRU_WIDTH
    cos, sa, sb, qd, kd, dmat, chunk_decay = _ret_tables(S)
    row = lambda w: pl.BlockSpec((None, tm, w), lambda b, i: (b, i, 0))
    col = lambda r: pl.BlockSpec((None, r, tm), lambda b, i: (b, 0, i))
    kspec = pl.BlockSpec((None, G, tm, 2 * HEAD_DIM), lambda b, i: (b, 0, i, 0))
    fixed = lambda *shape: pl.BlockSpec(shape, lambda b, i: (0,) * len(shape))
    pos = pl.BlockSpec((tm, W), lambda b, i: (i, 0))
    vec = lambda a: a.reshape(1, LW)
    out_shape = (
        jax.ShapeDtypeStruct((B, S, W), BF16),
        jax.ShapeDtypeStruct((B, S, LW), BF16),
        jax.ShapeDtypeStruct((B, S, NSA_KV_WIDTH), F32),
        jax.ShapeDtypeStruct((B, S, NSA_KV_WIDTH), F32),
        jax.ShapeDtypeStruct((B, G, S, 2 * HEAD_DIM), BF16),
        jax.ShapeDtypeStruct((B, S, NSA_KV_WIDTH), BF16),
        jax.ShapeDtypeStruct((B, S, NSA_WIDTH), F32),
        jax.ShapeDtypeStruct((B, NSA_WIDTH, S), BF16),
        jax.ShapeDtypeStruct((B, G * V_ROWS, S), BF16),
        jax.ShapeDtypeStruct((B, G * V_ROWS, S), BF16),
        jax.ShapeDtypeStruct((B, G * GL_ROWS, S), F32),
    )
    return pl.pallas_call(
        functools.partial(_proj_kernel, chunk_decay=chunk_decay),
        grid=(B, S // tm),
        in_specs=[row(D_MODEL), fixed(D_MODEL, nn), fixed(nt, D_MODEL),
                  pos, pos, pos, fixed(C, W), fixed(C, W), fixed(RET_HEADS, C, C),
                  fixed(CONV_WIDTH, LW), fixed(1, LW), fixed(LW, LW), fixed(1, LW), fixed(LW, LW),
                  fixed(1, LW), fixed(1, LW)],
        out_specs=(row(W), row(LW), row(NSA_KV_WIDTH), row(NSA_KV_WIDTH), kspec,
                   row(NSA_KV_WIDTH), row(NSA_WIDTH), col(NSA_WIDTH), col(G * V_ROWS),
                   col(G * V_ROWS), col(G * GL_ROWS)),
        out_shape=out_shape,
        scratch_shapes=[pltpu.VMEM((RET_HEADS, HEAD_DIM, HEAD_DIM), F32),
                        pltpu.VMEM((LRU_TS + SUBLANES, LW), F32), pltpu.VMEM((SUBLANES, LW), F32)],
        compiler_params=_cparams(("parallel", "arbitrary")),
        name="in_proj_mix",
    )(x, wn, wt, jnp.asarray(cos), jnp.asarray(sa), jnp.asarray(sb), jnp.asarray(qd),
      jnp.asarray(kd), jnp.asarray(dmat), conv_w, vec(conv_b), _block_diag(w_a).astype(BF16),
      vec(b_a), _block_diag(w_x).astype(BF16), vec(b_x), vec(lam))


def _prep_in_weights(w_in):
    seg = [w_in[:, IN_OFFS[i]:IN_OFFS[i + 1]] for i in range(len(IN_SIZES))]
    (rq, rk, rv, rg, nq, nkc, nvc, nks, nvs, nkw, nvw, ng, ngl, lx, lg) = seg
    wn = jnp.concatenate([rq, rk, rv, rg, nkc, nvc, nks, nkw, ng, lx, lg], axis=1).astype(BF16)
    glt = ngl.T.reshape(NSA_KV_HEADS, NSA_GROUP * N_BRANCH, D_MODEL)
    glt = jnp.pad(glt, ((0, 0), (0, GL_ROWS - NSA_GROUP * N_BRANCH), (0, 0)))
    wt = jnp.concatenate([nq.T * (HEAD_DIM ** -0.5), nvs.T, nvw.T,
                          glt.reshape(NSA_KV_HEADS * GL_ROWS, D_MODEL)], axis=0).astype(BF16)
    return wn, wt


def _retention_chunk(ret, cos, sa, sb, qdec, kdec, dmat_ref, state_ref, chunk_decay):
    W = RET_WIDTH
    q = ret[:, 0:W]
    k = ret[:, W:2 * W]
    v = ret[:, 2 * W:3 * W]
    gate = ret[:, 3 * W:4 * W]
    half = HEAD_DIM // 2

    def rope(t):
        cols = []
        for c0 in range(0, W, LANES):
            tc = t[:, c0:c0 + LANES]
            cols.append(tc * cos[:, c0:c0 + LANES]
                        + pltpu.roll(tc, LANES - half, 1) * sa[:, c0:c0 + LANES]
                        + pltpu.roll(tc, half, 1) * sb[:, c0:c0 + LANES])
        return jnp.concatenate(cols, axis=1)

    qr = rope(q)
    kr = rope(k) * (HEAD_DIM ** -0.5)
    qd = qr * qdec
    kd = kr * kdec
    ys = []
    for h in range(RET_HEADS):
        sl = slice(h * HEAD_DIM, (h + 1) * HEAD_DIM)
        qb, kb, vb = qr[:, sl].astype(BF16), kr[:, sl].astype(BF16), v[:, sl].astype(BF16)
        s = lax.dot_general(qb, kb, _CONTRACT_LAST, preferred_element_type=F32) * dmat_ref[h]
        inner = jnp.dot(s.astype(BF16), vb, preferred_element_type=F32)
        st = state_ref[h]
        cross = jnp.dot(qd[:, sl].astype(BF16), st.astype(BF16), preferred_element_type=F32)
        y = inner + cross
        mu = jnp.mean(y, axis=-1, keepdims=True)
        yc = y - mu
        var = jnp.mean(yc * yc, axis=-1, keepdims=True)
        ys.append(yc * lax.rsqrt(var + LN_EPS))
        kv = lax.dot_general(kd[:, sl].astype(BF16), vb, _CONTRACT_FIRST,
                             preferred_element_type=F32)
        state_ref[h] = st * chunk_decay[h] + kv
    y = jnp.concatenate(ys, axis=1)
    return y * (gate * jax.nn.sigmoid(gate))


def _ret_tables(S):
    C, H, d = RET_CHUNK, RET_HEADS, HEAD_DIM
    half = d // 2
    inv = 1.0 / (ROPE_BASE ** (np.arange(half, dtype=np.float64) / half))
    ang = np.arange(S, dtype=np.float64)[:, None] * inv[None, :]
    cos_h = np.concatenate([np.cos(ang), np.cos(ang)], axis=1)
    sin_lo = np.concatenate([-np.sin(ang), np.zeros_like(ang)], axis=1)
    sin_hi = np.concatenate([np.zeros_like(ang), np.sin(ang)], axis=1)
    tile = lambda a: np.tile(a, (1, H)).astype(np.float32)
    log_g = np.log(1.0 - 2.0 ** (-5.0 - np.arange(H, dtype=np.float64)))
    idx = np.arange(C, dtype=np.float64)
    diff = idx[:, None] - idx[None, :]
    dmat = np.where(diff >= 0, np.exp(np.maximum(diff, 0.0)[None] * log_g[:, None, None]), 0.0)
    kd = np.exp((C - 1.0 - idx)[:, None] * log_g[None, :])
    qd = np.exp((idx + 1.0)[:, None] * log_g[None, :])
    rep = lambda a: np.repeat(a, d, axis=1).astype(np.float32)
    chunk_decay = tuple(float(v) for v in np.exp(C * log_g))
    return (tile(cos_h), tile(sin_lo), tile(sin_hi), rep(qd), rep(kd),
            dmat.astype(np.float32), chunk_decay)


_CMP_ROWS = 256


def _cmp_kernel(kcn_ref, vcn_ref, pos_ref, wl_ref, kc_ref, vct_ref):
    n, W = _CMP_ROWS, NSA_KV_WIDTH
    res = []
    for i, ref in enumerate((kcn_ref, vcn_ref)):
        out = jnp.zeros((n, 2 * W), F32)
        cst = jnp.zeros((SUBLANES, 2 * W), F32)
        for l in range(CMP_STRIDE):
            x = ref[pl.ds(l, n, stride=CMP_STRIDE), :].astype(BF16)
            out = out + jnp.dot(x, wl_ref[i, l], preferred_element_type=F32)
            cst = cst + jnp.dot(pos_ref[i, l].astype(BF16), wl_ref[i, l],
                                preferred_element_type=F32)
        const = cst[0:1, 0:W] + cst[1:2, W:2 * W]
        res.append(out[:, 0:W] + pltpu.roll(out[:, W:2 * W], n - 1, 0) + const)
    kc_ref[...] = res[0].astype(BF16)
    vct_ref[...] = res[1].T.astype(BF16)


def _prep_cmp_weights(cmp_pos, cmp_w):
    L2, d, G = CMP_STRIDE, HEAD_DIM, NSA_KV_HEADS
    assert G == 2
    w5 = cmp_w.reshape(2, 2, L2, d, d)
    z = jnp.zeros((2, L2, d, d), F32)

    def both_groups(w):
        return jnp.concatenate([jnp.concatenate([w, z], axis=-1),
                                jnp.concatenate([z, w], axis=-1)], axis=-2)

    wl = jnp.concatenate([both_groups(w5[:, 0]), both_groups(w5[:, 1])], axis=-1).astype(BF16)
    p4 = cmp_pos.reshape(2, 2, L2, d)
    prow = jnp.concatenate([p4, p4], axis=-1).transpose(0, 2, 1, 3)
    prow = jnp.pad(prow, ((0, 0), (0, 0), (0, SUBLANES - 2), (0, 0)))
    return prow, wl


def _compress(kcn, vcn, prow, wl):
    B, S, W = kcn.shape
    n = _CMP_ROWS
    return pl.pallas_call(
        _cmp_kernel,
        grid=(B,),
        in_specs=[pl.BlockSpec((None, S, W), lambda b: (b, 0, 0)),
                  pl.BlockSpec((None, S, W), lambda b: (b, 0, 0)),
                  pl.BlockSpec(prow.shape, lambda b: (0, 0, 0, 0)),
                  pl.BlockSpec(wl.shape, lambda b: (0, 0, 0, 0))],
        out_specs=(pl.BlockSpec((None, n, NSA_KV_WIDTH), lambda b: (b, 0, 0)),
                   pl.BlockSpec((None, NSA_KV_WIDTH, n), lambda b: (b, 0, 0))),
        out_shape=(jax.ShapeDtypeStruct((B, n, NSA_KV_WIDTH), BF16),
                   jax.ShapeDtypeStruct((B, NSA_KV_WIDTH, n), BF16)),
        compiler_params=_cparams(("parallel",)),
        name="nsa_compress",
    )(kcn, vcn, prow, wl)


def _nsa_kernel(qt_ref, kc_ref, vct_ref, ks_ref, vst_ref, kw_ref, vwt_ref, glt_ref, ng_ref,
                selmap_ref, band_ref, cmpmask_ref, out_ref, val_ref, cnt_ref, s_ref, p_ref, acc_ref,
                st_ref):
    TQ, KB, R, d = NSA_TQ, NSA_KB, NSA_GROUP, HEAD_DIM
    n_sel = selmap_ref.shape[0]
    g = pl.program_id(1)
    qi = pl.program_id(2)
    t0 = qi * TQ
    heads = [slice(r * TQ, (r + 1) * TQ) for r in range(R)]
    qs = [qt_ref[r * d:(r + 1) * d, :] for r in range(R)]
    zq = jnp.zeros_like(qs[0])
    qs_nat = [jnp.concatenate([jnp.where(g == gg, q, zq) for gg in range(NSA_KV_HEADS)], axis=0)
              for q in qs]

    def key_off(kb):
        return pl.multiple_of(kb * KB, KB)

    n_wb = WINDOW // KB + 1
    w_kb = [qi - (n_wb - 1) + w for w in range(n_wb)]
    w_off = [key_off(jnp.maximum(kb, 0)) for kb in w_kb]
    w_bias = [band_ref[1]] + [None] * (n_wb - 2) + [band_ref[0]]
    qs_nat_all = jnp.concatenate(qs_nat, axis=1)

    n_cmp = kc_ref.shape[0]
    cmp_off = pl.multiple_of(n_cmp - qi * (TQ // CMP_STRIDE), TQ // CMP_STRIDE)
    cmp_bias = jnp.concatenate([cmpmask_ref[pl.ds(cmp_off, n_cmp), :]] * R, axis=1)
    tq = t0 + lax.broadcasted_iota(jnp.int32, (1, TQ), 1)
    seen = jnp.concatenate([tq >= CMP_LEN - 1] * R, axis=1)
    s = jnp.dot(kc_ref[...], qs_nat_all, preferred_element_type=F32) + cmp_bias

    w_s = []
    for w in range(n_wb):
        s_w = jnp.dot(kw_ref[pl.ds(w_off[w], KB), :], qs_nat_all, preferred_element_type=F32)
        if w_bias[w] is not None:
            s_w = s_w + jnp.concatenate([w_bias[w]] * R, axis=1)
        if w < n_wb - 1:
            s_w = s_w + jnp.where(w_kb[w] >= 0, 0.0, NEG)
        w_s.append(s_w)

    p = jnp.exp2(s - jnp.max(s, axis=0, keepdims=True))
    p = p * jnp.where(seen, 1.0 / jnp.sum(p, axis=0, keepdims=True), 0.0)
    o_c_all = jnp.dot(vct_ref[...], p.astype(BF16), preferred_element_type=F32)
    o_c = [o_c_all[:, hs] for hs in heads]
    psum = functools.reduce(jnp.add, [p[:, hs] for hs in heads])

    p_hi = psum.astype(BF16)
    p_lo = (psum - p_hi.astype(F32)).astype(BF16)
    imp = (jnp.dot(selmap_ref[...], p_hi, preferred_element_type=F32)
           + jnp.dot(selmap_ref[...], p_lo, preferred_element_type=F32))

    jrow = lax.broadcasted_iota(jnp.int32, (n_sel, 1), 0)
    cur = lax.shift_right_logical(tq, SEL_SHIFT)
    forced = (jrow == 0) | (jrow == cur) | (jrow == cur - 1)
    causal = jrow * SEL_LEN <= tq
    val_ref[...] = jnp.where(forced, FORCE_SCORE, jnp.where(causal, imp, -FORCE_SCORE))
    cnt_ref[...] = jnp.zeros_like(cnt_ref)
    n_live = (qi + 1) * (TQ // SEL_LEN)
    n_grp = n_sel // SUBLANES
    for c in range(n_grp):

        @pl.when(c * SUBLANES < n_live)
        def _():
            groups = [slice(k * SUBLANES, (k + 1) * SUBLANES) for k in range(n_grp)]
            vals = [val_ref[rows, :] for rows in groups]
            cnts = [cnt_ref[rows, :] for rows in groups]
            own = lax.broadcasted_iota(jnp.int32, (SUBLANES, 1), 0) + c * SUBLANES
            for jp in range(c * SUBLANES, (c + 1) * SUBLANES):
                cand = jnp.broadcast_to(val_ref[jp:jp + 1, :], (SUBLANES, TQ))
                for k in range(n_grp):
                    if k < c:
                        hit = jnp.where(cand > vals[k], 1.0, 0.0)
                    elif k > c:
                        hit = jnp.where(cand >= vals[k], 1.0, 0.0)
                    else:
                        hit = jnp.where(own > jp, jnp.where(cand >= vals[k], 1.0, 0.0),
                                        jnp.where(cand > vals[k], 1.0, 0.0))
                    cnts[k] = cnts[k] + hit
            for rows, cnt in zip(groups, cnts):
                cnt_ref[rows, :] = cnt

    bias = jnp.where(cnt_ref[...] < float(N_SELECT), 0.0, NEG).astype(BF16)
    qs_sel = [jnp.concatenate([q, bias], axis=0) for q in qs]

    def sel_scores(kb, r):
        return jnp.dot(ks_ref[pl.ds(key_off(kb), KB), :], qs_sel[r], preferred_element_type=F32)

    def sel_values(kb, p):
        return jnp.dot(vst_ref[:, pl.ds(key_off(kb), KB)], p, preferred_element_type=F32)

    ROW_M, ROW_A = 0, 1

    def sel_stage(i, cur_slot, *, diag, mxu_first=False):
        nxt = 1 - cur_slot
        for r, hs in enumerate(heads):
            if mxu_first and r == 0:
                pvs = [sel_values(jnp.maximum(i - 1, 0), p_ref[nxt, :, h2]) for h2 in heads]
                for r2, h2 in enumerate(heads):
                    s_ref[nxt, :, h2] = sel_scores(i + 1, r2)
            s = s_ref[cur_slot, :, hs]
            if diag:
                s = s + band_ref[0]
            m = st_ref[ROW_M:ROW_M + 1, hs]
            m_new = jnp.maximum(m, jnp.max(s, axis=0, keepdims=True))
            p_ref[cur_slot, :, hs] = jnp.exp2(s - m_new).astype(BF16)
            if mxu_first:
                pv_prev = pvs[r]
            else:
                pv_prev = sel_values(jnp.maximum(i - 1, 0), p_ref[nxt, :, hs])
                if not diag:
                    s_ref[nxt, :, hs] = sel_scores(i + 1, r)
            acc_ref[:, hs] = st_ref[ROW_A:ROW_A + 1, hs] * acc_ref[:, hs] + pv_prev
            st_ref[ROW_M:ROW_M + 1, hs] = m_new
            st_ref[ROW_A:ROW_A + 1, hs] = jnp.exp2(m - m_new)

    def sel_finish(i, cur_slot):
        for hs in heads:
            acc_ref[:, hs] = (st_ref[ROW_A:ROW_A + 1, hs] * acc_ref[:, hs]
                              + sel_values(i, p_ref[cur_slot, :, hs]))

    for r, hs in enumerate(heads):
        s_ref[0, :, hs] = sel_scores(0, r)
    p_ref[1] = jnp.zeros(p_ref.shape[1:], BF16)
    acc_ref[...] = jnp.zeros_like(acc_ref)
    st_ref[ROW_M:ROW_M + 1, :] = jnp.full((1, R * TQ), NEG, F32)
    st_ref[ROW_A:ROW_A + 1, :] = jnp.ones((1, R * TQ), F32)

    m_w = functools.reduce(jnp.maximum, [jnp.max(s_w, axis=0, keepdims=True) for s_w in w_s])
    acc_w_all = functools.reduce(jnp.add, [
        jnp.dot(vwt_ref[:, pl.ds(w_off[w], KB)], jnp.exp2(w_s[w] - m_w).astype(BF16),
                preferred_element_type=F32) for w in range(n_wb)])
    acc_w = [acc_w_all[:, hs] for hs in heads]

    def sel_pair(j, c):
        sel_stage(2 * j, 0, diag=False)
        sel_stage(2 * j + 1, 1, diag=False, mxu_first=True)
        return c

    lax.fori_loop(0, lax.shift_right_logical(qi, 1), sel_pair, 0)
    odd = lax.rem(qi, 2) == 1

    @pl.when(odd)
    def _():
        sel_stage(qi - 1, 0, diag=False)
        sel_stage(qi, 1, diag=True)
        sel_finish(qi, 1)

    @pl.when(jnp.logical_not(odd))
    def _():
        sel_stage(qi, 0, diag=True)
        sel_finish(qi, 0)

    gl = jax.nn.sigmoid(glt_ref[...])
    res = []
    for r, hs in enumerate(heads):
        a_s = acc_ref[:, hs]
        a_w = acc_w[r]
        gate = [gl[r * N_BRANCH + br:r * N_BRANCH + br + 1, :] for br in range(N_BRANCH)]
        res.append(gate[0] * o_c[r]
                   + (gate[1] / a_s[d:d + 1, :]) * a_s[0:d, :]
                   + (gate[2] / a_w[d:d + 1, :]) * a_w[0:d, :])
    ng = ng_ref[...]
    out_ref[...] = (jnp.concatenate(res, axis=0).T
                    * (ng * jax.nn.sigmoid(ng))).astype(out_ref.dtype)


def _nsa_tables(S):
    n_cmp = (S - CMP_LEN) // CMP_STRIDE + 1
    n_sel = S // SEL_LEN
    ci = np.arange(n_cmp)[:, None]
    sj = np.arange(n_sel)[None, :]
    overlap = (np.minimum(ci * CMP_STRIDE + CMP_LEN, sj * SEL_LEN + SEL_LEN)
               - np.maximum(ci * CMP_STRIDE, sj * SEL_LEN))
    sel_map = np.clip(overlap, 0, None) / CMP_STRIDE
    selmap_t = np.zeros((n_sel, _CMP_ROWS), np.float32)
    selmap_t[:, :n_cmp] = sel_map.T
    jl = np.arange(NSA_KB)[:, None]
    tl = np.arange(NSA_TQ)[None, :]
    band = np.stack([np.where(jl <= tl, 0.0, NEG), np.where(jl > tl, 0.0, NEG)]).astype(np.float32)
    rel = (np.arange(2 * _CMP_ROWS)[:, None] - _CMP_ROWS) * CMP_STRIDE + (CMP_LEN - 1)
    cmpmask = np.where(rel <= tl, 0.0, NEG).astype(np.float32)
    return selmap_t, band, cmpmask


def _nsa_attention(qt, kc, vct, ks, vst, kw, vwt, glt, ng):
    B, _, S = qt.shape
    G, R, d, TQ = NSA_KV_HEADS, NSA_GROUP, HEAD_DIM, NSA_TQ
    selmap_t, band, cmpmask = _nsa_tables(S)
    n_sel = S // SEL_LEN
    assert NSA_KB == TQ and WINDOW % NSA_KB == 0
    assert n_sel == d
    per_b_rows = lambda n: pl.BlockSpec((None, n, G * d), lambda b, g, q: (b, 0, 0))
    per_bg_cols = lambda rows, n: pl.BlockSpec((None, rows, n), lambda b, g, q: (b, g, 0))
    return pl.pallas_call(
        _nsa_kernel,
        grid=(B, G, S // TQ),
        in_specs=[pl.BlockSpec((None, R * d, TQ), lambda b, g, q: (b, g, q)),
                  per_b_rows(_CMP_ROWS), per_bg_cols(d, _CMP_ROWS),
                  pl.BlockSpec((None, None, S, 2 * d), lambda b, g, q: (b, g, 0, 0)),
                  per_bg_cols(V_ROWS, S), per_b_rows(S), per_bg_cols(V_ROWS, S),
                  pl.BlockSpec((None, GL_ROWS, TQ), lambda b, g, q: (b, g, q)),
                  pl.BlockSpec((None, TQ, R * d), lambda b, g, q: (b, q, g)),
                  pl.BlockSpec(selmap_t.shape, lambda b, g, q: (0, 0)),
                  pl.BlockSpec(band.shape, lambda b, g, q: (0, 0, 0)),
                  pl.BlockSpec(cmpmask.shape, lambda b, g, q: (0, 0))],
        out_specs=pl.BlockSpec((None, TQ, R * d), lambda b, g, q: (b, q, g)),
        out_shape=jax.ShapeDtypeStruct((B, S, NSA_WIDTH), BF16),
        scratch_shapes=[pltpu.VMEM((n_sel, TQ), F32), pltpu.VMEM((n_sel, TQ), F32),
                        pltpu.VMEM((2, NSA_KB, R * TQ), F32),
                        pltpu.VMEM((2, NSA_KB, R * TQ), BF16),
                        pltpu.VMEM((V_ROWS, R * TQ), F32),
                        pltpu.VMEM((SUBLANES, R * TQ), F32)],
        compiler_params=_cparams(("parallel", "parallel", "arbitrary")),
        name="nsa_attention",
    )(qt, kc, vct, ks, vst, kw, vwt, glt, ng, jnp.asarray(selmap_t, dtype=BF16), jnp.asarray(band),
      jnp.asarray(cmpmask))


def _nsa_pair_kernel(qt_ref, kc_ref, vct_ref, ks_ref, vst_ref, kw_ref, vwt_ref, glt_ref, ng_ref,
                     selmap_ref, band_ref, cmpmask_ref, out_ref, val_ref, cnt_ref, s_ref, p_ref,
                     acc_ref, st_ref):
    TQ, KB, R, d, G = NSA_TQ, NSA_KB, NSA_GROUP, HEAD_DIM, NSA_KV_HEADS
    n_sel = selmap_ref.shape[0]
    qi = pl.program_id(1)
    t0 = qi * TQ
    groups = range(G)
    heads = [slice(r * TQ, (r + 1) * TQ) for r in range(R)]
    qs = [[qt_ref[(g * R + r) * d:(g * R + r + 1) * d, :] for r in range(R)] for g in groups]
    zq = jnp.zeros((d, TQ), BF16)
    qs_nat = [jnp.concatenate(
        [jnp.concatenate([q if gg == g else zq for gg in groups], axis=0) for q in qs[g]], axis=1)
        for g in groups]

    def key_off(kb):
        return pl.multiple_of(kb * KB, KB)

    n_wb = WINDOW // KB + 1
    w_kb = [qi - (n_wb - 1) + w for w in range(n_wb)]
    w_off = [key_off(jnp.maximum(kb, 0)) for kb in w_kb]
    w_bias = [band_ref[1]] + [None] * (n_wb - 2) + [band_ref[0]]
    n_cmp = kc_ref.shape[0]
    cmp_off = pl.multiple_of(n_cmp - qi * (TQ // CMP_STRIDE), TQ // CMP_STRIDE)
    cmp_bias = jnp.concatenate([cmpmask_ref[pl.ds(cmp_off, n_cmp), :]] * R, axis=1)
    tq = t0 + lax.broadcasted_iota(jnp.int32, (1, TQ), 1)
    seen = jnp.concatenate([tq >= CMP_LEN - 1] * R, axis=1)

    s_cmp = [jnp.dot(kc_ref[...], qs_nat[g], preferred_element_type=F32) + cmp_bias
             for g in groups]
    w_s = []
    for g in groups:
        blocks = []
        for w in range(n_wb):
            s_w = jnp.dot(kw_ref[pl.ds(w_off[w], KB), :], qs_nat[g], preferred_element_type=F32)
            if w_bias[w] is not None:
                s_w = s_w + jnp.concatenate([w_bias[w]] * R, axis=1)
            if w < n_wb - 1:
                s_w = s_w + jnp.where(w_kb[w] >= 0, 0.0, NEG)
            blocks.append(s_w)
        w_s.append(blocks)

    jrow = lax.broadcasted_iota(jnp.int32, (n_sel, 1), 0)
    cur = lax.shift_right_logical(tq, SEL_SHIFT)
    forced = (jrow == 0) | (jrow == cur) | (jrow == cur - 1)
    causal = jrow * SEL_LEN <= tq
    o_c = []
    for g in groups:
        p = jnp.exp2(s_cmp[g] - jnp.max(s_cmp[g], axis=0, keepdims=True))
        p = p * jnp.where(seen, 1.0 / jnp.sum(p, axis=0, keepdims=True), 0.0)
        o_c.append(jnp.dot(vct_ref[g * d:(g + 1) * d, :], p.astype(BF16),
                           preferred_element_type=F32))
        psum = functools.reduce(jnp.add, [p[:, hs] for hs in heads])
        p_hi = psum.astype(BF16)
        p_lo = (psum - p_hi.astype(F32)).astype(BF16)
        imp = (jnp.dot(selmap_ref[...], p_hi, preferred_element_type=F32)
               + jnp.dot(selmap_ref[...], p_lo, preferred_element_type=F32))
        val_ref[g] = jnp.where(forced, FORCE_SCORE, jnp.where(causal, imp, -FORCE_SCORE))
    cnt_ref[...] = jnp.zeros_like(cnt_ref)

    n_live = (qi + 1) * (TQ // SEL_LEN)
    n_grp = n_sel // SUBLANES
    row_groups = [slice(k * SUBLANES, (k + 1) * SUBLANES) for k in range(n_grp)]
    for c in range(n_grp):

        @pl.when(c * SUBLANES < n_live)
        def _():
            own = lax.broadcasted_iota(jnp.int32, (SUBLANES, 1), 0) + c * SUBLANES
            for g in groups:
                vals = [val_ref[g, rows, :] for rows in row_groups]
                cnts = [cnt_ref[g, rows, :] for rows in row_groups]
                for jp in range(c * SUBLANES, (c + 1) * SUBLANES):
                    cand = jnp.broadcast_to(val_ref[g, jp:jp + 1, :], (SUBLANES, TQ))
                    for k in range(n_grp):
                        if k < c:
                            hit = jnp.where(cand > vals[k], 1.0, 0.0)
                        elif k > c:
                            hit = jnp.where(cand >= vals[k], 1.0, 0.0)
                        else:
                            hit = jnp.where(own > jp, jnp.where(cand >= vals[k], 1.0, 0.0),
                                            jnp.where(cand > vals[k], 1.0, 0.0))
                        cnts[k] = cnts[k] + hit
                for rows, cnt in zip(row_groups, cnts):
                    cnt_ref[g, rows, :] = cnt

    qs_sel = []
    for g in groups:
        bias = jnp.where(cnt_ref[g] < float(N_SELECT), 0.0, NEG).astype(BF16)
        qs_sel.append([jnp.concatenate([q, bias], axis=0) for q in qs[g]])

    def sel_scores(g, kb, r):
        return jnp.dot(ks_ref[g, pl.ds(key_off(kb), KB), :], qs_sel[g][r],
                       preferred_element_type=F32)

    def sel_values(g, kb, p):
        return jnp.dot(vst_ref[g * V_ROWS:(g + 1) * V_ROWS, pl.ds(key_off(kb), KB)], p,
                       preferred_element_type=F32)

    ROW_M, ROW_A = 0, 1

    def sel_stage(g, i, cur_slot, *, diag, mxu_first=False):
        nxt = 1 - cur_slot
        for r, hs in enumerate(heads):
            if mxu_first and r == 0:
                pvs = [sel_values(g, jnp.maximum(i - 1, 0), p_ref[g, nxt, :, h2]) for h2 in heads]
                for r2, h2 in enumerate(heads):
                    s_ref[g, nxt, :, h2] = sel_scores(g, i + 1, r2)
            s = s_ref[g, cur_slot, :, hs]
            if diag:
                s = s + band_ref[0]
            m = st_ref[g, ROW_M:ROW_M + 1, hs]
            m_new = jnp.maximum(m, jnp.max(s, axis=0, keepdims=True))
            p_ref[g, cur_slot, :, hs] = jnp.exp2(s - m_new).astype(BF16)
            if mxu_first:
                pv_prev = pvs[r]
            else:
                pv_prev = sel_values(g, jnp.maximum(i - 1, 0), p_ref[g, nxt, :, hs])
                if not diag:
                    s_ref[g, nxt, :, hs] = sel_scores(g, i + 1, r)
            acc_ref[g, :, hs] = st_ref[g, ROW_A:ROW_A + 1, hs] * acc_ref[g, :, hs] + pv_prev
            st_ref[g, ROW_M:ROW_M + 1, hs] = m_new
            st_ref[g, ROW_A:ROW_A + 1, hs] = jnp.exp2(m - m_new)

    def sel_finish(g, i, cur_slot):
        for hs in heads:
            acc_ref[g, :, hs] = (st_ref[g, ROW_A:ROW_A + 1, hs] * acc_ref[g, :, hs]
                                 + sel_values(g, i, p_ref[g, cur_slot, :, hs]))

    for g in groups:
        for r, hs in enumerate(heads):
            s_ref[g, 0, :, hs] = sel_scores(g, 0, r)
        p_ref[g, 1] = jnp.zeros(p_ref.shape[2:], BF16)
        acc_ref[g] = jnp.zeros(acc_ref.shape[1:], F32)
        st_ref[g, ROW_M:ROW_M + 1, :] = jnp.full((1, R * TQ), NEG, F32)
        st_ref[g, ROW_A:ROW_A + 1, :] = jnp.ones((1, R * TQ), F32)

    acc_w = []
    for g in groups:
        m_w = functools.reduce(jnp.maximum,
                               [jnp.max(s_w, axis=0, keepdims=True) for s_w in w_s[g]])
        acc_w.append(functools.reduce(jnp.add, [
            jnp.dot(vwt_ref[g * V_ROWS:(g + 1) * V_ROWS, pl.ds(w_off[w], KB)],
                    jnp.exp2(w_s[g][w] - m_w).astype(BF16), preferred_element_type=F32)
            for w in range(n_wb)]))

    def sel_pair(j, c):
        for g in groups:
            sel_stage(g, 2 * j, 0, diag=False)
        for g in groups:
            sel_stage(g, 2 * j + 1, 1, diag=False, mxu_first=True)
        return c

    lax.fori_loop(0, lax.shift_right_logical(qi, 1), sel_pair, 0)
    odd = lax.rem(qi, 2) == 1

    @pl.when(odd)
    def _():
        for g in groups:
            sel_stage(g, qi - 1, 0, diag=False)
        for g in groups:
            sel_stage(g, qi, 1, diag=True)
        for g in groups:
            sel_finish(g, qi, 1)

    @pl.when(jnp.logical_not(odd))
    def _():
        for g in groups:
            sel_stage(g, qi, 0, diag=True)
        for g in groups:
            sel_finish(g, qi, 0)

    gl = jax.nn.sigmoid(glt_ref[...])
    for g in groups:
        res = []
        for r, hs in enumerate(heads):
            a_s = acc_ref[g, :, hs]
            a_w = acc_w[g][:, hs]
            row0 = g * GL_ROWS + r * N_BRANCH
            gate = [gl[row0 + br:row0 + br + 1, :] for br in range(N_BRANCH)]
            res.append(gate[0] * o_c[g][:, hs]
                       + (gate[1] / a_s[d:d + 1, :]) * a_s[0:d, :]
                       + (gate[2] / a_w[d:d + 1, :]) * a_w[0:d, :])
        cols = slice(g * R * d, (g + 1) * R * d)
        ng = ng_ref[:, cols]
        out_ref[:, cols] = (jnp.concatenate(res, axis=0).T
                            * (ng * jax.nn.sigmoid(ng))).astype(out_ref.dtype)


def _nsa_attention_pair(qt, kc, vct, ks, vst, kw, vwt, glt, ng):
    B, _, S = qt.shape
    G, R, d, TQ = NSA_KV_HEADS, NSA_GROUP, HEAD_DIM, NSA_TQ
    selmap_t, band, cmpmask = _nsa_tables(S)
    n_sel = S // SEL_LEN
    assert NSA_KB == TQ and WINDOW % NSA_KB == 0
    assert n_sel == d
    per_b = lambda *shape: pl.BlockSpec((None,) + shape, lambda b, q: (b,) + (0,) * len(shape))
    fixed = lambda a: pl.BlockSpec(a.shape, lambda b, q: (0,) * a.ndim)
    return pl.pallas_call(
        _nsa_pair_kernel,
        grid=(B, S // TQ),
        in_specs=[pl.BlockSpec((None, G * R * d, TQ), lambda b, q: (b, 0, q)),
                  per_b(_CMP_ROWS, G * d), per_b(G * d, _CMP_ROWS),
                  per_b(G, S, 2 * d), per_b(G * V_ROWS, S), per_b(S, G * d), per_b(G * V_ROWS, S),
                  pl.BlockSpec((None, G * GL_ROWS, TQ), lambda b, q: (b, 0, q)),
                  pl.BlockSpec((None, TQ, G * R * d), lambda b, q: (b, q, 0)),
                  fixed(selmap_t), fixed(band), fixed(cmpmask)],
        out_specs=pl.BlockSpec((None, TQ, G * R * d), lambda b, q: (b, q, 0)),
        out_shape=jax.ShapeDtypeStruct((B, S, NSA_WIDTH), BF16),
        scratch_shapes=[pltpu.VMEM((G, n_sel, TQ), F32), pltpu.VMEM((G, n_sel, TQ), F32),
                        pltpu.VMEM((G, 2, NSA_KB, R * TQ), F32),
                        pltpu.VMEM((G, 2, NSA_KB, R * TQ), BF16),
                        pltpu.VMEM((G, V_ROWS, R * TQ), F32),
                        pltpu.VMEM((G, SUBLANES, R * TQ), F32)],
        compiler_params=_cparams(("parallel", "arbitrary")),
        name="nsa_attention",
    )(qt, kc, vct, ks, vst, kw, vwt, glt, ng, jnp.asarray(selmap_t, dtype=BF16), jnp.asarray(band),
      jnp.asarray(cmpmask))


def _lru_tile(lru, cw_ref, cb_ref, wa_ref, ba_ref, wx_ref, bx_ref, lam_ref, xext_ref, h_ref):
    ts, W, pad = LRU_TS, LRU_WIDTH, SUBLANES
    x = lru[:, 0:W]
    gate = lru[:, W:2 * W]
    xext_ref[pad:pad + ts, :] = x
    first = pad - (CONV_WIDTH - 1)
    xc = cb_ref[...] + cw_ref[0:1, :] * xext_ref[first:first + ts, :]
    for w in range(1, CONV_WIDTH):
        xc = xc + cw_ref[w:w + 1, :] * xext_ref[first + w:first + w + ts, :]
    xext_ref[0:pad, :] = x[ts - pad:ts, :]
    xcb = xc.astype(BF16)
    r = jax.nn.sigmoid(jnp.dot(xcb, wa_ref[...], preferred_element_type=F32) + ba_ref[...])
    gi = jax.nn.sigmoid(jnp.dot(xcb, wx_ref[...], preferred_element_type=F32) + bx_ref[...])
    nl = -lam_ref[...]
    softplus = jnp.maximum(nl, 0.0) + jnp.log1p(jnp.exp(-jnp.abs(nl)))
    log_a = (-LRU_C) * r * softplus
    a = jnp.exp(log_a)
    u = jnp.sqrt(-jnp.tanh(log_a) * (a * a + 1.0)) * (gi * xc)
    row = lax.broadcasted_iota(jnp.int32, (ts, 1), 0)
    step = 1
    while step < ts:
        if step < SUBLANES:
            keep = row >= step
            a_sh = jnp.where(keep, pltpu.roll(a, step, 0), 1.0)
            u_sh = jnp.where(keep, pltpu.roll(u, step, 0), 0.0)
        else:
            a_sh = jnp.concatenate([jnp.ones((step, W), F32), a[:ts - step, :]], axis=0)
            u_sh = jnp.concatenate([jnp.zeros((step, W), F32), u[:ts - step, :]], axis=0)
        u = a * u_sh + u
        a = a * a_sh
        step *= 2
    h = u + a * h_ref[0:1, :]
    h_ref[0:1, :] = h[ts - 1:ts, :]
    return h * (gate * jax.nn.sigmoid(gate))


def _block_diag(w):
    n, d, _ = w.shape
    z = jnp.zeros((d, d), w.dtype)
    return jnp.concatenate(
        [jnp.concatenate([w[i] if j == i else z for j in range(n)], axis=1) for i in range(n)],
        axis=0)


def _out_kernel(x_ref, yr_ref, yn_ref, yl_ref, wo_ref, g_ref, b_ref, out_ref):
    o1, o2 = RET_WIDTH, RET_WIDTH + NSA_WIDTH
    y = jnp.dot(yr_ref[...].astype(BF16), wo_ref[0:o1, :], preferred_element_type=F32)
    y = y + jnp.dot(yn_ref[...].astype(BF16), wo_ref[o1:o2, :], preferred_element_type=F32)
    y = y + jnp.dot(yl_ref[...].astype(BF16), wo_ref[o2:D_MODEL, :], preferred_element_type=F32)
    z = DEEPNORM_ALPHA * x_ref[...] + y
    mu = jnp.mean(z, axis=-1, keepdims=True)
    zc = z - mu
    var = jnp.mean(zc * zc, axis=-1, keepdims=True)
    out_ref[...] = zc * lax.rsqrt(var + LN_EPS) * g_ref[...] + b_ref[...]


def _out_proj(x, y_ret, y_nsa, y_lru, w_out, ln_g, ln_b):
    B, S, _ = x.shape
    tm = OUT_TM
    row = lambda w: pl.BlockSpec((None, tm, w), lambda b, i: (b, i, 0))
    fixed = lambda r: pl.BlockSpec((r, D_MODEL), lambda b, i: (0, 0))
    return pl.pallas_call(
        _out_kernel,
        grid=(B, S // tm),
        in_specs=[row(D_MODEL), row(RET_WIDTH), row(NSA_WIDTH), row(LRU_WIDTH),
                  fixed(D_MODEL), fixed(1), fixed(1)],
        out_specs=row(D_MODEL),
        out_shape=jax.ShapeDtypeStruct((B, S, D_MODEL), F32),
        compiler_params=_cparams(("parallel", "parallel")),
        name="out_proj_ln",
    )(x, y_ret, y_nsa, y_lru, w_out.astype(BF16), ln_g.reshape(1, D_MODEL),
      ln_b.reshape(1, D_MODEL))


def _layer(x, w_in, w_out, ln_g, ln_b, cmp_pos, cmp_w, conv_w, conv_b, w_a, b_a, w_x, b_x, lam):
    wn, wt = _prep_in_weights(w_in)
    y_ret, y_lru, kcn, vcn, ks, kw, ng, qt, vst, vwt, glt = _project_and_mix(
        x, wn, wt, conv_w, conv_b, w_a, b_a, w_x, b_x, lam)
    prow, wl = _prep_cmp_weights(cmp_pos, cmp_w)
    kc, vct = _compress(kcn, vcn, prow, wl)
    y_nsa = _nsa_attention_pair(qt, kc, vct, ks, vst, kw, vwt, glt, ng)
    return _out_proj(x, y_ret, y_nsa, y_lru, w_out, ln_g, ln_b)


def kernel(x, w_in, w_out, ln_g, ln_b, nsa_cmp_pos, nsa_cmp_w, lru_conv_w, lru_conv_b,
           lru_w_a, lru_b_a, lru_w_x, lru_b_x, lru_lambda):
    assert x.shape[1] % max(PROJ_TM, RET_CHUNK, LRU_TS, OUT_TM, NSA_TQ) == 0
    assert x.shape[1] // CMP_STRIDE == _CMP_ROWS
    for l in range(DEPTH):
        x = _layer(x, w_in[l], w_out[l], ln_g[l], ln_b[l], nsa_cmp_pos[l], nsa_cmp_w[l],
                   lru_conv_w[l], lru_conv_b[l], lru_w_a[l], lru_b_a[l], lru_w_x[l], lru_b_x[l],
                   lru_lambda[l])
    return x
```

```python
import functools
import math

import jax
import jax.numpy as jnp
import numpy as np
from jax import lax
from jax.experimental import pallas as pl
from jax.experimental.pallas import tpu as pltpu

F32 = jnp.float32
BF16 = jnp.bfloat16

D_MODEL = 1024
DEPTH = 2
HEAD_DIM = 64
RET_HEADS = 4
RET_WIDTH = RET_HEADS * HEAD_DIM
ROPE_BASE = 10000.0
NSA_HEADS = 8
NSA_KV_HEADS = 2
NSA_GROUP = NSA_HEADS // NSA_KV_HEADS
NSA_WIDTH = NSA_HEADS * HEAD_DIM
NSA_KV_WIDTH = NSA_KV_HEADS * HEAD_DIM
CMP_LEN = 32
CMP_STRIDE = 16
SEL_LEN = 64
N_SELECT = 16
WINDOW = 512
N_BRANCH = 3
FORCE_SCORE = 1.0e4
LRU_WIDTH = 256
LRU_BLOCKS = 4
LRU_BLOCK_DIM = LRU_WIDTH // LRU_BLOCKS
CONV_WIDTH = 4
LRU_C = 8.0
DEEPNORM_ALPHA = (2.0 * DEPTH) ** 0.25
LN_EPS = 1e-5
NEG = -1e30
LOG2E = math.log2(math.e)
SEL_SHIFT = int(math.log2(SEL_LEN))

IN_SIZES = (RET_WIDTH,) * 4 + (NSA_WIDTH,) + (NSA_KV_WIDTH,) * 6 + (
    NSA_WIDTH, NSA_HEADS * N_BRANCH, LRU_WIDTH, LRU_WIDTH)
IN_OFFS = tuple(int(v) for v in np.cumsum((0,) + IN_SIZES))

V7X_VMEM_LIMIT = 48 * 1024 * 1024
SUBLANES = 8
LANES = 128
BF16_SUBLANES = 16
PROJ_TM = 512
RET_CHUNK = 256
LRU_TS = 256
OUT_TM = 512
NSA_TQ = 256
NSA_KB = 256
GL_ROWS = 16
V_ROWS = HEAD_DIM + BF16_SUBLANES

_CONTRACT_LAST = (((1,), (1,)), ((), ()))
_CONTRACT_FIRST = (((0,), (0,)), ((), ()))


def _cparams(sem):
    return pltpu.CompilerParams(dimension_semantics=sem, vmem_limit_bytes=V7X_VMEM_LIMIT)


_WN_COLS = (4 * RET_WIDTH, 2 * NSA_KV_WIDTH, 2 * NSA_KV_WIDTH, NSA_WIDTH, 2 * LRU_WIDTH)
_WN_OFFS = tuple(int(v) for v in np.cumsum((0,) + _WN_COLS))
_WT_ROWS = (NSA_WIDTH, NSA_KV_WIDTH, NSA_KV_WIDTH, NSA_KV_HEADS * GL_ROWS)
_WT_OFFS = tuple(int(v) for v in np.cumsum((0,) + _WT_ROWS))


def _proj_kernel(x_ref, wn_ref, wt_ref,
                 cos_ref, sa_ref, sb_ref, qd_ref, kd_ref, dmat_ref,
                 cw_ref, cb_ref, wa_ref, ba_ref, wx_ref, bx_ref, lam_ref,
                 yret_ref, ylru_ref, kcn_ref, vcn_ref, ks_ref, kw_ref, ng_ref,
                 qt_ref, vst_ref, vwt_ref, glt_ref,
                 state_ref, xext_ref, h_ref, *, chunk_decay):
    tm = x_ref.shape[0]

    @pl.when(pl.program_id(1) == 0)
    def _():
        state_ref[...] = jnp.zeros_like(state_ref)
        xext_ref[0:SUBLANES, :] = jnp.zeros((SUBLANES, LRU_WIDTH), F32)
        h_ref[...] = jnp.zeros_like(h_ref)

    xb = x_ref[...].astype(BF16)

    def nat(i):
        return jnp.dot(xb, wn_ref[:, _WN_OFFS[i]:_WN_OFFS[i + 1]], preferred_element_type=F32)

    def tr(i):
        return lax.dot_general(wt_ref[_WT_OFFS[i]:_WT_OFFS[i + 1], :], xb, _CONTRACT_LAST,
                               preferred_element_type=F32)

    ret = nat(0)
    lru = nat(4)
    for c0 in range(0, tm, RET_CHUNK):
        rows = slice(c0, c0 + RET_CHUNK)
        yret_ref[rows, :] = _retention_chunk(
            ret[rows, :], cos_ref[rows, :], sa_ref[rows, :], sb_ref[rows, :], qd_ref[...],
            kd_ref[...], dmat_ref, state_ref, chunk_decay).astype(yret_ref.dtype)
    for c0 in range(0, tm, LRU_TS):
        rows = slice(c0, c0 + LRU_TS)
        ylru_ref[rows, :] = _lru_tile(lru[rows, :], cw_ref, cb_ref, wa_ref, ba_ref, wx_ref, bx_ref,
                                      lam_ref, xext_ref, h_ref).astype(ylru_ref.dtype)

    kvc = nat(1)
    kcn_ref[...] = kvc[:, 0:NSA_KV_WIDTH]
    vcn_ref[...] = kvc[:, NSA_KV_WIDTH:2 * NSA_KV_WIDTH]
    ksw = nat(2)
    ks = ksw[:, 0:NSA_KV_WIDTH]
    key = pl.program_id(1) * tm + lax.broadcasted_iota(jnp.int32, (tm, 1), 0)
    blk = lax.broadcasted_iota(jnp.int32, (1, HEAD_DIM), 1)
    onehot = jnp.where(lax.shift_right_logical(key, SEL_SHIFT) == blk, 1.0, 0.0)
    for g in range(NSA_KV_HEADS):
        ks_ref[g] = jnp.concatenate([ks[:, g * HEAD_DIM:(g + 1) * HEAD_DIM], onehot],
                                    axis=1).astype(BF16)
    kw_ref[...] = ksw[:, NSA_KV_WIDTH:2 * NSA_KV_WIDTH].astype(BF16)
    ng_ref[...] = nat(3)
    qt_ref[...] = (tr(0) * LOG2E).astype(BF16)
    ones_rows = jnp.where(lax.broadcasted_iota(jnp.int32, (BF16_SUBLANES, tm), 0) == 0,
                          1.0, 0.0).astype(BF16)
    for ref, i in ((vst_ref, 1), (vwt_ref, 2)):
        vt = tr(i)
        for g in range(NSA_KV_HEADS):
            ref[g * V_ROWS:g * V_ROWS + HEAD_DIM, :] = (
                vt[g * HEAD_DIM:(g + 1) * HEAD_DIM, :].astype(BF16))
            ref[g * V_ROWS + HEAD_DIM:(g + 1) * V_ROWS, :] = ones_rows
    glt_ref[...] = tr(3)


def _project_and_mix(x, wn, wt, conv_w, conv_b, w_a, b_a, w_x, b_x, lam):
    B, S, _ = x.shape
    tm = PROJ_TM
    nn, nt = wn.shape[1], wt.shape[0]
    G, C, W, LW = NSA_KV_HEADS, RET_CHUNK, RET_WIDTH, LRU_WIDTH
    cos, sa, sb, qd, kd, dmat, chunk_decay = _ret_tables(S)
    row = lambda w: pl.BlockSpec((None, tm, w), lambda b, i: (b, i, 0))
    col = lambda r: pl.BlockSpec((None, r, tm), lambda b, i: (b, 0, i))
    kspec = pl.BlockSpec((None, G, tm, 2 * HEAD_DIM), lambda b, i: (b, 0, i, 0))
    fixed = lambda *shape: pl.BlockSpec(shape, lambda b, i: (0,) * len(shape))
    pos = pl.BlockSpec((tm, W), lambda b, i: (i, 0))
    vec = lambda a: a.reshape(1, LW)
    out_shape = (
        jax.ShapeDtypeStruct((B, S, W), BF16),
        jax.ShapeDtypeStruct((B, S, LW), BF16),
        jax.ShapeDtypeStruct((B, S, NSA_KV_WIDTH), F32),
        jax.ShapeDtypeStruct((B, S, NSA_KV_WIDTH), F32),
        jax.ShapeDtypeStruct((B, G, S, 2 * HEAD_DIM), BF16),
        jax.ShapeDtypeStruct((B, S, NSA_KV_WIDTH), BF16),
        jax.ShapeDtypeStruct((B, S, NSA_WIDTH), F32),
        jax.ShapeDtypeStruct((B, NSA_WIDTH, S), BF16),
        jax.ShapeDtypeStruct((B, G * V_ROWS, S), BF16),
        jax.ShapeDtypeStruct((B, G * V_ROWS, S), BF16),
        jax.ShapeDtypeStruct((B, G * GL_ROWS, S), F32),
    )
    return pl.pallas_call(
        functools.partial(_proj_kernel, chunk_decay=chunk_decay),
        grid=(B, S // tm),
        in_specs=[row(D_MODEL), fixed(D_MODEL, nn), fixed(nt, D_MODEL),
                  pos, pos, pos, fixed(C, W), fixed(C, W), fixed(RET_HEADS, C, C),
                  fixed(CONV_WIDTH, LW), fixed(1, LW), fixed(LW, LW), fixed(1, LW), fixed(LW, LW),
                  fixed(1, LW), fixed(1, LW)],
        out_specs=(row(W), row(LW), row(NSA_KV_WIDTH), row(NSA_KV_WIDTH), kspec,
                   row(NSA_KV_WIDTH), row(NSA_WIDTH), col(NSA_WIDTH), col(G * V_ROWS),
                   col(G * V_ROWS), col(G * GL_ROWS)),
        out_shape=out_shape,
        scratch_shapes=[pltpu.VMEM((RET_HEADS, HEAD_DIM, HEAD_DIM), F32),
                        pltpu.VMEM((LRU_TS + SUBLANES, LW), F32), pltpu.VMEM((SUBLANES, LW), F32)],
        compiler_params=_cparams(("parallel", "arbitrary")),
        name="in_proj_mix",
    )(x, wn, wt, jnp.asarray(cos), jnp.asarray(sa), jnp.asarray(sb), jnp.asarray(qd),
      jnp.asarray(kd), jnp.asarray(dmat), conv_w, vec(conv_b), _block_diag(w_a).astype(BF16),
      vec(b_a), _block_diag(w_x).astype(BF16), vec(b_x), vec(lam))


def _prep_in_weights(w_in):
    seg = [w_in[:, IN_OFFS[i]:IN_OFFS[i + 1]] for i in range(len(IN_SIZES))]
    (rq, rk, rv, rg, nq, nkc, nvc, nks, nvs, nkw, nvw, ng, ngl, lx, lg) = seg
    wn = jnp.concatenate([rq, rk, rv, rg, nkc, nvc, nks, nkw, ng, lx, lg], axis=1).astype(BF16)
    glt = ngl.T.reshape(NSA_KV_HEADS, NSA_GROUP * N_BRANCH, D_MODEL)
    glt = jnp.pad(glt, ((0, 0), (0, GL_ROWS - NSA_GROUP * N_BRANCH), (0, 0)))
    wt = jnp.concatenate([nq.T * (HEAD_DIM ** -0.5), nvs.T, nvw.T,
                          glt.reshape(NSA_KV_HEADS * GL_ROWS, D_MODEL)], axis=0).astype(BF16)
    return wn, wt


def _retention_chunk(ret, cos, sa, sb, qdec, kdec, dmat_ref, state_ref, chunk_decay):
    W = RET_WIDTH
    q = ret[:, 0:W]
    k = ret[:, W:2 * W]
    v = ret[:, 2 * W:3 * W]
    gate = ret[:, 3 * W:4 * W]
    half = HEAD_DIM // 2

    def rope(t):
        cols = []
        for c0 in range(0, W, LANES):
            tc = t[:, c0:c0 + LANES]
            cols.append(tc * cos[:, c0:c0 + LANES]
                        + pltpu.roll(tc, LANES - half, 1) * sa[:, c0:c0 + LANES]
                        + pltpu.roll(tc, half, 1) * sb[:, c0:c0 + LANES])
        return jnp.concatenate(cols, axis=1)

    qr = rope(q)
    kr = rope(k) * (HEAD_DIM ** -0.5)
    qd = qr * qdec
    kd = kr * kdec
    ys = []
    for h in range(RET_HEADS):
        sl = slice(h * HEAD_DIM, (h + 1) * HEAD_DIM)
        qb, kb, vb = qr[:, sl].astype(BF16), kr[:, sl].astype(BF16), v[:, sl].astype(BF16)
        s = lax.dot_general(qb, kb, _CONTRACT_LAST, preferred_element_type=F32) * dmat_ref[h]
        inner = jnp.dot(s.astype(BF16), vb, preferred_element_type=F32)
        st = state_ref[h]
        cross = jnp.dot(qd[:, sl].astype(BF16), st.astype(BF16), preferred_element_type=F32)
        y = inner + cross
        mu = jnp.mean(y, axis=-1, keepdims=True)
        yc = y - mu
        var = jnp.mean(yc * yc, axis=-1, keepdims=True)
        ys.append(yc * lax.rsqrt(var + LN_EPS))
        kv = lax.dot_general(kd[:, sl].astype(BF16), vb, _CONTRACT_FIRST,
                             preferred_element_type=F32)
        state_ref[h] = st * chunk_decay[h] + kv
    y = jnp.concatenate(ys, axis=1)
    return y * (gate * jax.nn.sigmoid(gate))


def _ret_tables(S):
    C, H, d = RET_CHUNK, RET_HEADS, HEAD_DIM
    half = d // 2
    inv = 1.0 / (ROPE_BASE ** (np.arange(half, dtype=np.float64) / half))
    ang = np.arange(S, dtype=np.float64)[:, None] * inv[None, :]
    cos_h = np.concatenate([np.cos(ang), np.cos(ang)], axis=1)
    sin_lo = np.concatenate([-np.sin(ang), np.zeros_like(ang)], axis=1)
    sin_hi = np.concatenate([np.zeros_like(ang), np.sin(ang)], axis=1)
    tile = lambda a: np.tile(a, (1, H)).astype(np.float32)
    log_g = np.log(1.0 - 2.0 ** (-5.0 - np.arange(H, dtype=np.float64)))
    idx = np.arange(C, dtype=np.float64)
    diff = idx[:, None] - idx[None, :]
    dmat = np.where(diff >= 0, np.exp(np.maximum(diff, 0.0)[None] * log_g[:, None, None]), 0.0)
    kd = np.exp((C - 1.0 - idx)[:, None] * log_g[None, :])
    qd = np.exp((idx + 1.0)[:, None] * log_g[None, :])
    rep = lambda a: np.repeat(a, d, axis=1).astype(np.float32)
    chunk_decay = tuple(float(v) for v in np.exp(C * log_g))
    return (tile(cos_h), tile(sin_lo), tile(sin_hi), rep(qd), rep(kd),
            dmat.astype(np.float32), chunk_decay)


_CMP_ROWS = 256


def _cmp_kernel(kcn_ref, vcn_ref, pos_ref, wl_ref, kc_ref, vct_ref):
    n, W = _CMP_ROWS, NSA_KV_WIDTH
    res = []
    for i, ref in enumerate((kcn_ref, vcn_ref)):
        out = jnp.zeros((n, 2 * W), F32)
        cst = jnp.zeros((SUBLANES, 2 * W), F32)
        for l in range(CMP_STRIDE):
            x = ref[pl.ds(l, n, stride=CMP_STRIDE), :].astype(BF16)
            out = out + jnp.dot(x, wl_ref[i, l], preferred_element_type=F32)
            cst = cst + jnp.dot(pos_ref[i, l].astype(BF16), wl_ref[i, l],
                                preferred_element_type=F32)
        const = cst[0:1, 0:W] + cst[1:2, W:2 * W]
        res.append(out[:, 0:W] + pltpu.roll(out[:, W:2 * W], n - 1, 0) + const)
    kc_ref[...] = res[0].astype(BF16)
    vct_ref[...] = res[1].T.astype(BF16)


def _prep_cmp_weights(cmp_pos, cmp_w):
    L2, d, G = CMP_STRIDE, HEAD_DIM, NSA_KV_HEADS
    assert G == 2
    w5 = cmp_w.reshape(2, 2, L2, d, d)
    z = jnp.zeros((2, L2, d, d), F32)

    def both_groups(w):
        return jnp.concatenate([jnp.concatenate([w, z], axis=-1),
                                jnp.concatenate([z, w], axis=-1)], axis=-2)

    wl = jnp.concatenate([both_groups(w5[:, 0]), both_groups(w5[:, 1])], axis=-1).astype(BF16)
    p4 = cmp_pos.reshape(2, 2, L2, d)
    prow = jnp.concatenate([p4, p4], axis=-1).transpose(0, 2, 1, 3)
    prow = jnp.pad(prow, ((0, 0), (0, 0), (0, SUBLANES - 2), (0, 0)))
    return prow, wl


def _compress(kcn, vcn, prow, wl):
    B, S, W = kcn.shape
    n = _CMP_ROWS
    return pl.pallas_call(
        _cmp_kernel,
        grid=(B,),
        in_specs=[pl.BlockSpec((None, S, W), lambda b: (b, 0, 0)),
                  pl.BlockSpec((None, S, W), lambda b: (b, 0, 0)),
                  pl.BlockSpec(prow.shape, lambda b: (0, 0, 0, 0)),
                  pl.BlockSpec(wl.shape, lambda b: (0, 0, 0, 0))],
        out_specs=(pl.BlockSpec((None, n, NSA_KV_WIDTH), lambda b: (b, 0, 0)),
                   pl.BlockSpec((None, NSA_KV_WIDTH, n), lambda b: (b, 0, 0))),
        out_shape=(jax.ShapeDtypeStruct((B, n, NSA_KV_WIDTH), BF16),
                   jax.ShapeDtypeStruct((B, NSA_KV_WIDTH, n), BF16)),
        compiler_params=_cparams(("parallel",)),
        name="nsa_compress",
    )(kcn, vcn, prow, wl)


def _nsa_kernel(qt_ref, kc_ref, vct_ref, ks_ref, vst_ref, kw_ref, vwt_ref, glt_ref, ng_ref,
                selmap_ref, band_ref, cmpmask_ref, out_ref, val_ref, cnt_ref, s_ref, p_ref, acc_ref,
                st_ref):
    TQ, KB, R, d = NSA_TQ, NSA_KB, NSA_GROUP, HEAD_DIM
    n_sel = selmap_ref.shape[0]
    g = pl.program_id(1)
    qi = pl.program_id(2)
    t0 = qi * TQ
    heads = [slice(r * TQ, (r + 1) * TQ) for r in range(R)]
    qs = [qt_ref[r * d:(r + 1) * d, :] for r in range(R)]
    zq = jnp.zeros_like(qs[0])
    qs_nat = [jnp.concatenate([jnp.where(g == gg, q, zq) for gg in range(NSA_KV_HEADS)], axis=0)
              for q in qs]

    def key_off(kb):
        return pl.multiple_of(kb * KB, KB)

    n_wb = WINDOW // KB + 1
    w_kb = [qi - (n_wb - 1) + w for w in range(n_wb)]
    w_off = [key_off(jnp.maximum(kb, 0)) for kb in w_kb]
    w_bias = [band_ref[1]] + [None] * (n_wb - 2) + [band_ref[0]]
    qs_nat_all = jnp.concatenate(qs_nat, axis=1)

    n_cmp = kc_ref.shape[0]
    cmp_off = pl.multiple_of(n_cmp - qi * (TQ // CMP_STRIDE), TQ // CMP_STRIDE)
    cmp_bias = jnp.concatenate([cmpmask_ref[pl.ds(cmp_off, n_cmp), :]] * R, axis=1)
    tq = t0 + lax.broadcasted_iota(jnp.int32, (1, TQ), 1)
    seen = jnp.concatenate([tq >= CMP_LEN - 1] * R, axis=1)
    s = jnp.dot(kc_ref[...], qs_nat_all, preferred_element_type=F32) + cmp_bias

    w_s = []
    for w in range(n_wb):
        s_w = jnp.dot(kw_ref[pl.ds(w_off[w], KB), :], qs_nat_all, preferred_element_type=F32)
        if w_bias[w] is not None:
            s_w = s_w + jnp.concatenate([w_bias[w]] * R, axis=1)
        if w < n_wb - 1:
            s_w = s_w + jnp.where(w_kb[w] >= 0, 0.0, NEG)
        w_s.append(s_w)

    p = jnp.exp2(s - jnp.max(s, axis=0, keepdims=True))
    p = p * jnp.where(seen, 1.0 / jnp.sum(p, axis=0, keepdims=True), 0.0)
    o_c_all = jnp.dot(vct_ref[...], p.astype(BF16), preferred_element_type=F32)
    o_c = [o_c_all[:, hs] for hs in heads]
    psum = functools.reduce(jnp.add, [p[:, hs] for hs in heads])

    p_hi = psum.astype(BF16)
    p_lo = (psum - p_hi.astype(F32)).astype(BF16)
    imp = (jnp.dot(selmap_ref[...], p_hi, preferred_element_type=F32)
           + jnp.dot(selmap_ref[...], p_lo, preferred_element_type=F32))

    jrow = lax.broadcasted_iota(jnp.int32, (n_sel, 1), 0)
    cur = lax.shift_right_logical(tq, SEL_SHIFT)
    forced = (jrow == 0) | (jrow == cur) | (jrow == cur - 1)
    causal = jrow * SEL_LEN <= tq
    val_ref[...] = jnp.where(forced, FORCE_SCORE, jnp.where(causal, imp, -FORCE_SCORE))
    cnt_ref[...] = jnp.zeros_like(cnt_ref)
    n_live = (qi + 1) * (TQ // SEL_LEN)
    n_grp = n_sel // SUBLANES
    for c in range(n_grp):

        @pl.when(c * SUBLANES < n_live)
        def _():
            groups = [slice(k * SUBLANES, (k + 1) * SUBLANES) for k in range(n_grp)]
            vals = [val_ref[rows, :] for rows in groups]
            cnts = [cnt_ref[rows, :] for rows in groups]
            own = lax.broadcasted_iota(jnp.int32, (SUBLANES, 1), 0) + c * SUBLANES
            for jp in range(c * SUBLANES, (c + 1) * SUBLANES):
                cand = jnp.broadcast_to(val_ref[jp:jp + 1, :], (SUBLANES, TQ))
                for k in range(n_grp):
                    if k < c:
                        hit = jnp.where(cand > vals[k], 1.0, 0.0)
                    elif k > c:
                        hit = jnp.where(cand >= vals[k], 1.0, 0.0)
                    else:
                        hit = jnp.where(own > jp, jnp.where(cand >= vals[k], 1.0, 0.0),
                                        jnp.where(cand > vals[k], 1.0, 0.0))
                    cnts[k] = cnts[k] + hit
            for rows, cnt in zip(groups, cnts):
                cnt_ref[rows, :] = cnt

    bias = jnp.where(cnt_ref[...] < float(N_SELECT), 0.0, NEG).astype(BF16)
    qs_sel = [jnp.concatenate([q, bias], axis=0) for q in qs]

    def sel_scores(kb, r):
        return jnp.dot(ks_ref[pl.ds(key_off(kb), KB), :], qs_sel[r], preferred_element_type=F32)

    def sel_values(kb, p):
        return jnp.dot(vst_ref[:, pl.ds(key_off(kb), KB)], p, preferred_element_type=F32)

    ROW_M, ROW_A = 0, 1

    def sel_stage(i, cur_slot, *, diag, mxu_first=False):
        nxt = 1 - cur_slot
        for r, hs in enumerate(heads):
            if mxu_first and r == 0:
                pvs = [sel_values(jnp.maximum(i - 1, 0), p_ref[nxt, :, h2]) for h2 in heads]
                for r2, h2 in enumerate(heads):
                    s_ref[nxt, :, h2] = sel_scores(i + 1, r2)
            s = s_ref[cur_slot, :, hs]
            if diag:
                s = s + band_ref[0]
            m = st_ref[ROW_M:ROW_M + 1, hs]
            m_new = jnp.maximum(m, jnp.max(s, axis=0, keepdims=True))
            p_ref[cur_slot, :, hs] = jnp.exp2(s - m_new).astype(BF16)
            if mxu_first:
                pv_prev = pvs[r]
            else:
                pv_prev = sel_values(jnp.maximum(i - 1, 0), p_ref[nxt, :, hs])
                if not diag:
                    s_ref[nxt, :, hs] = sel_scores(i + 1, r)
            acc_ref[:, hs] = st_ref[ROW_A:ROW_A + 1, hs] * acc_ref[:, hs] + pv_prev
            st_ref[ROW_M:ROW_M + 1, hs] = m_new
            st_ref[ROW_A:ROW_A + 1, hs] = jnp.exp2(m - m_new)

    def sel_finish(i, cur_slot):
        for hs in heads:
            acc_ref[:, hs] = (st_ref[ROW_A:ROW_A + 1, hs] * acc_ref[:, hs]
                              + sel_values(i, p_ref[cur_slot, :, hs]))

    for r, hs in enumerate(heads):
        s_ref[0, :, hs] = sel_scores(0, r)
    p_ref[1] = jnp.zeros(p_ref.shape[1:], BF16)
    acc_ref[...] = jnp.zeros_like(acc_ref)
    st_ref[ROW_M:ROW_M + 1, :] = jnp.full((1, R * TQ), NEG, F32)
    st_ref[ROW_A:ROW_A + 1, :] = jnp.ones((1, R * TQ), F32)

    m_w = functools.reduce(jnp.maximum, [jnp.max(s_w, axis=0, keepdims=True) for s_w in w_s])
    acc_w_all = functools.reduce(jnp.add, [
        jnp.dot(vwt_ref[:, pl.ds(w_off[w], KB)], jnp.exp2(w_s[w] - m_w).astype(BF16),
                preferred_element_type=F32) for w in range(n_wb)])
    acc_w = [acc_w_all[:, hs] for hs in heads]

    def sel_pair(j, c):
        sel_stage(2 * j, 0, diag=False)
        sel_stage(2 * j + 1, 1, diag=False, mxu_first=True)
        return c

    lax.fori_loop(0, lax.shift_right_logical(qi, 1), sel_pair, 0)
    odd = lax.rem(qi, 2) == 1

    @pl.when(odd)
    def _():
        sel_stage(qi - 1, 0, diag=False)
        sel_stage(qi, 1, diag=True)
        sel_finish(qi, 1)

    @pl.when(jnp.logical_not(odd))
    def _():
        sel_stage(qi, 0, diag=True)
        sel_finish(qi, 0)

    gl = jax.nn.sigmoid(glt_ref[...])
    res = []
    for r, hs in enumerate(heads):
        a_s = acc_ref[:, hs]
        a_w = acc_w[r]
        gate = [gl[r * N_BRANCH + br:r * N_BRANCH + br + 1, :] for br in range(N_BRANCH)]
        res.append(gate[0] * o_c[r]
                   + (gate[1] / a_s[d:d + 1, :]) * a_s[0:d, :]
                   + (gate[2] / a_w[d:d + 1, :]) * a_w[0:d, :])
    ng = ng_ref[...]
    out_ref[...] = (jnp.concatenate(res, axis=0).T
                    * (ng * jax.nn.sigmoid(ng))).astype(out_ref.dtype)


def _nsa_tables(S):
    n_cmp = (S - CMP_LEN) // CMP_STRIDE + 1
    n_sel = S // SEL_LEN
    ci = np.arange(n_cmp)[:, None]
    sj = np.arange(n_sel)[None, :]
    overlap = (np.minimum(ci * CMP_STRIDE + CMP_LEN, sj * SEL_LEN + SEL_LEN)
               - np.maximum(ci * CMP_STRIDE, sj * SEL_LEN))
    sel_map = np.clip(overlap, 0, None) / CMP_STRIDE
    selmap_t = np.zeros((n_sel, _CMP_ROWS), np.float32)
    selmap_t[:, :n_cmp] = sel_map.T
    jl = np.arange(NSA_KB)[:, None]
    tl = np.arange(NSA_TQ)[None, :]
    band = np.stack([np.where(jl <= tl, 0.0, NEG), np.where(jl > tl, 0.0, NEG)]).astype(np.float32)
    rel = (np.arange(2 * _CMP_ROWS)[:, None] - _CMP_ROWS) * CMP_STRIDE + (CMP_LEN - 1)
    cmpmask = np.where(rel <= tl, 0.0, NEG).astype(np.float32)
    return selmap_t, band, cmpmask


def _nsa_attention(qt, kc, vct, ks, vst, kw, vwt, glt, ng):
    B, _, S = qt.shape
    G, R, d, TQ = NSA_KV_HEADS, NSA_GROUP, HEAD_DIM, NSA_TQ
    selmap_t, band, cmpmask = _nsa_tables(S)
    n_sel = S // SEL_LEN
    assert NSA_KB == TQ and WINDOW % NSA_KB == 0
    assert n_sel == d
    per_b_rows = lambda n: pl.BlockSpec((None, n, G * d), lambda b, g, q: (b, 0, 0))
    per_bg_cols = lambda rows, n: pl.BlockSpec((None, rows, n), lambda b, g, q: (b, g, 0))
    return pl.pallas_call(
        _nsa_kernel,
        grid=(B, G, S // TQ),
        in_specs=[pl.BlockSpec((None, R * d, TQ), lambda b, g, q: (b, g, q)),
                  per_b_rows(_CMP_ROWS), per_bg_cols(d, _CMP_ROWS),
                  pl.BlockSpec((None, None, S, 2 * d), lambda b, g, q: (b, g, 0, 0)),
                  per_bg_cols(V_ROWS, S), per_b_rows(S), per_bg_cols(V_ROWS, S),
                  pl.BlockSpec((None, GL_ROWS, TQ), lambda b, g, q: (b, g, q)),
                  pl.BlockSpec((None, TQ, R * d), lambda b, g, q: (b, q, g)),
                  pl.BlockSpec(selmap_t.shape, lambda b, g, q: (0, 0)),
                  pl.BlockSpec(band.shape, lambda b, g, q: (0, 0, 0)),
                  pl.BlockSpec(cmpmask.shape, lambda b, g, q: (0, 0))],
        out_specs=pl.BlockSpec((None, TQ, R * d), lambda b, g, q: (b, q, g)),
        out_shape=jax.ShapeDtypeStruct((B, S, NSA_WIDTH), BF16),
        scratch_shapes=[pltpu.VMEM((n_sel, TQ), F32), pltpu.VMEM((n_sel, TQ), F32),
                        pltpu.VMEM((2, NSA_KB, R * TQ), F32),
                        pltpu.VMEM((2, NSA_KB, R * TQ), BF16),
                        pltpu.VMEM((V_ROWS, R * TQ), F32),
                        pltpu.VMEM((SUBLANES, R * TQ), F32)],
        compiler_params=_cparams(("parallel", "parallel", "arbitrary")),
        name="nsa_attention",
    )(qt, kc, vct, ks, vst, kw, vwt, glt, ng, jnp.asarray(selmap_t, dtype=BF16), jnp.asarray(band),
      jnp.asarray(cmpmask))


def _nsa_pair_kernel(qt_ref, kc_ref, vct_ref, ks_ref, vst_ref, kw_ref, vwt_ref, glt_ref, ng_ref,
                     selmap_ref, band_ref, cmpmask_ref, x_ref, yret_ref, ylru_ref, wo_ref, lng_ref,
                     lnb_ref, out_ref, val_ref, cnt_ref, s_ref, p_ref, acc_ref, st_ref, part_ref):
    TQ, KB, R, d, G = NSA_TQ, NSA_KB, NSA_GROUP, HEAD_DIM, NSA_KV_HEADS
    n_sel = selmap_ref.shape[0]
    qi = pl.program_id(1)
    t0 = qi * TQ
    o1, o2 = RET_WIDTH, RET_WIDTH + NSA_WIDTH
    part_ref[...] = (DEEPNORM_ALPHA * x_ref[...]
                     + jnp.dot(yret_ref[...], wo_ref[0:o1, :], preferred_element_type=F32)
                     + jnp.dot(ylru_ref[...], wo_ref[o2:D_MODEL, :], preferred_element_type=F32))
    groups = range(G)
    heads = [slice(r * TQ, (r + 1) * TQ) for r in range(R)]
    qs = [[qt_ref[(g * R + r) * d:(g * R + r + 1) * d, :] for r in range(R)] for g in groups]
    zq = jnp.zeros((d, TQ), BF16)
    qs_nat = [jnp.concatenate(
        [jnp.concatenate([q if gg == g else zq for gg in groups], axis=0) for q in qs[g]], axis=1)
        for g in groups]

    def key_off(kb):
        return pl.multiple_of(kb * KB, KB)

    n_wb = WINDOW // KB + 1
    w_kb = [qi - (n_wb - 1) + w for w in range(n_wb)]
    w_off = [key_off(jnp.maximum(kb, 0)) for kb in w_kb]
    w_bias = [band_ref[1]] + [None] * (n_wb - 2) + [band_ref[0]]
    n_cmp = kc_ref.shape[0]
    cmp_off = pl.multiple_of(n_cmp - qi * (TQ // CMP_STRIDE), TQ // CMP_STRIDE)
    cmp_bias = jnp.concatenate([cmpmask_ref[pl.ds(cmp_off, n_cmp), :]] * R, axis=1)
    tq = t0 + lax.broadcasted_iota(jnp.int32, (1, TQ), 1)
    seen = jnp.concatenate([tq >= CMP_LEN - 1] * R, axis=1)

    s_cmp = [jnp.dot(kc_ref[...], qs_nat[g], preferred_element_type=F32) + cmp_bias
             for g in groups]
    w_s = []
    for g in groups:
        blocks = []
        for w in range(n_wb):
            s_w = jnp.dot(kw_ref[pl.ds(w_off[w], KB), :], qs_nat[g], preferred_element_type=F32)
            if w_bias[w] is not None:
                s_w = s_w + jnp.concatenate([w_bias[w]] * R, axis=1)
            if w < n_wb - 1:
                s_w = s_w + jnp.where(w_kb[w] >= 0, 0.0, NEG)
            blocks.append(s_w)
        w_s.append(blocks)

    jrow = lax.broadcasted_iota(jnp.int32, (n_sel, 1), 0)
    cur = lax.shift_right_logical(tq, SEL_SHIFT)
    forced = (jrow == 0) | (jrow == cur) | (jrow == cur - 1)
    causal = jrow * SEL_LEN <= tq
    o_c = []
    for g in groups:
        p = jnp.exp2(s_cmp[g] - jnp.max(s_cmp[g], axis=0, keepdims=True))
        p = p * jnp.where(seen, 1.0 / jnp.sum(p, axis=0, keepdims=True), 0.0)
        o_c.append(jnp.dot(vct_ref[g * d:(g + 1) * d, :], p.astype(BF16),
                           preferred_element_type=F32))
        psum = functools.reduce(jnp.add, [p[:, hs] for hs in heads])
        p_hi = psum.astype(BF16)
        p_lo = (psum - p_hi.astype(F32)).astype(BF16)
        imp = (jnp.dot(selmap_ref[...], p_hi, preferred_element_type=F32)
               + jnp.dot(selmap_ref[...], p_lo, preferred_element_type=F32))
        val_ref[g] = jnp.where(forced, FORCE_SCORE, jnp.where(causal, imp, -FORCE_SCORE))
    cnt_ref[...] = jnp.zeros_like(cnt_ref)

    n_live = (qi + 1) * (TQ // SEL_LEN)
    n_grp = n_sel // SUBLANES
    row_groups = [slice(k * SUBLANES, (k + 1) * SUBLANES) for k in range(n_grp)]
    for c in range(n_grp):

        @pl.when(c * SUBLANES < n_live)
        def _():
            own = lax.broadcasted_iota(jnp.int32, (SUBLANES, 1), 0) + c * SUBLANES
            for g in groups:
                vals = [val_ref[g, rows, :] for rows in row_groups]
                cnts = [cnt_ref[g, rows, :] for rows in row_groups]
                for jp in range(c * SUBLANES, (c + 1) * SUBLANES):
                    cand = jnp.broadcast_to(val_ref[g, jp:jp + 1, :], (SUBLANES, TQ))
                    for k in range(n_grp):
                        if k < c:
                            hit = jnp.where(cand > vals[k], 1.0, 0.0)
                        elif k > c:
                            hit = jnp.where(cand >= vals[k], 1.0, 0.0)
                        else:
                            hit = jnp.where(own > jp, jnp.where(cand >= vals[k], 1.0, 0.0),
                                            jnp.where(cand > vals[k], 1.0, 0.0))
                        cnts[k] = cnts[k] + hit
                for rows, cnt in zip(row_groups, cnts):
                    cnt_ref[g, rows, :] = cnt

    qs_sel = []
    for g in groups:
        bias = jnp.where(cnt_ref[g] < float(N_SELECT), 0.0, NEG).astype(BF16)
        qs_sel.append([jnp.concatenate([q, bias], axis=0) for q in qs[g]])

    def sel_scores(g, kb, r):
        return jnp.dot(ks_ref[g, pl.ds(key_off(kb), KB), :], qs_sel[g][r],
                       preferred_element_type=F32)

    def sel_values(g, kb, p):
        return jnp.dot(vst_ref[g * V_ROWS:(g + 1) * V_ROWS, pl.ds(key_off(kb), KB)], p,
                       preferred_element_type=F32)

    ROW_M, ROW_A = 0, 1

    def sel_stage(g, i, cur_slot, *, diag, mxu_first=False):
        nxt = 1 - cur_slot
        for r, hs in enumerate(heads):
            if mxu_first and r == 0:
                pvs = [sel_values(g, jnp.maximum(i - 1, 0), p_ref[g, nxt, :, h2]) for h2 in heads]
                for r2, h2 in enumerate(heads):
                    s_ref[g, nxt, :, h2] = sel_scores(g, i + 1, r2)
            s = s_ref[g, cur_slot, :, hs]
            if diag:
                s = s + band_ref[0]
            m = st_ref[g, ROW_M:ROW_M + 1, hs]
            m_new = jnp.maximum(m, jnp.max(s, axis=0, keepdims=True))
            p_ref[g, cur_slot, :, hs] = jnp.exp2(s - m_new).astype(BF16)
            if mxu_first:
                pv_prev = pvs[r]
            else:
                pv_prev = sel_values(g, jnp.maximum(i - 1, 0), p_ref[g, nxt, :, hs])
                if not diag:
                    s_ref[g, nxt, :, hs] = sel_scores(g, i + 1, r)
            acc_ref[g, :, hs] = st_ref[g, ROW_A:ROW_A + 1, hs] * acc_ref[g, :, hs] + pv_prev
            st_ref[g, ROW_M:ROW_M + 1, hs] = m_new
            st_ref[g, ROW_A:ROW_A + 1, hs] = jnp.exp2(m - m_new)

    def sel_finish(g, i, cur_slot):
        for hs in heads:
            acc_ref[g, :, hs] = (st_ref[g, ROW_A:ROW_A + 1, hs] * acc_ref[g, :, hs]
                                 + sel_values(g, i, p_ref[g, cur_slot, :, hs]))

    for g in groups:
        for r, hs in enumerate(heads):
            s_ref[g, 0, :, hs] = sel_scores(g, 0, r)
        p_ref[g, 1] = jnp.zeros(p_ref.shape[2:], BF16)
        acc_ref[g] = jnp.zeros(acc_ref.shape[1:], F32)
        st_ref[g, ROW_M:ROW_M + 1, :] = jnp.full((1, R * TQ), NEG, F32)
        st_ref[g, ROW_A:ROW_A + 1, :] = jnp.ones((1, R * TQ), F32)

    acc_w = []
    for g in groups:
        m_w = functools.reduce(jnp.maximum,
                               [jnp.max(s_w, axis=0, keepdims=True) for s_w in w_s[g]])
        acc_w.append(functools.reduce(jnp.add, [
            jnp.dot(vwt_ref[g * V_ROWS:(g + 1) * V_ROWS, pl.ds(w_off[w], KB)],
                    jnp.exp2(w_s[g][w] - m_w).astype(BF16), preferred_element_type=F32)
            for w in range(n_wb)]))

    def sel_pair(j, c):
        for g in groups:
            sel_stage(g, 2 * j, 0, diag=False)
        for g in groups:
            sel_stage(g, 2 * j + 1, 1, diag=False, mxu_first=True)
        return c

    lax.fori_loop(0, lax.shift_right_logical(qi, 1), sel_pair, 0)
    odd = lax.rem(qi, 2) == 1

    @pl.when(odd)
    def _():
        for g in groups:
            sel_stage(g, qi - 1, 0, diag=False)
        for g in groups:
            sel_stage(g, qi, 1, diag=True)
        for g in groups:
            sel_finish(g, qi, 1)

    @pl.when(jnp.logical_not(odd))
    def _():
        for g in groups:
            sel_stage(g, qi, 0, diag=True)
        for g in groups:
            sel_finish(g, qi, 0)

    gl = jax.nn.sigmoid(glt_ref[...])
    y_nsa = []
    for g in groups:
        res = []
        for r, hs in enumerate(heads):
            a_s = acc_ref[g, :, hs]
            a_w = acc_w[g][:, hs]
            row0 = g * GL_ROWS + r * N_BRANCH
            gate = [gl[row0 + br:row0 + br + 1, :] for br in range(N_BRANCH)]
            res.append(gate[0] * o_c[g][:, hs]
                       + (gate[1] / a_s[d:d + 1, :]) * a_s[0:d, :]
                       + (gate[2] / a_w[d:d + 1, :]) * a_w[0:d, :])
        ng = ng_ref[:, g * R * d:(g + 1) * R * d]
        y_nsa.append((jnp.concatenate(res, axis=0).T * (ng * jax.nn.sigmoid(ng))).astype(BF16))

    z = part_ref[...] + jnp.dot(jnp.concatenate(y_nsa, axis=1), wo_ref[o1:o2, :],
                                preferred_element_type=F32)
    mu = jnp.mean(z, axis=-1, keepdims=True)
    zc = z - mu
    var = jnp.mean(zc * zc, axis=-1, keepdims=True)
    out_ref[...] = zc * lax.rsqrt(var + LN_EPS) * lng_ref[...] + lnb_ref[...]


def _nsa_attention_pair(qt, kc, vct, ks, vst, kw, vwt, glt, ng, x, y_ret, y_lru, w_out, ln_g, ln_b):
    B, _, S = qt.shape
    G, R, d, TQ = NSA_KV_HEADS, NSA_GROUP, HEAD_DIM, NSA_TQ
    selmap_t, band, cmpmask = _nsa_tables(S)
    n_sel = S // SEL_LEN
    assert NSA_KB == TQ and WINDOW % NSA_KB == 0
    assert n_sel == d
    per_b = lambda *shape: pl.BlockSpec((None,) + shape, lambda b, q: (b,) + (0,) * len(shape))
    fixed = lambda a: pl.BlockSpec(a.shape, lambda b, q: (0,) * a.ndim)
    rows = lambda w: pl.BlockSpec((None, TQ, w), lambda b, q: (b, q, 0))
    return pl.pallas_call(
        _nsa_pair_kernel,
        grid=(B, S // TQ),
        in_specs=[pl.BlockSpec((None, G * R * d, TQ), lambda b, q: (b, 0, q)),
                  per_b(_CMP_ROWS, G * d), per_b(G * d, _CMP_ROWS),
                  per_b(G, S, 2 * d), per_b(G * V_ROWS, S), per_b(S, G * d), per_b(G * V_ROWS, S),
                  pl.BlockSpec((None, G * GL_ROWS, TQ), lambda b, q: (b, 0, q)),
                  rows(G * R * d), fixed(selmap_t), fixed(band), fixed(cmpmask),
                  rows(D_MODEL), rows(RET_WIDTH), rows(LRU_WIDTH),
                  pl.BlockSpec((D_MODEL, D_MODEL), lambda b, q: (0, 0)),
                  pl.BlockSpec((1, D_MODEL), lambda b, q: (0, 0)),
                  pl.BlockSpec((1, D_MODEL), lambda b, q: (0, 0))],
        out_specs=rows(D_MODEL),
        out_shape=jax.ShapeDtypeStruct((B, S, D_MODEL), F32),
        scratch_shapes=[pltpu.VMEM((G, n_sel, TQ), F32), pltpu.VMEM((G, n_sel, TQ), F32),
                        pltpu.VMEM((G, 2, NSA_KB, R * TQ), F32),
                        pltpu.VMEM((G, 2, NSA_KB, R * TQ), BF16),
                        pltpu.VMEM((G, V_ROWS, R * TQ), F32),
                        pltpu.VMEM((G, SUBLANES, R * TQ), F32),
                        pltpu.VMEM((TQ, D_MODEL), F32)],
        compiler_params=_cparams(("parallel", "arbitrary")),
        name="nsa_attention_out",
    )(qt, kc, vct, ks, vst, kw, vwt, glt, ng, jnp.asarray(selmap_t, dtype=BF16), jnp.asarray(band),
      jnp.asarray(cmpmask), x, y_ret, y_lru, w_out.astype(BF16), ln_g.reshape(1, D_MODEL),
      ln_b.reshape(1, D_MODEL))


def _lru_tile(lru, cw_ref, cb_ref, wa_ref, ba_ref, wx_ref, bx_ref, lam_ref, xext_ref, h_ref):
    ts, W, pad = LRU_TS, LRU_WIDTH, SUBLANES
    x = lru[:, 0:W]
    gate = lru[:, W:2 * W]
    xext_ref[pad:pad + ts, :] = x
    first = pad - (CONV_WIDTH - 1)
    xc = cb_ref[...] + cw_ref[0:1, :] * xext_ref[first:first + ts, :]
    for w in range(1, CONV_WIDTH):
        xc = xc + cw_ref[w:w + 1, :] * xext_ref[first + w:first + w + ts, :]
    xext_ref[0:pad, :] = x[ts - pad:ts, :]
    xcb = xc.astype(BF16)
    r = jax.nn.sigmoid(jnp.dot(xcb, wa_ref[...], preferred_element_type=F32) + ba_ref[...])
    gi = jax.nn.sigmoid(jnp.dot(xcb, wx_ref[...], preferred_element_type=F32) + bx_ref[...])
    nl = -lam_ref[...]
    softplus = jnp.maximum(nl, 0.0) + jnp.log1p(jnp.exp(-jnp.abs(nl)))
    log_a = (-LRU_C) * r * softplus
    a = jnp.exp(log_a)
    u = jnp.sqrt(-jnp.tanh(log_a) * (a * a + 1.0)) * (gi * xc)
    row = lax.broadcasted_iota(jnp.int32, (ts, 1), 0)
    step = 1
    while step < ts:
        if step < SUBLANES:
            keep = row >= step
            a_sh = jnp.where(keep, pltpu.roll(a, step, 0), 1.0)
            u_sh = jnp.where(keep, pltpu.roll(u, step, 0), 0.0)
        else:
            a_sh = jnp.concatenate([jnp.ones((step, W), F32), a[:ts - step, :]], axis=0)
            u_sh = jnp.concatenate([jnp.zeros((step, W), F32), u[:ts - step, :]], axis=0)
        u = a * u_sh + u
        a = a * a_sh
        step *= 2
    h = u + a * h_ref[0:1, :]
    h_ref[0:1, :] = h[ts - 1:ts, :]
    return h * (gate * jax.nn.sigmoid(gate))


def _block_diag(w):
    n, d, _ = w.shape
    z = jnp.zeros((d, d), w.dtype)
    return jnp.concatenate(
        [jnp.concatenate([w[i] if j == i else z for j in range(n)], axis=1) for i in range(n)],
        axis=0)


def _out_kernel(x_ref, yr_ref, yn_ref, yl_ref, wo_ref, g_ref, b_ref, out_ref):
    o1, o2 = RET_WIDTH, RET_WIDTH + NSA_WIDTH
    y = jnp.dot(yr_ref[...].astype(BF16), wo_ref[0:o1, :], preferred_element_type=F32)
    y = y + jnp.dot(yn_ref[...].astype(BF16), wo_ref[o1:o2, :], preferred_element_type=F32)
    y = y + jnp.dot(yl_ref[...].astype(BF16), wo_ref[o2:D_MODEL, :], preferred_element_type=F32)
    z = DEEPNORM_ALPHA * x_ref[...] + y
    mu = jnp.mean(z, axis=-1, keepdims=True)
    zc = z - mu
    var = jnp.mean(zc * zc, axis=-1, keepdims=True)
    out_ref[...] = zc * lax.rsqrt(var + LN_EPS) * g_ref[...] + b_ref[...]


def _out_proj(x, y_ret, y_nsa, y_lru, w_out, ln_g, ln_b):
    B, S, _ = x.shape
    tm = OUT_TM
    row = lambda w: pl.BlockSpec((None, tm, w), lambda b, i: (b, i, 0))
    fixed = lambda r: pl.BlockSpec((r, D_MODEL), lambda b, i: (0, 0))
    return pl.pallas_call(
        _out_kernel,
        grid=(B, S // tm),
        in_specs=[row(D_MODEL), row(RET_WIDTH), row(NSA_WIDTH), row(LRU_WIDTH),
                  fixed(D_MODEL), fixed(1), fixed(1)],
        out_specs=row(D_MODEL),
        out_shape=jax.ShapeDtypeStruct((B, S, D_MODEL), F32),
        compiler_params=_cparams(("parallel", "parallel")),
        name="out_proj_ln",
    )(x, y_ret, y_nsa, y_lru, w_out.astype(BF16), ln_g.reshape(1, D_MODEL),
      ln_b.reshape(1, D_MODEL))


def _layer(x, w_in, w_out, ln_g, ln_b, cmp_pos, cmp_w, conv_w, conv_b, w_a, b_a, w_x, b_x, lam):
    wn, wt = _prep_in_weights(w_in)
    y_ret, y_lru, kcn, vcn, ks, kw, ng, qt, vst, vwt, glt = _project_and_mix(
        x, wn, wt, conv_w, conv_b, w_a, b_a, w_x, b_x, lam)
    prow, wl = _prep_cmp_weights(cmp_pos, cmp_w)
    kc, vct = _compress(kcn, vcn, prow, wl)
    return _nsa_attention_pair(qt, kc, vct, ks, vst, kw, vwt, glt, ng, x, y_ret, y_lru, w_out,
                               ln_g, ln_b)


def kernel(x, w_in, w_out, ln_g, ln_b, nsa_cmp_pos, nsa_cmp_w, lru_conv_w, lru_conv_b,
           lru_w_a, lru_b_a, lru_w_x, lru_b_x, lru_lambda):
    assert x.shape[1] % max(PROJ_TM, RET_CHUNK, LRU_TS, OUT_TM, NSA_TQ) == 0
    assert x.shape[1] // CMP_STRIDE == _CMP_ROWS
    for l in range(DEPTH):
        x = _layer(x, w_in[l], w_out[l], ln_g[l], ln_b[l], nsa_cmp_pos[l], nsa_cmp_w[l],
                   lru_conv_w[l], lru_conv_b[l], lru_w_a[l], lru_b_a[l], lru_w_x[l], lru_b_x[l],
                   lru_lambda[l])
    return x
```

```python
import functools
import math

import jax
import jax.numpy as jnp
import numpy as np
from jax import lax
from jax.experimental import pallas as pl
from jax.experimental.pallas import tpu as pltpu

F32 = jnp.float32
BF16 = jnp.bfloat16

D_MODEL = 1024
DEPTH = 2
HEAD_DIM = 64
RET_HEADS = 4
RET_WIDTH = RET_HEADS * HEAD_DIM
ROPE_BASE = 10000.0
NSA_HEADS = 8
NSA_KV_HEADS = 2
NSA_GROUP = NSA_HEADS // NSA_KV_HEADS
NSA_WIDTH = NSA_HEADS * HEAD_DIM
NSA_KV_WIDTH = NSA_KV_HEADS * HEAD_DIM
CMP_LEN = 32
CMP_STRIDE = 16
SEL_LEN = 64
N_SELECT = 16
WINDOW = 512
N_BRANCH = 3
FORCE_SCORE = 1.0e4
LRU_WIDTH = 256
LRU_BLOCKS = 4
LRU_BLOCK_DIM = LRU_WIDTH // LRU_BLOCKS
CONV_WIDTH = 4
LRU_C = 8.0
DEEPNORM_ALPHA = (2.0 * DEPTH) ** 0.25
LN_EPS = 1e-5
NEG = -1e30
LOG2E = math.log2(math.e)
SEL_SHIFT = int(math.log2(SEL_LEN))

IN_SIZES = (RET_WIDTH,) * 4 + (NSA_WIDTH,) + (NSA_KV_WIDTH,) * 6 + (
    NSA_WIDTH, NSA_HEADS * N_BRANCH, LRU_WIDTH, LRU_WIDTH)
IN_OFFS = tuple(int(v) for v in np.cumsum((0,) + IN_SIZES))

V7X_VMEM_LIMIT = 48 * 1024 * 1024
SUBLANES = 8
LANES = 128
BF16_SUBLANES = 16
PROJ_TM = 512
RET_CHUNK = 256
LRU_TS = 256
NSA_TQ = 256
NSA_KB = 256
GL_ROWS = 16
V_ROWS = HEAD_DIM + BF16_SUBLANES

_CONTRACT_LAST = (((1,), (1,)), ((), ()))
_CONTRACT_FIRST = (((0,), (0,)), ((), ()))


def _cparams(sem):
    return pltpu.CompilerParams(dimension_semantics=sem, vmem_limit_bytes=V7X_VMEM_LIMIT)


_WN_COLS = (4 * RET_WIDTH, 2 * NSA_KV_WIDTH, 2 * NSA_KV_WIDTH, NSA_WIDTH, 2 * LRU_WIDTH)
_WN_OFFS = tuple(int(v) for v in np.cumsum((0,) + _WN_COLS))
_WT_ROWS = (NSA_WIDTH, NSA_KV_WIDTH, NSA_KV_WIDTH, NSA_KV_HEADS * GL_ROWS)
_WT_OFFS = tuple(int(v) for v in np.cumsum((0,) + _WT_ROWS))


def _proj_kernel(x_ref, wn_ref, wt_ref,
                 cos_ref, sa_ref, sb_ref, qd_ref, kd_ref, dmat_ref,
                 cw_ref, cb_ref, wa_ref, ba_ref, wx_ref, bx_ref, lam_ref,
                 yret_ref, ylru_ref, kcn_ref, vcn_ref, ks_ref, kw_ref, ng_ref,
                 qt_ref, vst_ref, vwt_ref, glt_ref,
                 state_ref, xext_ref, h_ref, *, chunk_decay):
    tm = x_ref.shape[0]

    @pl.when(pl.program_id(1) == 0)
    def _():
        state_ref[...] = jnp.zeros_like(state_ref)
        xext_ref[0:SUBLANES, :] = jnp.zeros((SUBLANES, LRU_WIDTH), F32)
        h_ref[...] = jnp.zeros_like(h_ref)

    xb = x_ref[...].astype(BF16)

    def nat(i):
        return jnp.dot(xb, wn_ref[:, _WN_OFFS[i]:_WN_OFFS[i + 1]], preferred_element_type=F32)

    def tr(i):
        return lax.dot_general(wt_ref[_WT_OFFS[i]:_WT_OFFS[i + 1], :], xb, _CONTRACT_LAST,
                               preferred_element_type=F32)

    def nat_rows(i, rows):
        return jnp.dot(xb[rows, :], wn_ref[:, _WN_OFFS[i]:_WN_OFFS[i + 1]],
                       preferred_element_type=F32)

    assert RET_CHUNK == LRU_TS
    chunks = list(range(0, tm, RET_CHUNK))
    ret, lru = {}, {}

    def project_chunk(c0):
        rows = slice(c0, c0 + RET_CHUNK)
        ret[c0] = nat_rows(0, rows)
        lru[c0] = nat_rows(4, rows)

    def mix_retention(c0):
        if c0 + RET_CHUNK < tm:
            project_chunk(c0 + RET_CHUNK)
        rows = slice(c0, c0 + RET_CHUNK)
        yret_ref[rows, :] = _retention_chunk(
            ret[c0], cos_ref[rows, :], sa_ref[rows, :], sb_ref[rows, :], qd_ref[...],
            kd_ref[...], dmat_ref, state_ref, chunk_decay).astype(yret_ref.dtype)

    def mix_lru(c0):
        rows = slice(c0, c0 + LRU_TS)
        ylru_ref[rows, :] = _lru_tile(lru[c0], cw_ref, cb_ref, wa_ref, ba_ref, wx_ref, bx_ref,
                                      lam_ref, xext_ref, h_ref).astype(ylru_ref.dtype)

    project_chunk(0)

    def proj_keys():
        kvc = nat(1)
        kcn_ref[...] = kvc[:, 0:NSA_KV_WIDTH]
        vcn_ref[...] = kvc[:, NSA_KV_WIDTH:2 * NSA_KV_WIDTH]
        ksw = nat(2)
        ks = ksw[:, 0:NSA_KV_WIDTH]
        key = pl.program_id(1) * tm + lax.broadcasted_iota(jnp.int32, (tm, 1), 0)
        blk = lax.broadcasted_iota(jnp.int32, (1, HEAD_DIM), 1)
        onehot = jnp.where(lax.shift_right_logical(key, SEL_SHIFT) == blk, 1.0, 0.0)
        for g in range(NSA_KV_HEADS):
            ks_ref[g] = jnp.concatenate([ks[:, g * HEAD_DIM:(g + 1) * HEAD_DIM], onehot],
                                        axis=1).astype(BF16)
        kw_ref[...] = ksw[:, NSA_KV_WIDTH:2 * NSA_KV_WIDTH].astype(BF16)

    def proj_gate():
        ng_ref[...] = nat(3)

    def proj_queries():
        qt_ref[...] = (tr(0) * LOG2E).astype(BF16)

    def proj_values():
        ones_rows = jnp.where(lax.broadcasted_iota(jnp.int32, (BF16_SUBLANES, tm), 0) == 0,
                              1.0, 0.0).astype(BF16)
        for ref, i in ((vst_ref, 1), (vwt_ref, 2)):
            vt = tr(i)
            for g in range(NSA_KV_HEADS):
                ref[g * V_ROWS:g * V_ROWS + HEAD_DIM, :] = (
                    vt[g * HEAD_DIM:(g + 1) * HEAD_DIM, :].astype(BF16))
                ref[g * V_ROWS + HEAD_DIM:(g + 1) * V_ROWS, :] = ones_rows
        glt_ref[...] = tr(3)

    mixers = [functools.partial(mix, c0) for c0 in chunks for mix in (mix_retention, mix_lru)]
    projections = [proj_keys, proj_gate, proj_queries, proj_values]
    for step in range(max(len(mixers), len(projections))):
        if step < len(mixers):
            mixers[step]()
        if step < len(projections):
            projections[step]()


def _project_and_mix(x, wn, wt, conv_w, conv_b, w_a, b_a, w_x, b_x, lam):
    B, S, _ = x.shape
    tm = PROJ_TM
    nn, nt = wn.shape[1], wt.shape[0]
    G, C, W, LW = NSA_KV_HEADS, RET_CHUNK, RET_WIDTH, LRU_WIDTH
    cos, sa, sb, qd, kd, dmat, chunk_decay = _ret_tables(S)
    row = lambda w: pl.BlockSpec((None, tm, w), lambda b, i: (b, i, 0))
    col = lambda r: pl.BlockSpec((None, r, tm), lambda b, i: (b, 0, i))
    kspec = pl.BlockSpec((None, G, tm, 2 * HEAD_DIM), lambda b, i: (b, 0, i, 0))
    fixed = lambda *shape: pl.BlockSpec(shape, lambda b, i: (0,) * len(shape))
    pos = pl.BlockSpec((tm, W), lambda b, i: (i, 0))
    vec = lambda a: a.reshape(1, LW)
    out_shape = (
        jax.ShapeDtypeStruct((B, S, W), BF16),
        jax.ShapeDtypeStruct((B, S, LW), BF16),
        jax.ShapeDtypeStruct((B, S, NSA_KV_WIDTH), F32),
        jax.ShapeDtypeStruct((B, S, NSA_KV_WIDTH), F32),
        jax.ShapeDtypeStruct((B, G, S, 2 * HEAD_DIM), BF16),
        jax.ShapeDtypeStruct((B, S, NSA_KV_WIDTH), BF16),
        jax.ShapeDtypeStruct((B, S, NSA_WIDTH), F32),
        jax.ShapeDtypeStruct((B, NSA_WIDTH, S), BF16),
        jax.ShapeDtypeStruct((B, G * V_ROWS, S), BF16),
        jax.ShapeDtypeStruct((B, G * V_ROWS, S), BF16),
        jax.ShapeDtypeStruct((B, G * GL_ROWS, S), F32),
    )
    return pl.pallas_call(
        functools.partial(_proj_kernel, chunk_decay=chunk_decay),
        grid=(B, S // tm),
        in_specs=[row(D_MODEL), fixed(D_MODEL, nn), fixed(nt, D_MODEL),
                  pos, pos, pos, fixed(C, W), fixed(C, W), fixed(RET_HEADS, C, C),
                  fixed(CONV_WIDTH, LW), fixed(1, LW), fixed(LW, LW), fixed(1, LW), fixed(LW, LW),
                  fixed(1, LW), fixed(1, LW)],
        out_specs=(row(W), row(LW), row(NSA_KV_WIDTH), row(NSA_KV_WIDTH), kspec,
                   row(NSA_KV_WIDTH), row(NSA_WIDTH), col(NSA_WIDTH), col(G * V_ROWS),
                   col(G * V_ROWS), col(G * GL_ROWS)),
        out_shape=out_shape,
        scratch_shapes=[pltpu.VMEM((RET_HEADS, HEAD_DIM, HEAD_DIM), F32),
                        pltpu.VMEM((LRU_TS + SUBLANES, LW), F32), pltpu.VMEM((SUBLANES, LW), F32)],
        compiler_params=_cparams(("parallel", "arbitrary")),
        name="in_proj_mix",
    )(x, wn, wt, jnp.asarray(cos), jnp.asarray(sa), jnp.asarray(sb), jnp.asarray(qd),
      jnp.asarray(kd), jnp.asarray(dmat), conv_w, vec(conv_b), _block_diag(w_a).astype(BF16),
      vec(b_a), _block_diag(w_x).astype(BF16), vec(b_x), vec(lam))


def _prep_in_weights(w_in):
    seg = [w_in[:, IN_OFFS[i]:IN_OFFS[i + 1]] for i in range(len(IN_SIZES))]
    (rq, rk, rv, rg, nq, nkc, nvc, nks, nvs, nkw, nvw, ng, ngl, lx, lg) = seg
    wn = jnp.concatenate([rq, rk, rv, rg, nkc, nvc, nks, nkw, ng, lx, lg], axis=1).astype(BF16)
    glt = ngl.T.reshape(NSA_KV_HEADS, NSA_GROUP * N_BRANCH, D_MODEL)
    glt = jnp.pad(glt, ((0, 0), (0, GL_ROWS - NSA_GROUP * N_BRANCH), (0, 0)))
    wt = jnp.concatenate([nq.T * (HEAD_DIM ** -0.5), nvs.T, nvw.T,
                          glt.reshape(NSA_KV_HEADS * GL_ROWS, D_MODEL)], axis=0).astype(BF16)
    return wn, wt


def _retention_chunk(ret, cos, sa, sb, qdec, kdec, dmat_ref, state_ref, chunk_decay):
    W = RET_WIDTH
    q = ret[:, 0:W]
    k = ret[:, W:2 * W]
    v = ret[:, 2 * W:3 * W]
    gate = ret[:, 3 * W:4 * W]
    half = HEAD_DIM // 2

    def rope(t):
        cols = []
        for c0 in range(0, W, LANES):
            tc = t[:, c0:c0 + LANES]
            cols.append(tc * cos[:, c0:c0 + LANES]
                        + pltpu.roll(tc, LANES - half, 1) * sa[:, c0:c0 + LANES]
                        + pltpu.roll(tc, half, 1) * sb[:, c0:c0 + LANES])
        return jnp.concatenate(cols, axis=1)

    qr = rope(q)
    kr = rope(k) * (HEAD_DIM ** -0.5)
    qd = qr * qdec
    kd = kr * kdec
    ys = []
    for h in range(RET_HEADS):
        sl = slice(h * HEAD_DIM, (h + 1) * HEAD_DIM)
        qb, kb, vb = qr[:, sl].astype(BF16), kr[:, sl].astype(BF16), v[:, sl].astype(BF16)
        s = lax.dot_general(qb, kb, _CONTRACT_LAST, preferred_element_type=F32) * dmat_ref[h]
        inner = jnp.dot(s.astype(BF16), vb, preferred_element_type=F32)
        st = state_ref[h]
        cross = jnp.dot(qd[:, sl].astype(BF16), st.astype(BF16), preferred_element_type=F32)
        y = inner + cross
        mu = jnp.mean(y, axis=-1, keepdims=True)
        yc = y - mu
        var = jnp.mean(yc * yc, axis=-1, keepdims=True)
        ys.append(yc * lax.rsqrt(var + LN_EPS))
        kv = lax.dot_general(kd[:, sl].astype(BF16), vb, _CONTRACT_FIRST,
                             preferred_element_type=F32)
        state_ref[h] = st * chunk_decay[h] + kv
    y = jnp.concatenate(ys, axis=1)
    return y * (gate * jax.nn.sigmoid(gate))


def _ret_tables(S):
    C, H, d = RET_CHUNK, RET_HEADS, HEAD_DIM
    half = d // 2
    inv = 1.0 / (ROPE_BASE ** (np.arange(half, dtype=np.float64) / half))
    ang = np.arange(S, dtype=np.float64)[:, None] * inv[None, :]
    cos_h = np.concatenate([np.cos(ang), np.cos(ang)], axis=1)
    sin_lo = np.concatenate([-np.sin(ang), np.zeros_like(ang)], axis=1)
    sin_hi = np.concatenate([np.zeros_like(ang), np.sin(ang)], axis=1)
    tile = lambda a: np.tile(a, (1, H)).astype(np.float32)
    log_g = np.log(1.0 - 2.0 ** (-5.0 - np.arange(H, dtype=np.float64)))
    idx = np.arange(C, dtype=np.float64)
    diff = idx[:, None] - idx[None, :]
    dmat = np.where(diff >= 0, np.exp(np.maximum(diff, 0.0)[None] * log_g[:, None, None]), 0.0)
    kd = np.exp((C - 1.0 - idx)[:, None] * log_g[None, :])
    qd = np.exp((idx + 1.0)[:, None] * log_g[None, :])
    rep = lambda a: np.repeat(a, d, axis=1).astype(np.float32)
    chunk_decay = tuple(float(v) for v in np.exp(C * log_g))
    return (tile(cos_h), tile(sin_lo), tile(sin_hi), rep(qd), rep(kd),
            dmat.astype(np.float32), chunk_decay)


_CMP_ROWS = 256


def _cmp_kernel(kcn_ref, vcn_ref, pos_ref, wl_ref, kc_ref, vct_ref):
    n, W = _CMP_ROWS, NSA_KV_WIDTH
    res = []
    for i, ref in enumerate((kcn_ref, vcn_ref)):
        out = jnp.zeros((n, 2 * W), F32)
        cst = jnp.zeros((SUBLANES, 2 * W), F32)
        for l in range(CMP_STRIDE):
            x = ref[pl.ds(l, n, stride=CMP_STRIDE), :].astype(BF16)
            out = out + jnp.dot(x, wl_ref[i, l], preferred_element_type=F32)
            cst = cst + jnp.dot(pos_ref[i, l].astype(BF16), wl_ref[i, l],
                                preferred_element_type=F32)
        const = cst[0:1, 0:W] + cst[1:2, W:2 * W]
        res.append(out[:, 0:W] + pltpu.roll(out[:, W:2 * W], n - 1, 0) + const)
    kc_ref[...] = res[0].astype(BF16)
    vct_ref[...] = res[1].T.astype(BF16)


def _prep_cmp_weights(cmp_pos, cmp_w):
    L2, d, G = CMP_STRIDE, HEAD_DIM, NSA_KV_HEADS
    assert G == 2
    w5 = cmp_w.reshape(2, 2, L2, d, d)
    z = jnp.zeros((2, L2, d, d), F32)

    def both_groups(w):
        return jnp.concatenate([jnp.concatenate([w, z], axis=-1),
                                jnp.concatenate([z, w], axis=-1)], axis=-2)

    wl = jnp.concatenate([both_groups(w5[:, 0]), both_groups(w5[:, 1])], axis=-1).astype(BF16)
    p4 = cmp_pos.reshape(2, 2, L2, d)
    prow = jnp.concatenate([p4, p4], axis=-1).transpose(0, 2, 1, 3)
    prow = jnp.pad(prow, ((0, 0), (0, 0), (0, SUBLANES - 2), (0, 0)))
    return prow, wl


def _compress(kcn, vcn, prow, wl):
    B, S, W = kcn.shape
    n = _CMP_ROWS
    return pl.pallas_call(
        _cmp_kernel,
        grid=(B,),
        in_specs=[pl.BlockSpec((None, S, W), lambda b: (b, 0, 0)),
                  pl.BlockSpec((None, S, W), lambda b: (b, 0, 0)),
                  pl.BlockSpec(prow.shape, lambda b: (0, 0, 0, 0)),
                  pl.BlockSpec(wl.shape, lambda b: (0, 0, 0, 0))],
        out_specs=(pl.BlockSpec((None, n, NSA_KV_WIDTH), lambda b: (b, 0, 0)),
                   pl.BlockSpec((None, NSA_KV_WIDTH, n), lambda b: (b, 0, 0))),
        out_shape=(jax.ShapeDtypeStruct((B, n, NSA_KV_WIDTH), BF16),
                   jax.ShapeDtypeStruct((B, NSA_KV_WIDTH, n), BF16)),
        compiler_params=_cparams(("parallel",)),
        name="nsa_compress",
    )(kcn, vcn, prow, wl)


def _nsa_tables(S):
    n_cmp = (S - CMP_LEN) // CMP_STRIDE + 1
    n_sel = S // SEL_LEN
    ci = np.arange(n_cmp)[:, None]
    sj = np.arange(n_sel)[None, :]
    overlap = (np.minimum(ci * CMP_STRIDE + CMP_LEN, sj * SEL_LEN + SEL_LEN)
               - np.maximum(ci * CMP_STRIDE, sj * SEL_LEN))
    sel_map = np.clip(overlap, 0, None) / CMP_STRIDE
    selmap_t = np.zeros((n_sel, _CMP_ROWS), np.float32)
    selmap_t[:, :n_cmp] = sel_map.T
    jl = np.arange(NSA_KB)[:, None]
    tl = np.arange(NSA_TQ)[None, :]
    band = np.stack([np.where(jl <= tl, 0.0, NEG), np.where(jl > tl, 0.0, NEG)]).astype(np.float32)
    rel = (np.arange(2 * _CMP_ROWS)[:, None] - _CMP_ROWS) * CMP_STRIDE + (CMP_LEN - 1)
    cmpmask = np.where(rel <= tl, 0.0, NEG).astype(np.float32)
    return selmap_t, band, cmpmask


def _nsa_kernel(qt_ref, kc_ref, vct_ref, ks_ref, vst_ref, kw_ref, vwt_ref, glt_ref, ng_ref,
                     selmap_ref, band_ref, cmpmask_ref, x_ref, yret_ref, ylru_ref, wo_ref, lng_ref,
                     lnb_ref, out_ref, val_ref, cnt_ref, s_ref, p_ref, acc_ref, st_ref, part_ref):
    TQ, KB, R, d, G = NSA_TQ, NSA_KB, NSA_GROUP, HEAD_DIM, NSA_KV_HEADS
    n_sel = selmap_ref.shape[0]
    qi = pl.program_id(1)
    t0 = qi * TQ
    o1, o2 = RET_WIDTH, RET_WIDTH + NSA_WIDTH
    part_ref[...] = (DEEPNORM_ALPHA * x_ref[...]
                     + jnp.dot(yret_ref[...], wo_ref[0:o1, :], preferred_element_type=F32)
                     + jnp.dot(ylru_ref[...], wo_ref[o2:D_MODEL, :], preferred_element_type=F32))
    groups = range(G)
    heads = [slice(r * TQ, (r + 1) * TQ) for r in range(R)]
    qs = [[qt_ref[(g * R + r) * d:(g * R + r + 1) * d, :] for r in range(R)] for g in groups]
    zq = jnp.zeros((d, TQ), BF16)
    qs_nat = [jnp.concatenate(
        [jnp.concatenate([q if gg == g else zq for gg in groups], axis=0) for q in qs[g]], axis=1)
        for g in groups]

    def key_off(kb):
        return pl.multiple_of(kb * KB, KB)

    n_wb = WINDOW // KB + 1
    w_kb = [qi - (n_wb - 1) + w for w in range(n_wb)]
    w_off = [key_off(jnp.maximum(kb, 0)) for kb in w_kb]
    w_bias = [band_ref[1]] + [None] * (n_wb - 2) + [band_ref[0]]
    n_cmp = kc_ref.shape[0]
    cmp_off = pl.multiple_of(n_cmp - qi * (TQ // CMP_STRIDE), TQ // CMP_STRIDE)
    cmp_bias = jnp.concatenate([cmpmask_ref[pl.ds(cmp_off, n_cmp), :]] * R, axis=1)
    tq = t0 + lax.broadcasted_iota(jnp.int32, (1, TQ), 1)
    seen = jnp.concatenate([tq >= CMP_LEN - 1] * R, axis=1)

    s_cmp = [jnp.dot(kc_ref[...], qs_nat[g], preferred_element_type=F32) + cmp_bias
             for g in groups]
    w_s = []
    for g in groups:
        blocks = []
        for w in range(n_wb):
            s_w = jnp.dot(kw_ref[pl.ds(w_off[w], KB), :], qs_nat[g], preferred_element_type=F32)
            if w_bias[w] is not None:
                s_w = s_w + jnp.concatenate([w_bias[w]] * R, axis=1)
            if w < n_wb - 1:
                s_w = s_w + jnp.where(w_kb[w] >= 0, 0.0, NEG)
            blocks.append(s_w)
        w_s.append(blocks)

    jrow = lax.broadcasted_iota(jnp.int32, (n_sel, 1), 0)
    cur = lax.shift_right_logical(tq, SEL_SHIFT)
    forced = (jrow == 0) | (jrow == cur) | (jrow == cur - 1)
    causal = jrow * SEL_LEN <= tq
    o_c = []
    for g in groups:
        p = jnp.exp2(s_cmp[g] - jnp.max(s_cmp[g], axis=0, keepdims=True))
        p = p * jnp.where(seen, 1.0 / jnp.sum(p, axis=0, keepdims=True), 0.0)
        o_c.append(jnp.dot(vct_ref[g * d:(g + 1) * d, :], p.astype(BF16),
                           preferred_element_type=F32))
        psum = functools.reduce(jnp.add, [p[:, hs] for hs in heads])
        p_hi = psum.astype(BF16)
        p_lo = (psum - p_hi.astype(F32)).astype(BF16)
        imp = (jnp.dot(selmap_ref[...], p_hi, preferred_element_type=F32)
               + jnp.dot(selmap_ref[...], p_lo, preferred_element_type=F32))
        val_ref[g] = jnp.where(forced, FORCE_SCORE, jnp.where(causal, imp, -FORCE_SCORE))
    cnt_ref[...] = jnp.zeros_like(cnt_ref)

    n_live = (qi + 1) * (TQ // SEL_LEN)
    n_grp = n_sel // SUBLANES
    row_groups = [slice(k * SUBLANES, (k + 1) * SUBLANES) for k in range(n_grp)]
    for c in range(n_grp):

        @pl.when(c * SUBLANES < n_live)
        def _():
            own = lax.broadcasted_iota(jnp.int32, (SUBLANES, 1), 0) + c * SUBLANES
            for g in groups:
                vals = [val_ref[g, rows, :] for rows in row_groups]
                cnts = [cnt_ref[g, rows, :] for rows in row_groups]
                for jp in range(c * SUBLANES, (c + 1) * SUBLANES):
                    cand = jnp.broadcast_to(val_ref[g, jp:jp + 1, :], (SUBLANES, TQ))
                    for k in range(n_grp):
                        if k < c:
                            hit = jnp.where(cand > vals[k], 1.0, 0.0)
                        elif k > c:
                            hit = jnp.where(cand >= vals[k], 1.0, 0.0)
                        else:
                            hit = jnp.where(own > jp, jnp.where(cand >= vals[k], 1.0, 0.0),
                                            jnp.where(cand > vals[k], 1.0, 0.0))
                        cnts[k] = cnts[k] + hit
                for rows, cnt in zip(row_groups, cnts):
                    cnt_ref[g, rows, :] = cnt

    qs_sel = []
    for g in groups:
        bias = jnp.where(cnt_ref[g] < float(N_SELECT), 0.0, NEG).astype(BF16)
        qs_sel.append([jnp.concatenate([q, bias], axis=0) for q in qs[g]])

    def sel_scores(g, kb, r):
        return jnp.dot(ks_ref[g, pl.ds(key_off(kb), KB), :], qs_sel[g][r],
                       preferred_element_type=F32)

    def sel_values(g, kb, p):
        return jnp.dot(vst_ref[g * V_ROWS:(g + 1) * V_ROWS, pl.ds(key_off(kb), KB)], p,
                       preferred_element_type=F32)

    ROW_M, ROW_A = 0, 1

    def sel_stage(i, cur_slot, *, diag, mxu_first=False):
        nxt = 1 - cur_slot
        chains = [(g, r, hs) for g in groups for r, hs in enumerate(heads)]
        pvs = {}
        for g, r, hs in chains:
            if mxu_first and r == 0:
                for r2, h2 in enumerate(heads):
                    pvs[g, r2] = sel_values(g, jnp.maximum(i - 1, 0), p_ref[g, nxt, :, h2])
                for r2, h2 in enumerate(heads):
                    s_ref[g, nxt, :, h2] = sel_scores(g, i + 1, r2)
            s = s_ref[g, cur_slot, :, hs]
            if diag:
                s = s + band_ref[0]
            m = st_ref[g, ROW_M:ROW_M + 1, hs]
            m_new = jnp.maximum(m, jnp.max(s, axis=0, keepdims=True))
            p_ref[g, cur_slot, :, hs] = jnp.exp2(s - m_new).astype(BF16)
            if mxu_first:
                pv_prev = pvs[g, r]
            else:
                pv_prev = sel_values(g, jnp.maximum(i - 1, 0), p_ref[g, nxt, :, hs])
                if not diag:
                    s_ref[g, nxt, :, hs] = sel_scores(g, i + 1, r)
            acc_ref[g, :, hs] = st_ref[g, ROW_A:ROW_A + 1, hs] * acc_ref[g, :, hs] + pv_prev
            st_ref[g, ROW_M:ROW_M + 1, hs] = m_new
            st_ref[g, ROW_A:ROW_A + 1, hs] = jnp.exp2(m - m_new)

    def sel_finish(i, cur_slot):
        for hs in heads:
            for g in groups:
                acc_ref[g, :, hs] = (st_ref[g, ROW_A:ROW_A + 1, hs] * acc_ref[g, :, hs]
                                     + sel_values(g, i, p_ref[g, cur_slot, :, hs]))

    for g in groups:
        for r, hs in enumerate(heads):
            s_ref[g, 0, :, hs] = sel_scores(g, 0, r)
        p_ref[g, 1] = jnp.zeros(p_ref.shape[2:], BF16)
        acc_ref[g] = jnp.zeros(acc_ref.shape[1:], F32)
        st_ref[g, ROW_M:ROW_M + 1, :] = jnp.full((1, R * TQ), NEG, F32)
        st_ref[g, ROW_A:ROW_A + 1, :] = jnp.ones((1, R * TQ), F32)

    acc_w = []
    for g in groups:
        m_w = functools.reduce(jnp.maximum,
                               [jnp.max(s_w, axis=0, keepdims=True) for s_w in w_s[g]])
        acc_w.append(functools.reduce(jnp.add, [
            jnp.dot(vwt_ref[g * V_ROWS:(g + 1) * V_ROWS, pl.ds(w_off[w], KB)],
                    jnp.exp2(w_s[g][w] - m_w).astype(BF16), preferred_element_type=F32)
            for w in range(n_wb)]))

    def sel_pair(j, c):
        sel_stage(2 * j, 0, diag=False)
        sel_stage(2 * j + 1, 1, diag=False, mxu_first=True)
        return c

    lax.fori_loop(0, lax.shift_right_logical(qi, 1), sel_pair, 0)
    odd = lax.rem(qi, 2) == 1

    @pl.when(odd)
    def _():
        sel_stage(qi - 1, 0, diag=False)
        sel_stage(qi, 1, diag=True)
        sel_finish(qi, 1)

    @pl.when(jnp.logical_not(odd))
    def _():
        sel_stage(qi, 0, diag=True)
        sel_finish(qi, 0)

    gl = jax.nn.sigmoid(glt_ref[...])
    y_nsa = []
    for g in groups:
        res = []
        for r, hs in enumerate(heads):
            a_s = acc_ref[g, :, hs]
            a_w = acc_w[g][:, hs]
            row0 = g * GL_ROWS + r * N_BRANCH
            gate = [gl[row0 + br:row0 + br + 1, :] for br in range(N_BRANCH)]
            res.append(gate[0] * o_c[g][:, hs]
                       + (gate[1] / a_s[d:d + 1, :]) * a_s[0:d, :]
                       + (gate[2] / a_w[d:d + 1, :]) * a_w[0:d, :])
        ng = ng_ref[:, g * R * d:(g + 1) * R * d]
        y_nsa.append((jnp.concatenate(res, axis=0).T * (ng * jax.nn.sigmoid(ng))).astype(BF16))

    z = part_ref[...] + jnp.dot(jnp.concatenate(y_nsa, axis=1), wo_ref[o1:o2, :],
                                preferred_element_type=F32)
    mu = jnp.mean(z, axis=-1, keepdims=True)
    zc = z - mu
    var = jnp.mean(zc * zc, axis=-1, keepdims=True)
    out_ref[...] = zc * lax.rsqrt(var + LN_EPS) * lng_ref[...] + lnb_ref[...]


def _nsa_attention(qt, kc, vct, ks, vst, kw, vwt, glt, ng, x, y_ret, y_lru, w_out, ln_g, ln_b):
    B, _, S = qt.shape
    G, R, d, TQ = NSA_KV_HEADS, NSA_GROUP, HEAD_DIM, NSA_TQ
    selmap_t, band, cmpmask = _nsa_tables(S)
    n_sel = S // SEL_LEN
    assert NSA_KB == TQ and WINDOW % NSA_KB == 0
    assert n_sel == d
    per_b = lambda *shape: pl.BlockSpec((None,) + shape, lambda b, q: (b,) + (0,) * len(shape))
    fixed = lambda a: pl.BlockSpec(a.shape, lambda b, q: (0,) * a.ndim)
    rows = lambda w: pl.BlockSpec((None, TQ, w), lambda b, q: (b, q, 0))
    return pl.pallas_call(
        _nsa_kernel,
        grid=(B, S // TQ),
        in_specs=[pl.BlockSpec((None, G * R * d, TQ), lambda b, q: (b, 0, q)),
                  per_b(_CMP_ROWS, G * d), per_b(G * d, _CMP_ROWS),
                  per_b(G, S, 2 * d), per_b(G * V_ROWS, S), per_b(S, G * d), per_b(G * V_ROWS, S),
                  pl.BlockSpec((None, G * GL_ROWS, TQ), lambda b, q: (b, 0, q)),
                  rows(G * R * d), fixed(selmap_t), fixed(band), fixed(cmpmask),
                  rows(D_MODEL), rows(RET_WIDTH), rows(LRU_WIDTH),
                  pl.BlockSpec((D_MODEL, D_MODEL), lambda b, q: (0, 0)),
                  pl.BlockSpec((1, D_MODEL), lambda b, q: (0, 0)),
                  pl.BlockSpec((1, D_MODEL), lambda b, q: (0, 0))],
        out_specs=rows(D_MODEL),
        out_shape=jax.ShapeDtypeStruct((B, S, D_MODEL), F32),
        scratch_shapes=[pltpu.VMEM((G, n_sel, TQ), F32), pltpu.VMEM((G, n_sel, TQ), F32),
                        pltpu.VMEM((G, 2, NSA_KB, R * TQ), F32),
                        pltpu.VMEM((G, 2, NSA_KB, R * TQ), BF16),
                        pltpu.VMEM((G, V_ROWS, R * TQ), F32),
                        pltpu.VMEM((G, SUBLANES, R * TQ), F32),
                        pltpu.VMEM((TQ, D_MODEL), F32)],
        compiler_params=_cparams(("parallel", "arbitrary")),
        name="nsa_attention_out",
    )(qt, kc, vct, ks, vst, kw, vwt, glt, ng, jnp.asarray(selmap_t, dtype=BF16), jnp.asarray(band),
      jnp.asarray(cmpmask), x, y_ret, y_lru, w_out.astype(BF16), ln_g.reshape(1, D_MODEL),
      ln_b.reshape(1, D_MODEL))


def _lru_tile(lru, cw_ref, cb_ref, wa_ref, ba_ref, wx_ref, bx_ref, lam_ref, xext_ref, h_ref):
    ts, W, pad = LRU_TS, LRU_WIDTH, SUBLANES
    x = lru[:, 0:W]
    gate = lru[:, W:2 * W]
    xext_ref[pad:pad + ts, :] = x
    first = pad - (CONV_WIDTH - 1)
    xc = cb_ref[...] + cw_ref[0:1, :] * xext_ref[first:first + ts, :]
    for w in range(1, CONV_WIDTH):
        xc = xc + cw_ref[w:w + 1, :] * xext_ref[first + w:first + w + ts, :]
    xext_ref[0:pad, :] = x[ts - pad:ts, :]
    xcb = xc.astype(BF16)
    r = jax.nn.sigmoid(jnp.dot(xcb, wa_ref[...], preferred_element_type=F32) + ba_ref[...])
    gi = jax.nn.sigmoid(jnp.dot(xcb, wx_ref[...], preferred_element_type=F32) + bx_ref[...])
    nl = -lam_ref[...]
    softplus = jnp.maximum(nl, 0.0) + jnp.log1p(jnp.exp(-jnp.abs(nl)))
    log_a = (-LRU_C) * r * softplus
    a = jnp.exp(log_a)
    u = jnp.sqrt(-jnp.tanh(log_a) * (a * a + 1.0)) * (gi * xc)
    row = lax.broadcasted_iota(jnp.int32, (ts, 1), 0)
    step = 1
    while step < ts:
        if step < SUBLANES:
            keep = row >= step
            a_sh = jnp.where(keep, pltpu.roll(a, step, 0), 1.0)
            u_sh = jnp.where(keep, pltpu.roll(u, step, 0), 0.0)
        else:
            a_sh = jnp.concatenate([jnp.ones((step, W), F32), a[:ts - step, :]], axis=0)
            u_sh = jnp.concatenate([jnp.zeros((step, W), F32), u[:ts - step, :]], axis=0)
        u = a * u_sh + u
        a = a * a_sh
        step *= 2
    h = u + a * h_ref[0:1, :]
    h_ref[0:1, :] = h[ts - 1:ts, :]
    return h * (gate * jax.nn.sigmoid(gate))


def _block_diag(w):
    n, d, _ = w.shape
    z = jnp.zeros((d, d), w.dtype)
    return jnp.concatenate(
        [jnp.concatenate([w[i] if j == i else z for j in range(n)], axis=1) for i in range(n)],
        axis=0)


def _layer(x, w_in, w_out, ln_g, ln_b, cmp_pos, cmp_w, conv_w, conv_b, w_a, b_a, w_x, b_x, lam):
    wn, wt = _prep_in_weights(w_in)
    y_ret, y_lru, kcn, vcn, ks, kw, ng, qt, vst, vwt, glt = _project_and_mix(
        x, wn, wt, conv_w, conv_b, w_a, b_a, w_x, b_x, lam)
    prow, wl = _prep_cmp_weights(cmp_pos, cmp_w)
    kc, vct = _compress(kcn, vcn, prow, wl)
    return _nsa_attention(qt, kc, vct, ks, vst, kw, vwt, glt, ng, x, y_ret, y_lru, w_out,
                               ln_g, ln_b)


def kernel(x, w_in, w_out, ln_g, ln_b, nsa_cmp_pos, nsa_cmp_w, lru_conv_w, lru_conv_b,
           lru_w_a, lru_b_a, lru_w_x, lru_b_x, lru_lambda):
    assert x.shape[1] % max(PROJ_TM, RET_CHUNK, LRU_TS, NSA_TQ) == 0
    assert x.shape[1] // CMP_STRIDE == _CMP_ROWS
    for l in range(DEPTH):
        x = _layer(x, w_in[l], w_out[l], ln_g[l], ln_b[l], nsa_cmp_pos[l], nsa_cmp_w[l],
                   lru_conv_w[l], lru_conv_b[l], lru_w_a[l], lru_b_a[l], lru_w_x[l], lru_b_x[l],
                   lru_lambda[l])
    return x
```

```python
import functools
import math

import jax
import jax.numpy as jnp
import numpy as np
from jax import lax
from jax.experimental import pallas as pl
from jax.experimental.pallas import tpu as pltpu

F32 = jnp.float32
BF16 = jnp.bfloat16

D_MODEL = 1024
DEPTH = 2
HEAD_DIM = 64
RET_HEADS = 4
RET_WIDTH = RET_HEADS * HEAD_DIM
ROPE_BASE = 10000.0
NSA_HEADS = 8
NSA_KV_HEADS = 2
NSA_GROUP = NSA_HEADS // NSA_KV_HEADS
NSA_WIDTH = NSA_HEADS * HEAD_DIM
NSA_KV_WIDTH = NSA_KV_HEADS * HEAD_DIM
CMP_LEN = 32
CMP_STRIDE = 16
SEL_LEN = 64
N_SELECT = 16
WINDOW = 512
N_BRANCH = 3
FORCE_SCORE = 1.0e4
LRU_WIDTH = 256
LRU_BLOCKS = 4
LRU_BLOCK_DIM = LRU_WIDTH // LRU_BLOCKS
CONV_WIDTH = 4
LRU_C = 8.0
DEEPNORM_ALPHA = (2.0 * DEPTH) ** 0.25
LN_EPS = 1e-5
NEG = -1e30
LOG2E = math.log2(math.e)
SEL_SHIFT = int(math.log2(SEL_LEN))

IN_SIZES = (RET_WIDTH,) * 4 + (NSA_WIDTH,) + (NSA_KV_WIDTH,) * 6 + (
    NSA_WIDTH, NSA_HEADS * N_BRANCH, LRU_WIDTH, LRU_WIDTH)
IN_OFFS = tuple(int(v) for v in np.cumsum((0,) + IN_SIZES))

V7X_VMEM_LIMIT = 48 * 1024 * 1024
SUBLANES = 8
LANES = 128
BF16_SUBLANES = 16
PROJ_TM = 512
RET_CHUNK = 256
LRU_TS = 256
NSA_TQ = 256
NSA_KB = 256
NSA_TILES = 2
GL_ROWS = 16
V_ROWS = HEAD_DIM + BF16_SUBLANES

_CONTRACT_LAST = (((1,), (1,)), ((), ()))
_CONTRACT_FIRST = (((0,), (0,)), ((), ()))


def _cparams(sem):
    return pltpu.CompilerParams(dimension_semantics=sem, vmem_limit_bytes=V7X_VMEM_LIMIT)


_WN_COLS = (4 * RET_WIDTH, 2 * NSA_KV_WIDTH, 2 * NSA_KV_WIDTH, NSA_WIDTH, 2 * LRU_WIDTH)
_WN_OFFS = tuple(int(v) for v in np.cumsum((0,) + _WN_COLS))
_WT_ROWS = (NSA_WIDTH, NSA_KV_WIDTH, NSA_KV_WIDTH, NSA_KV_HEADS * GL_ROWS)
_WT_OFFS = tuple(int(v) for v in np.cumsum((0,) + _WT_ROWS))


def _proj_kernel(x_ref, wn_ref, wt_ref,
                 cos_ref, sa_ref, sb_ref, qd_ref, kd_ref, dmat_ref,
                 cw_ref, cb_ref, wa_ref, ba_ref, wx_ref, bx_ref, lam_ref,
                 yret_ref, ylru_ref, kcn_ref, vcn_ref, ks_ref, kw_ref, ng_ref,
                 qt_ref, vst_ref, vwt_ref, glt_ref,
                 state_ref, xext_ref, h_ref, *, chunk_decay):
    tm = x_ref.shape[0]

    @pl.when(pl.program_id(1) == 0)
    def _():
        state_ref[...] = jnp.zeros_like(state_ref)
        xext_ref[0:SUBLANES, :] = jnp.zeros((SUBLANES, LRU_WIDTH), F32)
        h_ref[...] = jnp.zeros_like(h_ref)

    xb = x_ref[...].astype(BF16)

    def nat(i):
        return jnp.dot(xb, wn_ref[:, _WN_OFFS[i]:_WN_OFFS[i + 1]], preferred_element_type=F32)

    def tr(i):
        return lax.dot_general(wt_ref[_WT_OFFS[i]:_WT_OFFS[i + 1], :], xb, _CONTRACT_LAST,
                               preferred_element_type=F32)

    def nat_rows(i, rows):
        return jnp.dot(xb[rows, :], wn_ref[:, _WN_OFFS[i]:_WN_OFFS[i + 1]],
                       preferred_element_type=F32)

    assert RET_CHUNK == LRU_TS
    chunks = list(range(0, tm, RET_CHUNK))
    ret, lru = {}, {}

    def project_chunk(c0):
        rows = slice(c0, c0 + RET_CHUNK)
        ret[c0] = nat_rows(0, rows)
        lru[c0] = nat_rows(4, rows)

    def mix_retention(c0):
        if c0 + RET_CHUNK < tm:
            project_chunk(c0 + RET_CHUNK)
        rows = slice(c0, c0 + RET_CHUNK)
        yret_ref[rows, :] = _retention_chunk(
            ret[c0], cos_ref[rows, :], sa_ref[rows, :], sb_ref[rows, :], qd_ref[...],
            kd_ref[...], dmat_ref, state_ref, chunk_decay).astype(yret_ref.dtype)

    def mix_lru(c0):
        rows = slice(c0, c0 + LRU_TS)
        ylru_ref[rows, :] = _lru_tile(lru[c0], cw_ref, cb_ref, wa_ref, ba_ref, wx_ref, bx_ref,
                                      lam_ref, xext_ref, h_ref).astype(ylru_ref.dtype)

    project_chunk(0)

    def proj_keys():
        kvc = nat(1)
        kcn_ref[...] = kvc[:, 0:NSA_KV_WIDTH]
        vcn_ref[...] = kvc[:, NSA_KV_WIDTH:2 * NSA_KV_WIDTH]
        ksw = nat(2)
        ks = ksw[:, 0:NSA_KV_WIDTH]
        key = pl.program_id(1) * tm + lax.broadcasted_iota(jnp.int32, (tm, 1), 0)
        blk = lax.broadcasted_iota(jnp.int32, (1, HEAD_DIM), 1)
        onehot = jnp.where(lax.shift_right_logical(key, SEL_SHIFT) == blk, 1.0, 0.0)
        for g in range(NSA_KV_HEADS):
            ks_ref[g] = jnp.concatenate([ks[:, g * HEAD_DIM:(g + 1) * HEAD_DIM], onehot],
                                        axis=1).astype(BF16)
        kw_ref[...] = ksw[:, NSA_KV_WIDTH:2 * NSA_KV_WIDTH].astype(BF16)

    def proj_gate():
        ng_ref[...] = nat(3)

    def proj_queries():
        qt_ref[...] = (tr(0) * LOG2E).astype(BF16)

    def proj_values():
        ones_rows = jnp.where(lax.broadcasted_iota(jnp.int32, (BF16_SUBLANES, tm), 0) == 0,
                              1.0, 0.0).astype(BF16)
        for ref, i in ((vst_ref, 1), (vwt_ref, 2)):
            vt = tr(i)
            for g in range(NSA_KV_HEADS):
                ref[g * V_ROWS:g * V_ROWS + HEAD_DIM, :] = (
                    vt[g * HEAD_DIM:(g + 1) * HEAD_DIM, :].astype(BF16))
                ref[g * V_ROWS + HEAD_DIM:(g + 1) * V_ROWS, :] = ones_rows
        glt_ref[...] = tr(3)

    mixers = [functools.partial(mix, c0) for c0 in chunks for mix in (mix_retention, mix_lru)]
    projections = [proj_keys, proj_gate, proj_queries, proj_values]
    for step in range(max(len(mixers), len(projections))):
        if step < len(mixers):
            mixers[step]()
        if step < len(projections):
            projections[step]()


def _project_and_mix(x, wn, wt, conv_w, conv_b, w_a, b_a, w_x, b_x, lam):
    B, S, _ = x.shape
    tm = PROJ_TM
    nn, nt = wn.shape[1], wt.shape[0]
    G, C, W, LW = NSA_KV_HEADS, RET_CHUNK, RET_WIDTH, LRU_WIDTH
    cos, sa, sb, qd, kd, dmat, chunk_decay = _ret_tables(S)
    row = lambda w: pl.BlockSpec((None, tm, w), lambda b, i: (b, i, 0))
    col = lambda r: pl.BlockSpec((None, r, tm), lambda b, i: (b, 0, i))
    kspec = pl.BlockSpec((None, G, tm, 2 * HEAD_DIM), lambda b, i: (b, 0, i, 0))
    fixed = lambda *shape: pl.BlockSpec(shape, lambda b, i: (0,) * len(shape))
    pos = pl.BlockSpec((tm, W), lambda b, i: (i, 0))
    vec = lambda a: a.reshape(1, LW)
    out_shape = (
        jax.ShapeDtypeStruct((B, S, W), BF16),
        jax.ShapeDtypeStruct((B, S, LW), BF16),
        jax.ShapeDtypeStruct((B, S, NSA_KV_WIDTH), F32),
        jax.ShapeDtypeStruct((B, S, NSA_KV_WIDTH), F32),
        jax.ShapeDtypeStruct((B, G, S, 2 * HEAD_DIM), BF16),
        jax.ShapeDtypeStruct((B, S, NSA_KV_WIDTH), BF16),
        jax.ShapeDtypeStruct((B, S, NSA_WIDTH), F32),
        jax.ShapeDtypeStruct((B, NSA_WIDTH, S), BF16),
        jax.ShapeDtypeStruct((B, G * V_ROWS, S), BF16),
        jax.ShapeDtypeStruct((B, G * V_ROWS, S), BF16),
        jax.ShapeDtypeStruct((B, G * GL_ROWS, S), F32),
    )
    return pl.pallas_call(
        functools.partial(_proj_kernel, chunk_decay=chunk_decay),
        grid=(B, S // tm),
        in_specs=[row(D_MODEL), fixed(D_MODEL, nn), fixed(nt, D_MODEL),
                  pos, pos, pos, fixed(C, W), fixed(C, W), fixed(RET_HEADS, C, C),
                  fixed(CONV_WIDTH, LW), fixed(1, LW), fixed(LW, LW), fixed(1, LW), fixed(LW, LW),
                  fixed(1, LW), fixed(1, LW)],
        out_specs=(row(W), row(LW), row(NSA_KV_WIDTH), row(NSA_KV_WIDTH), kspec,
                   row(NSA_KV_WIDTH), row(NSA_WIDTH), col(NSA_WIDTH), col(G * V_ROWS),
                   col(G * V_ROWS), col(G * GL_ROWS)),
        out_shape=out_shape,
        scratch_shapes=[pltpu.VMEM((RET_HEADS, HEAD_DIM, HEAD_DIM), F32),
                        pltpu.VMEM((LRU_TS + SUBLANES, LW), F32), pltpu.VMEM((SUBLANES, LW), F32)],
        compiler_params=_cparams(("parallel", "arbitrary")),
        name="in_proj_mix",
    )(x, wn, wt, jnp.asarray(cos), jnp.asarray(sa), jnp.asarray(sb), jnp.asarray(qd),
      jnp.asarray(kd), jnp.asarray(dmat), conv_w, vec(conv_b), _block_diag(w_a).astype(BF16),
      vec(b_a), _block_diag(w_x).astype(BF16), vec(b_x), vec(lam))


def _prep_in_weights(w_in):
    seg = [w_in[:, IN_OFFS[i]:IN_OFFS[i + 1]] for i in range(len(IN_SIZES))]
    (rq, rk, rv, rg, nq, nkc, nvc, nks, nvs, nkw, nvw, ng, ngl, lx, lg) = seg
    wn = jnp.concatenate([rq, rk, rv, rg, nkc, nvc, nks, nkw, ng, lx, lg], axis=1).astype(BF16)
    glt = ngl.T.reshape(NSA_KV_HEADS, NSA_GROUP * N_BRANCH, D_MODEL)
    glt = jnp.pad(glt, ((0, 0), (0, GL_ROWS - NSA_GROUP * N_BRANCH), (0, 0)))
    wt = jnp.concatenate([nq.T * (HEAD_DIM ** -0.5), nvs.T, nvw.T,
                          glt.reshape(NSA_KV_HEADS * GL_ROWS, D_MODEL)], axis=0).astype(BF16)
    return wn, wt


def _retention_chunk(ret, cos, sa, sb, qdec, kdec, dmat_ref, state_ref, chunk_decay):
    W = RET_WIDTH
    q = ret[:, 0:W]
    k = ret[:, W:2 * W]
    v = ret[:, 2 * W:3 * W]
    gate = ret[:, 3 * W:4 * W]
    half = HEAD_DIM // 2

    def rope(t):
        cols = []
        for c0 in range(0, W, LANES):
            tc = t[:, c0:c0 + LANES]
            cols.append(tc * cos[:, c0:c0 + LANES]
                        + pltpu.roll(tc, LANES - half, 1) * sa[:, c0:c0 + LANES]
                        + pltpu.roll(tc, half, 1) * sb[:, c0:c0 + LANES])
        return jnp.concatenate(cols, axis=1)

    qr = rope(q)
    kr = rope(k) * (HEAD_DIM ** -0.5)
    qd = qr * qdec
    kd = kr * kdec
    ys = []
    for h in range(RET_HEADS):
        sl = slice(h * HEAD_DIM, (h + 1) * HEAD_DIM)
        qb, kb, vb = qr[:, sl].astype(BF16), kr[:, sl].astype(BF16), v[:, sl].astype(BF16)
        s = lax.dot_general(qb, kb, _CONTRACT_LAST, preferred_element_type=F32) * dmat_ref[h]
        inner = jnp.dot(s.astype(BF16), vb, preferred_element_type=F32)
        st = state_ref[h]
        cross = jnp.dot(qd[:, sl].astype(BF16), st.astype(BF16), preferred_element_type=F32)
        y = inner + cross
        mu = jnp.mean(y, axis=-1, keepdims=True)
        yc = y - mu
        var = jnp.mean(yc * yc, axis=-1, keepdims=True)
        ys.append(yc * lax.rsqrt(var + LN_EPS))
        kv = lax.dot_general(kd[:, sl].astype(BF16), vb, _CONTRACT_FIRST,
                             preferred_element_type=F32)
        state_ref[h] = st * chunk_decay[h] + kv
    y = jnp.concatenate(ys, axis=1)
    return y * (gate * jax.nn.sigmoid(gate))


def _ret_tables(S):
    C, H, d = RET_CHUNK, RET_HEADS, HEAD_DIM
    half = d // 2
    inv = 1.0 / (ROPE_BASE ** (np.arange(half, dtype=np.float64) / half))
    ang = np.arange(S, dtype=np.float64)[:, None] * inv[None, :]
    cos_h = np.concatenate([np.cos(ang), np.cos(ang)], axis=1)
    sin_lo = np.concatenate([-np.sin(ang), np.zeros_like(ang)], axis=1)
    sin_hi = np.concatenate([np.zeros_like(ang), np.sin(ang)], axis=1)
    tile = lambda a: np.tile(a, (1, H)).astype(np.float32)
    log_g = np.log(1.0 - 2.0 ** (-5.0 - np.arange(H, dtype=np.float64)))
    idx = np.arange(C, dtype=np.float64)
    diff = idx[:, None] - idx[None, :]
    dmat = np.where(diff >= 0, np.exp(np.maximum(diff, 0.0)[None] * log_g[:, None, None]), 0.0)
    kd = np.exp((C - 1.0 - idx)[:, None] * log_g[None, :])
    qd = np.exp((idx + 1.0)[:, None] * log_g[None, :])
    rep = lambda a: np.repeat(a, d, axis=1).astype(np.float32)
    chunk_decay = tuple(float(v) for v in np.exp(C * log_g))
    return (tile(cos_h), tile(sin_lo), tile(sin_hi), rep(qd), rep(kd),
            dmat.astype(np.float32), chunk_decay)


_CMP_ROWS = 256


def _cmp_kernel(kcn_ref, vcn_ref, pos_ref, wl_ref, kc_ref, vct_ref):
    n, W = _CMP_ROWS, NSA_KV_WIDTH
    res = []
    for i, ref in enumerate((kcn_ref, vcn_ref)):
        out = jnp.zeros((n, 2 * W), F32)
        cst = jnp.zeros((SUBLANES, 2 * W), F32)
        for l in range(CMP_STRIDE):
            x = ref[pl.ds(l, n, stride=CMP_STRIDE), :].astype(BF16)
            out = out + jnp.dot(x, wl_ref[i, l], preferred_element_type=F32)
            cst = cst + jnp.dot(pos_ref[i, l].astype(BF16), wl_ref[i, l],
                                preferred_element_type=F32)
        const = cst[0:1, 0:W] + cst[1:2, W:2 * W]
        res.append(out[:, 0:W] + pltpu.roll(out[:, W:2 * W], n - 1, 0) + const)
    kc_ref[...] = res[0].astype(BF16)
    vct_ref[...] = res[1].T.astype(BF16)


def _prep_cmp_weights(cmp_pos, cmp_w):
    L2, d, G = CMP_STRIDE, HEAD_DIM, NSA_KV_HEADS
    assert G == 2
    w5 = cmp_w.reshape(2, 2, L2, d, d)
    z = jnp.zeros((2, L2, d, d), F32)

    def both_groups(w):
        return jnp.concatenate([jnp.concatenate([w, z], axis=-1),
                                jnp.concatenate([z, w], axis=-1)], axis=-2)

    wl = jnp.concatenate([both_groups(w5[:, 0]), both_groups(w5[:, 1])], axis=-1).astype(BF16)
    p4 = cmp_pos.reshape(2, 2, L2, d)
    prow = jnp.concatenate([p4, p4], axis=-1).transpose(0, 2, 1, 3)
    prow = jnp.pad(prow, ((0, 0), (0, 0), (0, SUBLANES - 2), (0, 0)))
    return prow, wl


def _compress(kcn, vcn, prow, wl):
    B, S, W = kcn.shape
    n = _CMP_ROWS
    return pl.pallas_call(
        _cmp_kernel,
        grid=(B,),
        in_specs=[pl.BlockSpec((None, S, W), lambda b: (b, 0, 0)),
                  pl.BlockSpec((None, S, W), lambda b: (b, 0, 0)),
                  pl.BlockSpec(prow.shape, lambda b: (0, 0, 0, 0)),
                  pl.BlockSpec(wl.shape, lambda b: (0, 0, 0, 0))],
        out_specs=(pl.BlockSpec((None, n, NSA_KV_WIDTH), lambda b: (b, 0, 0)),
                   pl.BlockSpec((None, NSA_KV_WIDTH, n), lambda b: (b, 0, 0))),
        out_shape=(jax.ShapeDtypeStruct((B, n, NSA_KV_WIDTH), BF16),
                   jax.ShapeDtypeStruct((B, NSA_KV_WIDTH, n), BF16)),
        compiler_params=_cparams(("parallel",)),
        name="nsa_compress",
    )(kcn, vcn, prow, wl)


def _nsa_tables(S):
    n_cmp = (S - CMP_LEN) // CMP_STRIDE + 1
    n_sel = S // SEL_LEN
    ci = np.arange(n_cmp)[:, None]
    sj = np.arange(n_sel)[None, :]
    overlap = (np.minimum(ci * CMP_STRIDE + CMP_LEN, sj * SEL_LEN + SEL_LEN)
               - np.maximum(ci * CMP_STRIDE, sj * SEL_LEN))
    sel_map = np.clip(overlap, 0, None) / CMP_STRIDE
    selmap_t = np.zeros((n_sel, _CMP_ROWS), np.float32)
    selmap_t[:, :n_cmp] = sel_map.T
    jl = np.arange(NSA_KB)[:, None]
    tl = np.arange(NSA_TQ)[None, :]
    band = np.stack([np.where(jl <= tl, 0.0, NEG), np.where(jl > tl, 0.0, NEG)]).astype(np.float32)
    rel = (np.arange(2 * _CMP_ROWS)[:, None] - _CMP_ROWS) * CMP_STRIDE + (CMP_LEN - 1)
    cmpmask = np.where(rel <= tl, 0.0, NEG).astype(np.float32)
    return selmap_t, band, cmpmask


def _nsa_kernel(*refs):
    for sub in range(NSA_TILES):
        _nsa_tile(sub, *refs)


def _nsa_tile(sub, qt_ref, kc_ref, vct_ref, ks_ref, vst_ref, kw_ref, vwt_ref, glt_ref, ng_ref,
              selmap_ref, band_ref, cmpmask_ref, x_ref, yret_ref, ylru_ref, wo_ref, lng_ref,
              lnb_ref, out_ref, val_ref, cnt_ref, s_ref, p_ref, acc_ref, st_ref, part_ref):
    TQ, KB, R, d, G = NSA_TQ, NSA_KB, NSA_GROUP, HEAD_DIM, NSA_KV_HEADS
    n_sel = selmap_ref.shape[0]
    qi = pl.program_id(1) * NSA_TILES + sub
    t0 = qi * TQ
    tile = slice(sub * TQ, (sub + 1) * TQ)
    o1, o2 = RET_WIDTH, RET_WIDTH + NSA_WIDTH
    part_ref[...] = (DEEPNORM_ALPHA * x_ref[tile, :]
                     + jnp.dot(yret_ref[tile, :], wo_ref[0:o1, :], preferred_element_type=F32)
                     + jnp.dot(ylru_ref[tile, :], wo_ref[o2:D_MODEL, :],
                               preferred_element_type=F32))
    groups = range(G)
    heads = [slice(r * TQ, (r + 1) * TQ) for r in range(R)]
    qs = [[qt_ref[(g * R + r) * d:(g * R + r + 1) * d, tile] for r in range(R)] for g in groups]
    zq = jnp.zeros((d, TQ), BF16)
    qs_nat = [jnp.concatenate(
        [jnp.concatenate([q if gg == g else zq for gg in groups], axis=0) for q in qs[g]], axis=1)
        for g in groups]

    def key_off(kb):
        return pl.multiple_of(kb * KB, KB)

    n_wb = WINDOW // KB + 1
    w_kb = [qi - (n_wb - 1) + w for w in range(n_wb)]
    w_off = [key_off(jnp.maximum(kb, 0)) for kb in w_kb]
    w_bias = [band_ref[1]] + [None] * (n_wb - 2) + [band_ref[0]]
    n_cmp = kc_ref.shape[0]
    cmp_off = pl.multiple_of(n_cmp - qi * (TQ // CMP_STRIDE), TQ // CMP_STRIDE)
    cmp_bias = jnp.concatenate([cmpmask_ref[pl.ds(cmp_off, n_cmp), :]] * R, axis=1)
    tq = t0 + lax.broadcasted_iota(jnp.int32, (1, TQ), 1)
    seen = jnp.concatenate([tq >= CMP_LEN - 1] * R, axis=1)

    s_cmp = [jnp.dot(kc_ref[...], qs_nat[g], preferred_element_type=F32) + cmp_bias
             for g in groups]
    w_s = []
    for g in groups:
        blocks = []
        for w in range(n_wb):
            s_w = jnp.dot(kw_ref[pl.ds(w_off[w], KB), :], qs_nat[g], preferred_element_type=F32)
            if w_bias[w] is not None:
                s_w = s_w + jnp.concatenate([w_bias[w]] * R, axis=1)
            if w < n_wb - 1:
                s_w = s_w + jnp.where(w_kb[w] >= 0, 0.0, NEG)
            blocks.append(s_w)
        w_s.append(blocks)

    jrow = lax.broadcasted_iota(jnp.int32, (n_sel, 1), 0)
    cur = lax.shift_right_logical(tq, SEL_SHIFT)
    forced = (jrow == 0) | (jrow == cur) | (jrow == cur - 1)
    causal = jrow * SEL_LEN <= tq
    o_c = []
    for g in groups:
        p = jnp.exp2(s_cmp[g] - jnp.max(s_cmp[g], axis=0, keepdims=True))
        p = p * jnp.where(seen, 1.0 / jnp.sum(p, axis=0, keepdims=True), 0.0)
        o_c.append(jnp.dot(vct_ref[g * d:(g + 1) * d, :], p.astype(BF16),
                           preferred_element_type=F32))
        psum = functools.reduce(jnp.add, [p[:, hs] for hs in heads])
        p_hi = psum.astype(BF16)
        p_lo = (psum - p_hi.astype(F32)).astype(BF16)
        imp = (jnp.dot(selmap_ref[...], p_hi, preferred_element_type=F32)
               + jnp.dot(selmap_ref[...], p_lo, preferred_element_type=F32))
        val_ref[g] = jnp.where(forced, FORCE_SCORE, jnp.where(causal, imp, -FORCE_SCORE))
    cnt_ref[...] = jnp.zeros_like(cnt_ref)

    n_live = (qi + 1) * (TQ // SEL_LEN)
    n_grp = n_sel // SUBLANES
    row_groups = [slice(k * SUBLANES, (k + 1) * SUBLANES) for k in range(n_grp)]
    for c in range(n_grp):

        @pl.when(c * SUBLANES < n_live)
        def _():
            own = lax.broadcasted_iota(jnp.int32, (SUBLANES, 1), 0) + c * SUBLANES
            for g in groups:
                vals = [val_ref[g, rows, :] for rows in row_groups]
                cnts = [cnt_ref[g, rows, :] for rows in row_groups]
                for jp in range(c * SUBLANES, (c + 1) * SUBLANES):
                    cand = jnp.broadcast_to(val_ref[g, jp:jp + 1, :], (SUBLANES, TQ))
                    for k in range(n_grp):
                        if k < c:
                            hit = jnp.where(cand > vals[k], 1.0, 0.0)
                        elif k > c:
                            hit = jnp.where(cand >= vals[k], 1.0, 0.0)
                        else:
                            hit = jnp.where(own > jp, jnp.where(cand >= vals[k], 1.0, 0.0),
                                            jnp.where(cand > vals[k], 1.0, 0.0))
                        cnts[k] = cnts[k] + hit
                for rows, cnt in zip(row_groups, cnts):
                    cnt_ref[g, rows, :] = cnt

    qs_sel = []
    for g in groups:
        bias = jnp.where(cnt_ref[g] < float(N_SELECT), 0.0, NEG).astype(BF16)
        qs_sel.append([jnp.concatenate([q, bias], axis=0) for q in qs[g]])

    def sel_scores(g, kb, r):
        return jnp.dot(ks_ref[g, pl.ds(key_off(kb), KB), :], qs_sel[g][r],
                       preferred_element_type=F32)

    def sel_values(g, kb, p):
        return jnp.dot(vst_ref[g * V_ROWS:(g + 1) * V_ROWS, pl.ds(key_off(kb), KB)], p,
                       preferred_element_type=F32)

    ROW_M, ROW_A = 0, 1

    def sel_stage(i, cur_slot, *, diag, mxu_first=False):
        nxt = 1 - cur_slot
        chains = [(g, r, hs) for g in groups for r, hs in enumerate(heads)]
        pvs = {}
        for g, r, hs in chains:
            if mxu_first and r == 0:
                for r2, h2 in enumerate(heads):
                    pvs[g, r2] = sel_values(g, jnp.maximum(i - 1, 0), p_ref[g, nxt, :, h2])
                for r2, h2 in enumerate(heads):
                    s_ref[g, nxt, :, h2] = sel_scores(g, i + 1, r2)
            s = s_ref[g, cur_slot, :, hs]
            if diag:
                s = s + band_ref[0]
            m = st_ref[g, ROW_M:ROW_M + 1, hs]
            m_new = jnp.maximum(m, jnp.max(s, axis=0, keepdims=True))
            p_ref[g, cur_slot, :, hs] = jnp.exp2(s - m_new).astype(BF16)
            if mxu_first:
                pv_prev = pvs[g, r]
            else:
                pv_prev = sel_values(g, jnp.maximum(i - 1, 0), p_ref[g, nxt, :, hs])
                if not diag:
                    s_ref[g, nxt, :, hs] = sel_scores(g, i + 1, r)
            acc_ref[g, :, hs] = st_ref[g, ROW_A:ROW_A + 1, hs] * acc_ref[g, :, hs] + pv_prev
            st_ref[g, ROW_M:ROW_M + 1, hs] = m_new
            st_ref[g, ROW_A:ROW_A + 1, hs] = jnp.exp2(m - m_new)

    def sel_finish(i, cur_slot):
        for hs in heads:
            for g in groups:
                acc_ref[g, :, hs] = (st_ref[g, ROW_A:ROW_A + 1, hs] * acc_ref[g, :, hs]
                                     + sel_values(g, i, p_ref[g, cur_slot, :, hs]))

    for g in groups:
        for r, hs in enumerate(heads):
            s_ref[g, 0, :, hs] = sel_scores(g, 0, r)
        p_ref[g, 1] = jnp.zeros(p_ref.shape[2:], BF16)
        acc_ref[g] = jnp.zeros(acc_ref.shape[1:], F32)
        st_ref[g, ROW_M:ROW_M + 1, :] = jnp.full((1, R * TQ), NEG, F32)
        st_ref[g, ROW_A:ROW_A + 1, :] = jnp.ones((1, R * TQ), F32)

    acc_w = []
    for g in groups:
        m_w = functools.reduce(jnp.maximum,
                               [jnp.max(s_w, axis=0, keepdims=True) for s_w in w_s[g]])
        acc_w.append(functools.reduce(jnp.add, [
            jnp.dot(vwt_ref[g * V_ROWS:(g + 1) * V_ROWS, pl.ds(w_off[w], KB)],
                    jnp.exp2(w_s[g][w] - m_w).astype(BF16), preferred_element_type=F32)
            for w in range(n_wb)]))

    def sel_pair(j, c):
        sel_stage(2 * j, 0, diag=False)
        sel_stage(2 * j + 1, 1, diag=False, mxu_first=True)
        return c

    lax.fori_loop(0, lax.shift_right_logical(qi, 1), sel_pair, 0)
    assert NSA_TILES % 2 == 0
    if sub % 2 == 1:
        sel_stage(qi - 1, 0, diag=False)
        sel_stage(qi, 1, diag=True)
        sel_finish(qi, 1)
    else:
        sel_stage(qi, 0, diag=True)
        sel_finish(qi, 0)

    gl = jax.nn.sigmoid(glt_ref[:, tile])
    y_nsa = []
    for g in groups:
        res = []
        for r, hs in enumerate(heads):
            a_s = acc_ref[g, :, hs]
            a_w = acc_w[g][:, hs]
            row0 = g * GL_ROWS + r * N_BRANCH
            gate = [gl[row0 + br:row0 + br + 1, :] for br in range(N_BRANCH)]
            res.append(gate[0] * o_c[g][:, hs]
                       + (gate[1] / a_s[d:d + 1, :]) * a_s[0:d, :]
                       + (gate[2] / a_w[d:d + 1, :]) * a_w[0:d, :])
        ng = ng_ref[tile, g * R * d:(g + 1) * R * d]
        y_nsa.append((jnp.concatenate(res, axis=0).T * (ng * jax.nn.sigmoid(ng))).astype(BF16))

    z = part_ref[...] + jnp.dot(jnp.concatenate(y_nsa, axis=1), wo_ref[o1:o2, :],
                                preferred_element_type=F32)
    mu = jnp.mean(z, axis=-1, keepdims=True)
    zc = z - mu
    var = jnp.mean(zc * zc, axis=-1, keepdims=True)
    out_ref[tile, :] = zc * lax.rsqrt(var + LN_EPS) * lng_ref[...] + lnb_ref[...]


def _nsa_attention(qt, kc, vct, ks, vst, kw, vwt, glt, ng, x, y_ret, y_lru, w_out, ln_g, ln_b):
    B, _, S = qt.shape
    G, R, d, TQ = NSA_KV_HEADS, NSA_GROUP, HEAD_DIM, NSA_TQ
    selmap_t, band, cmpmask = _nsa_tables(S)
    n_sel = S // SEL_LEN
    assert NSA_KB == TQ and WINDOW % NSA_KB == 0
    assert n_sel == d
    per_b = lambda *shape: pl.BlockSpec((None,) + shape, lambda b, q: (b,) + (0,) * len(shape))
    fixed = lambda a: pl.BlockSpec(a.shape, lambda b, q: (0,) * a.ndim)
    step = NSA_TILES * TQ
    rows = lambda w: pl.BlockSpec((None, step, w), lambda b, q: (b, q, 0))
    return pl.pallas_call(
        _nsa_kernel,
        grid=(B, S // step),
        in_specs=[pl.BlockSpec((None, G * R * d, step), lambda b, q: (b, 0, q)),
                  per_b(_CMP_ROWS, G * d), per_b(G * d, _CMP_ROWS),
                  per_b(G, S, 2 * d), per_b(G * V_ROWS, S), per_b(S, G * d), per_b(G * V_ROWS, S),
                  pl.BlockSpec((None, G * GL_ROWS, step), lambda b, q: (b, 0, q)),
                  rows(G * R * d), fixed(selmap_t), fixed(band), fixed(cmpmask),
                  rows(D_MODEL), rows(RET_WIDTH), rows(LRU_WIDTH),
                  pl.BlockSpec((D_MODEL, D_MODEL), lambda b, q: (0, 0)),
                  pl.BlockSpec((1, D_MODEL), lambda b, q: (0, 0)),
                  pl.BlockSpec((1, D_MODEL), lambda b, q: (0, 0))],
        out_specs=rows(D_MODEL),
        out_shape=jax.ShapeDtypeStruct((B, S, D_MODEL), F32),
        scratch_shapes=[pltpu.VMEM((G, n_sel, TQ), F32), pltpu.VMEM((G, n_sel, TQ), F32),
                        pltpu.VMEM((G, 2, NSA_KB, R * TQ), F32),
                        pltpu.VMEM((G, 2, NSA_KB, R * TQ), BF16),
                        pltpu.VMEM((G, V_ROWS, R * TQ), F32),
                        pltpu.VMEM((G, SUBLANES, R * TQ), F32),
                        pltpu.VMEM((TQ, D_MODEL), F32)],
        compiler_params=_cparams(("parallel", "arbitrary")),
        name="nsa_attention_out",
    )(qt, kc, vct, ks, vst, kw, vwt, glt, ng, jnp.asarray(selmap_t, dtype=BF16), jnp.asarray(band),
      jnp.asarray(cmpmask), x, y_ret, y_lru, w_out.astype(BF16), ln_g.reshape(1, D_MODEL),
      ln_b.reshape(1, D_MODEL))


def _lru_tile(lru, cw_ref, cb_ref, wa_ref, ba_ref, wx_ref, bx_ref, lam_ref, xext_ref, h_ref):
    ts, W, pad = LRU_TS, LRU_WIDTH, SUBLANES
    x = lru[:, 0:W]
    gate = lru[:, W:2 * W]
    xext_ref[pad:pad + ts, :] = x
    first = pad - (CONV_WIDTH - 1)
    xc = cb_ref[...] + cw_ref[0:1, :] * xext_ref[first:first + ts, :]
    for w in range(1, CONV_WIDTH):
        xc = xc + cw_ref[w:w + 1, :] * xext_ref[first + w:first + w + ts, :]
    xext_ref[0:pad, :] = x[ts - pad:ts, :]
    xcb = xc.astype(BF16)
    r = jax.nn.sigmoid(jnp.dot(xcb, wa_ref[...], preferred_element_type=F32) + ba_ref[...])
    gi = jax.nn.sigmoid(jnp.dot(xcb, wx_ref[...], preferred_element_type=F32) + bx_ref[...])
    nl = -lam_ref[...]
    softplus = jnp.maximum(nl, 0.0) + jnp.log1p(jnp.exp(-jnp.abs(nl)))
    log_a = (-LRU_C) * r * softplus
    a = jnp.exp(log_a)
    u = jnp.sqrt(-jnp.tanh(log_a) * (a * a + 1.0)) * (gi * xc)
    row = lax.broadcasted_iota(jnp.int32, (ts, 1), 0)
    step = 1
    while step < ts:
        if step < SUBLANES:
            keep = row >= step
            a_sh = jnp.where(keep, pltpu.roll(a, step, 0), 1.0)
            u_sh = jnp.where(keep, pltpu.roll(u, step, 0), 0.0)
        else:
            a_sh = jnp.concatenate([jnp.ones((step, W), F32), a[:ts - step, :]], axis=0)
            u_sh = jnp.concatenate([jnp.zeros((step, W), F32), u[:ts - step, :]], axis=0)
        u = a * u_sh + u
        a = a * a_sh
        step *= 2
    h = u + a * h_ref[0:1, :]
    h_ref[0:1, :] = h[ts - 1:ts, :]
    return h * (gate * jax.nn.sigmoid(gate))


def _block_diag(w):
    n, d, _ = w.shape
    z = jnp.zeros((d, d), w.dtype)
    return jnp.concatenate(
        [jnp.concatenate([w[i] if j == i else z for j in range(n)], axis=1) for i in range(n)],
        axis=0)


def _layer(x, w_in, w_out, ln_g, ln_b, cmp_pos, cmp_w, conv_w, conv_b, w_a, b_a, w_x, b_x, lam):
    wn, wt = _prep_in_weights(w_in)
    y_ret, y_lru, kcn, vcn, ks, kw, ng, qt, vst, vwt, glt = _project_and_mix(
        x, wn, wt, conv_w, conv_b, w_a, b_a, w_x, b_x, lam)
    prow, wl = _prep_cmp_weights(cmp_pos, cmp_w)
    kc, vct = _compress(kcn, vcn, prow, wl)
    return _nsa_attention(qt, kc, vct, ks, vst, kw, vwt, glt, ng, x, y_ret, y_lru, w_out,
                               ln_g, ln_b)


def kernel(x, w_in, w_out, ln_g, ln_b, nsa_cmp_pos, nsa_cmp_w, lru_conv_w, lru_conv_b,
           lru_w_a, lru_b_a, lru_w_x, lru_b_x, lru_lambda):
    assert x.shape[1] % max(PROJ_TM, RET_CHUNK, LRU_TS, NSA_TILES * NSA_TQ) == 0
    assert x.shape[1] // CMP_STRIDE == _CMP_ROWS
    for l in range(DEPTH):
        x = _layer(x, w_in[l], w_out[l], ln_g[l], ln_b[l], nsa_cmp_pos[l], nsa_cmp_w[l],
                   lru_conv_w[l], lru_conv_b[l], lru_w_a[l], lru_b_a[l], lru_w_x[l], lru_b_x[l],
                   lru_lambda[l])
    return x
```

```python
import functools
import math

import jax
import jax.numpy as jnp
import numpy as np
from jax import lax
from jax.experimental import pallas as pl
from jax.experimental.pallas import tpu as pltpu

F32 = jnp.float32
BF16 = jnp.bfloat16

D_MODEL = 1024
DEPTH = 2
HEAD_DIM = 64
RET_HEADS = 4
RET_WIDTH = RET_HEADS * HEAD_DIM
ROPE_BASE = 10000.0
NSA_HEADS = 8
NSA_KV_HEADS = 2
NSA_GROUP = NSA_HEADS // NSA_KV_HEADS
NSA_WIDTH = NSA_HEADS * HEAD_DIM
NSA_KV_WIDTH = NSA_KV_HEADS * HEAD_DIM
CMP_LEN = 32
CMP_STRIDE = 16
SEL_LEN = 64
N_SELECT = 16
WINDOW = 512
N_BRANCH = 3
FORCE_SCORE = 1.0e4
LRU_WIDTH = 256
LRU_BLOCKS = 4
LRU_BLOCK_DIM = LRU_WIDTH // LRU_BLOCKS
CONV_WIDTH = 4
LRU_C = 8.0
DEEPNORM_ALPHA = (2.0 * DEPTH) ** 0.25
LN_EPS = 1e-5
NEG = -1e30
LOG2E = math.log2(math.e)
SEL_SHIFT = int(math.log2(SEL_LEN))

IN_SIZES = (RET_WIDTH,) * 4 + (NSA_WIDTH,) + (NSA_KV_WIDTH,) * 6 + (
    NSA_WIDTH, NSA_HEADS * N_BRANCH, LRU_WIDTH, LRU_WIDTH)
IN_OFFS = tuple(int(v) for v in np.cumsum((0,) + IN_SIZES))

V7X_VMEM_LIMIT = 48 * 1024 * 1024
SUBLANES = 8
LANES = 128
BF16_SUBLANES = 16
PROJ_TM = 512
RET_CHUNK = 256
LRU_TS = 256
NSA_TQ = 256
NSA_KB = 256
NSA_TILES = 2
GL_ROWS = 16
V_ROWS = HEAD_DIM + BF16_SUBLANES

_CONTRACT_LAST = (((1,), (1,)), ((), ()))
_CONTRACT_FIRST = (((0,), (0,)), ((), ()))


def _cparams(sem):
    return pltpu.CompilerParams(dimension_semantics=sem, vmem_limit_bytes=V7X_VMEM_LIMIT)


_WN_COLS = (4 * RET_WIDTH, 2 * NSA_KV_WIDTH, 2 * NSA_KV_WIDTH, NSA_WIDTH, 2 * LRU_WIDTH)
_WN_OFFS = tuple(int(v) for v in np.cumsum((0,) + _WN_COLS))
_WT_ROWS = (NSA_WIDTH, NSA_KV_WIDTH, NSA_KV_WIDTH, NSA_KV_HEADS * GL_ROWS)
_WT_OFFS = tuple(int(v) for v in np.cumsum((0,) + _WT_ROWS))


def _proj_kernel(x_ref, wn_ref, wt_ref,
                 cos_ref, sa_ref, sb_ref, qd_ref, kd_ref, dmat_ref,
                 cw_ref, cb_ref, wa_ref, ba_ref, wx_ref, bx_ref, lam_ref,
                 yret_ref, ylru_ref, kcn_ref, vcn_ref, ks_ref, kw_ref, ng_ref,
                 qt_ref, vst_ref, vwt_ref, glt_ref,
                 state_ref, xext_ref, h_ref, *, chunk_decay):
    tm = x_ref.shape[0]

    @pl.when(pl.program_id(1) == 0)
    def _():
        state_ref[...] = jnp.zeros_like(state_ref)
        xext_ref[0:SUBLANES, :] = jnp.zeros((SUBLANES, LRU_WIDTH), F32)
        h_ref[...] = jnp.zeros_like(h_ref)

    xb = x_ref[...].astype(BF16)

    def nat(i):
        return jnp.dot(xb, wn_ref[:, _WN_OFFS[i]:_WN_OFFS[i + 1]], preferred_element_type=F32)

    def tr(i):
        return lax.dot_general(wt_ref[_WT_OFFS[i]:_WT_OFFS[i + 1], :], xb, _CONTRACT_LAST,
                               preferred_element_type=F32)

    def nat_rows(i, rows):
        return jnp.dot(xb[rows, :], wn_ref[:, _WN_OFFS[i]:_WN_OFFS[i + 1]],
                       preferred_element_type=F32)

    assert RET_CHUNK == LRU_TS
    chunks = list(range(0, tm, RET_CHUNK))
    ret, lru = {}, {}

    def project_chunk(c0):
        rows = slice(c0, c0 + RET_CHUNK)
        ret[c0] = nat_rows(0, rows)
        lru[c0] = nat_rows(4, rows)

    def mix_retention(c0):
        if c0 + RET_CHUNK < tm:
            project_chunk(c0 + RET_CHUNK)
        rows = slice(c0, c0 + RET_CHUNK)
        yret_ref[rows, :] = _retention_chunk(
            ret[c0], cos_ref[rows, :], sa_ref[rows, :], sb_ref[rows, :], qd_ref[...],
            kd_ref[...], dmat_ref, state_ref, chunk_decay).astype(yret_ref.dtype)

    def mix_lru(c0):
        rows = slice(c0, c0 + LRU_TS)
        ylru_ref[rows, :] = _lru_tile(lru[c0], cw_ref, cb_ref, wa_ref, ba_ref, wx_ref, bx_ref,
                                      lam_ref, xext_ref, h_ref).astype(ylru_ref.dtype)

    project_chunk(0)

    def proj_keys():
        kvc = nat(1)
        kcn_ref[...] = kvc[:, 0:NSA_KV_WIDTH]
        vcn_ref[...] = kvc[:, NSA_KV_WIDTH:2 * NSA_KV_WIDTH]
        ksw = nat(2)
        ks = ksw[:, 0:NSA_KV_WIDTH]
        key = pl.program_id(1) * tm + lax.broadcasted_iota(jnp.int32, (tm, 1), 0)
        blk = lax.broadcasted_iota(jnp.int32, (1, HEAD_DIM), 1)
        onehot = jnp.where(lax.shift_right_logical(key, SEL_SHIFT) == blk, 1.0, 0.0)
        for g in range(NSA_KV_HEADS):
            ks_ref[g] = jnp.concatenate([ks[:, g * HEAD_DIM:(g + 1) * HEAD_DIM], onehot],
                                        axis=1).astype(BF16)
        kw_ref[...] = ksw[:, NSA_KV_WIDTH:2 * NSA_KV_WIDTH].astype(BF16)

    def proj_gate():
        ng_ref[...] = nat(3)

    def proj_queries():
        qt_ref[...] = (tr(0) * LOG2E).astype(BF16)

    def proj_values():
        ones_rows = jnp.where(lax.broadcasted_iota(jnp.int32, (BF16_SUBLANES, tm), 0) == 0,
                              1.0, 0.0).astype(BF16)
        for ref, i in ((vst_ref, 1), (vwt_ref, 2)):
            vt = tr(i)
            for g in range(NSA_KV_HEADS):
                ref[g * V_ROWS:g * V_ROWS + HEAD_DIM, :] = (
                    vt[g * HEAD_DIM:(g + 1) * HEAD_DIM, :].astype(BF16))
                ref[g * V_ROWS + HEAD_DIM:(g + 1) * V_ROWS, :] = ones_rows
        glt_ref[...] = tr(3)

    mixers = [functools.partial(mix, c0) for c0 in chunks for mix in (mix_retention, mix_lru)]
    projections = [proj_keys, proj_gate, proj_queries, proj_values]
    for step in range(max(len(mixers), len(projections))):
        if step < len(mixers):
            mixers[step]()
        if step < len(projections):
            projections[step]()


def _project_and_mix(x, layer, wn, wt, conv_w, conv_b, wa, b_a, wx, b_x, lam):
    B, S, _ = x.shape
    tm = PROJ_TM
    nn, nt = wn.shape[-1], wt.shape[-2]
    G, C, W, LW = NSA_KV_HEADS, RET_CHUNK, RET_WIDTH, LRU_WIDTH
    cos, sa, sb, qd, kd, dmat, chunk_decay = _ret_tables(S)
    row = lambda w: pl.BlockSpec((None, tm, w), lambda b, i: (b, i, 0))
    col = lambda r: pl.BlockSpec((None, r, tm), lambda b, i: (b, 0, i))
    kspec = pl.BlockSpec((None, G, tm, 2 * HEAD_DIM), lambda b, i: (b, 0, i, 0))
    fixed = lambda *shape: pl.BlockSpec(shape, lambda b, i: (0,) * len(shape))
    of_layer = lambda *shape: pl.BlockSpec((None,) + shape,
                                           lambda b, i: (layer,) + (0,) * len(shape))
    pos = pl.BlockSpec((tm, W), lambda b, i: (i, 0))
    out_shape = (
        jax.ShapeDtypeStruct((B, S, W), BF16),
        jax.ShapeDtypeStruct((B, S, LW), BF16),
        jax.ShapeDtypeStruct((B, S, NSA_KV_WIDTH), F32),
        jax.ShapeDtypeStruct((B, S, NSA_KV_WIDTH), F32),
        jax.ShapeDtypeStruct((B, G, S, 2 * HEAD_DIM), BF16),
        jax.ShapeDtypeStruct((B, S, NSA_KV_WIDTH), BF16),
        jax.ShapeDtypeStruct((B, S, NSA_WIDTH), F32),
        jax.ShapeDtypeStruct((B, NSA_WIDTH, S), BF16),
        jax.ShapeDtypeStruct((B, G * V_ROWS, S), BF16),
        jax.ShapeDtypeStruct((B, G * V_ROWS, S), BF16),
        jax.ShapeDtypeStruct((B, G * GL_ROWS, S), F32),
    )
    return pl.pallas_call(
        functools.partial(_proj_kernel, chunk_decay=chunk_decay),
        grid=(B, S // tm),
        in_specs=[row(D_MODEL), of_layer(D_MODEL, nn), of_layer(nt, D_MODEL),
                  pos, pos, pos, fixed(C, W), fixed(C, W), fixed(RET_HEADS, C, C),
                  of_layer(CONV_WIDTH, LW), of_layer(1, LW), of_layer(LW, LW), of_layer(1, LW),
                  of_layer(LW, LW), of_layer(1, LW), of_layer(1, LW)],
        out_specs=(row(W), row(LW), row(NSA_KV_WIDTH), row(NSA_KV_WIDTH), kspec,
                   row(NSA_KV_WIDTH), row(NSA_WIDTH), col(NSA_WIDTH), col(G * V_ROWS),
                   col(G * V_ROWS), col(G * GL_ROWS)),
        out_shape=out_shape,
        scratch_shapes=[pltpu.VMEM((RET_HEADS, HEAD_DIM, HEAD_DIM), F32),
                        pltpu.VMEM((LRU_TS + SUBLANES, LW), F32), pltpu.VMEM((SUBLANES, LW), F32)],
        compiler_params=_cparams(("parallel", "arbitrary")),
        name="in_proj_mix",
    )(x, wn, wt, jnp.asarray(cos), jnp.asarray(sa), jnp.asarray(sb), jnp.asarray(qd),
      jnp.asarray(kd), jnp.asarray(dmat), conv_w, conv_b, wa, b_a, wx, b_x, lam)


def _prep_in_weights(w_in):
    L = w_in.shape[0]
    seg = [w_in[..., IN_OFFS[i]:IN_OFFS[i + 1]] for i in range(len(IN_SIZES))]
    (rq, rk, rv, rg, nq, nkc, nvc, nks, nvs, nkw, nvw, ng, ngl, lx, lg) = seg
    wn = jnp.concatenate([rq, rk, rv, rg, nkc, nvc, nks, nkw, ng, lx, lg], axis=-1).astype(BF16)
    tr = lambda a: jnp.swapaxes(a, -1, -2)
    glt = tr(ngl).reshape(L, NSA_KV_HEADS, NSA_GROUP * N_BRANCH, D_MODEL)
    glt = jnp.pad(glt, ((0, 0), (0, 0), (0, GL_ROWS - NSA_GROUP * N_BRANCH), (0, 0)))
    wt = jnp.concatenate([tr(nq) * (HEAD_DIM ** -0.5), tr(nvs), tr(nvw),
                          glt.reshape(L, NSA_KV_HEADS * GL_ROWS, D_MODEL)], axis=-2).astype(BF16)
    return wn, wt


def _retention_chunk(ret, cos, sa, sb, qdec, kdec, dmat_ref, state_ref, chunk_decay):
    W = RET_WIDTH
    q = ret[:, 0:W]
    k = ret[:, W:2 * W]
    v = ret[:, 2 * W:3 * W]
    gate = ret[:, 3 * W:4 * W]
    half = HEAD_DIM // 2

    def rope(t):
        cols = []
        for c0 in range(0, W, LANES):
            tc = t[:, c0:c0 + LANES]
            cols.append(tc * cos[:, c0:c0 + LANES]
                        + pltpu.roll(tc, LANES - half, 1) * sa[:, c0:c0 + LANES]
                        + pltpu.roll(tc, half, 1) * sb[:, c0:c0 + LANES])
        return jnp.concatenate(cols, axis=1)

    qr = rope(q)
    kr = rope(k) * (HEAD_DIM ** -0.5)
    qd = qr * qdec
    kd = kr * kdec
    ys = []
    for h in range(RET_HEADS):
        sl = slice(h * HEAD_DIM, (h + 1) * HEAD_DIM)
        qb, kb, vb = qr[:, sl].astype(BF16), kr[:, sl].astype(BF16), v[:, sl].astype(BF16)
        s = lax.dot_general(qb, kb, _CONTRACT_LAST, preferred_element_type=F32) * dmat_ref[h]
        inner = jnp.dot(s.astype(BF16), vb, preferred_element_type=F32)
        st = state_ref[h]
        cross = jnp.dot(qd[:, sl].astype(BF16), st.astype(BF16), preferred_element_type=F32)
        y = inner + cross
        mu = jnp.mean(y, axis=-1, keepdims=True)
        yc = y - mu
        var = jnp.mean(yc * yc, axis=-1, keepdims=True)
        ys.append(yc * lax.rsqrt(var + LN_EPS))
        kv = lax.dot_general(kd[:, sl].astype(BF16), vb, _CONTRACT_FIRST,
                             preferred_element_type=F32)
        state_ref[h] = st * chunk_decay[h] + kv
    y = jnp.concatenate(ys, axis=1)
    return y * (gate * jax.nn.sigmoid(gate))


def _ret_tables(S):
    C, H, d = RET_CHUNK, RET_HEADS, HEAD_DIM
    half = d // 2
    inv = 1.0 / (ROPE_BASE ** (np.arange(half, dtype=np.float64) / half))
    ang = np.arange(S, dtype=np.float64)[:, None] * inv[None, :]
    cos_h = np.concatenate([np.cos(ang), np.cos(ang)], axis=1)
    sin_lo = np.concatenate([-np.sin(ang), np.zeros_like(ang)], axis=1)
    sin_hi = np.concatenate([np.zeros_like(ang), np.sin(ang)], axis=1)
    tile = lambda a: np.tile(a, (1, H)).astype(np.float32)
    log_g = np.log(1.0 - 2.0 ** (-5.0 - np.arange(H, dtype=np.float64)))
    idx = np.arange(C, dtype=np.float64)
    diff = idx[:, None] - idx[None, :]
    dmat = np.where(diff >= 0, np.exp(np.maximum(diff, 0.0)[None] * log_g[:, None, None]), 0.0)
    kd = np.exp((C - 1.0 - idx)[:, None] * log_g[None, :])
    qd = np.exp((idx + 1.0)[:, None] * log_g[None, :])
    rep = lambda a: np.repeat(a, d, axis=1).astype(np.float32)
    chunk_decay = tuple(float(v) for v in np.exp(C * log_g))
    return (tile(cos_h), tile(sin_lo), tile(sin_hi), rep(qd), rep(kd),
            dmat.astype(np.float32), chunk_decay)


_CMP_ROWS = 256


def _cmp_kernel(kcn_ref, vcn_ref, pos_ref, wl_ref, kc_ref, vct_ref):
    n, W = _CMP_ROWS, NSA_KV_WIDTH
    res = []
    for i, ref in enumerate((kcn_ref, vcn_ref)):
        out = jnp.zeros((n, 2 * W), F32)
        cst = jnp.zeros((SUBLANES, 2 * W), F32)
        for l in range(CMP_STRIDE):
            x = ref[pl.ds(l, n, stride=CMP_STRIDE), :].astype(BF16)
            out = out + jnp.dot(x, wl_ref[i, l], preferred_element_type=F32)
            cst = cst + jnp.dot(pos_ref[i, l].astype(BF16), wl_ref[i, l],
                                preferred_element_type=F32)
        const = cst[0:1, 0:W] + cst[1:2, W:2 * W]
        res.append(out[:, 0:W] + pltpu.roll(out[:, W:2 * W], n - 1, 0) + const)
    kc_ref[...] = res[0].astype(BF16)
    vct_ref[...] = res[1].T.astype(BF16)


def _prep_cmp_weights(cmp_pos, cmp_w):
    L2, d, G = CMP_STRIDE, HEAD_DIM, NSA_KV_HEADS
    assert G == 2
    L = cmp_w.shape[0]
    w5 = cmp_w.reshape(L, 2, 2, L2, d, d)
    z = jnp.zeros((L, 2, L2, d, d), F32)

    def both_groups(w):
        return jnp.concatenate([jnp.concatenate([w, z], axis=-1),
                                jnp.concatenate([z, w], axis=-1)], axis=-2)

    wl = jnp.concatenate([both_groups(w5[:, :, 0]), both_groups(w5[:, :, 1])],
                         axis=-1).astype(BF16)
    p4 = cmp_pos.reshape(L, 2, 2, L2, d)
    prow = jnp.concatenate([p4, p4], axis=-1).transpose(0, 1, 3, 2, 4)
    prow = jnp.pad(prow, ((0, 0), (0, 0), (0, 0), (0, SUBLANES - 2), (0, 0)))
    return prow, wl


def _compress(kcn, vcn, layer, prow, wl):
    B, S, W = kcn.shape
    n = _CMP_ROWS
    of_layer = lambda a: pl.BlockSpec((None,) + a.shape[1:],
                                      lambda b: (layer,) + (0,) * (a.ndim - 1))
    return pl.pallas_call(
        _cmp_kernel,
        grid=(B,),
        in_specs=[pl.BlockSpec((None, S, W), lambda b: (b, 0, 0)),
                  pl.BlockSpec((None, S, W), lambda b: (b, 0, 0)),
                  of_layer(prow), of_layer(wl)],
        out_specs=(pl.BlockSpec((None, n, NSA_KV_WIDTH), lambda b: (b, 0, 0)),
                   pl.BlockSpec((None, NSA_KV_WIDTH, n), lambda b: (b, 0, 0))),
        out_shape=(jax.ShapeDtypeStruct((B, n, NSA_KV_WIDTH), BF16),
                   jax.ShapeDtypeStruct((B, NSA_KV_WIDTH, n), BF16)),
        compiler_params=_cparams(("parallel",)),
        name="nsa_compress",
    )(kcn, vcn, prow, wl)


def _nsa_tables(S):
    n_cmp = (S - CMP_LEN) // CMP_STRIDE + 1
    n_sel = S // SEL_LEN
    ci = np.arange(n_cmp)[:, None]
    sj = np.arange(n_sel)[None, :]
    overlap = (np.minimum(ci * CMP_STRIDE + CMP_LEN, sj * SEL_LEN + SEL_LEN)
               - np.maximum(ci * CMP_STRIDE, sj * SEL_LEN))
    sel_map = np.clip(overlap, 0, None) / CMP_STRIDE
    selmap_t = np.zeros((n_sel, _CMP_ROWS), np.float32)
    selmap_t[:, :n_cmp] = sel_map.T
    jl = np.arange(NSA_KB)[:, None]
    tl = np.arange(NSA_TQ)[None, :]
    band = np.stack([np.where(jl <= tl, 0.0, NEG), np.where(jl > tl, 0.0, NEG)]).astype(np.float32)
    rel = (np.arange(2 * _CMP_ROWS)[:, None] - _CMP_ROWS) * CMP_STRIDE + (CMP_LEN - 1)
    cmpmask = np.where(rel <= tl, 0.0, NEG).astype(np.float32)
    return selmap_t, band, cmpmask


def _nsa_kernel(*refs):
    for sub in range(NSA_TILES):
        _nsa_tile(sub, *refs)


def _nsa_tile(sub, qt_ref, kc_ref, vct_ref, ks_ref, vst_ref, kw_ref, vwt_ref, glt_ref, ng_ref,
              selmap_ref, band_ref, cmpmask_ref, x_ref, yret_ref, ylru_ref, wo_ref, lng_ref,
              lnb_ref, out_ref, val_ref, cnt_ref, s_ref, p_ref, acc_ref, st_ref, part_ref):
    TQ, KB, R, d, G = NSA_TQ, NSA_KB, NSA_GROUP, HEAD_DIM, NSA_KV_HEADS
    n_sel = selmap_ref.shape[0]
    qi = pl.program_id(1) * NSA_TILES + sub
    t0 = qi * TQ
    tile = slice(sub * TQ, (sub + 1) * TQ)
    o1, o2 = RET_WIDTH, RET_WIDTH + NSA_WIDTH
    part_ref[...] = (DEEPNORM_ALPHA * x_ref[tile, :]
                     + jnp.dot(yret_ref[tile, :], wo_ref[0:o1, :], preferred_element_type=F32)
                     + jnp.dot(ylru_ref[tile, :], wo_ref[o2:D_MODEL, :],
                               preferred_element_type=F32))
    groups = range(G)
    heads = [slice(r * TQ, (r + 1) * TQ) for r in range(R)]
    qs = [[qt_ref[(g * R + r) * d:(g * R + r + 1) * d, tile] for r in range(R)] for g in groups]
    zq = jnp.zeros((d, TQ), BF16)
    qs_nat = [jnp.concatenate(
        [jnp.concatenate([q if gg == g else zq for gg in groups], axis=0) for q in qs[g]], axis=1)
        for g in groups]

    def key_off(kb):
        return pl.multiple_of(kb * KB, KB)

    n_wb = WINDOW // KB + 1
    w_kb = [qi - (n_wb - 1) + w for w in range(n_wb)]
    w_off = [key_off(jnp.maximum(kb, 0)) for kb in w_kb]
    w_bias = [band_ref[1]] + [None] * (n_wb - 2) + [band_ref[0]]
    n_cmp = kc_ref.shape[0]
    cmp_off = pl.multiple_of(n_cmp - qi * (TQ // CMP_STRIDE), TQ // CMP_STRIDE)
    cmp_bias = jnp.concatenate([cmpmask_ref[pl.ds(cmp_off, n_cmp), :]] * R, axis=1)
    tq = t0 + lax.broadcasted_iota(jnp.int32, (1, TQ), 1)
    seen = jnp.concatenate([tq >= CMP_LEN - 1] * R, axis=1)

    s_cmp = [jnp.dot(kc_ref[...], qs_nat[g], preferred_element_type=F32) + cmp_bias
             for g in groups]
    w_s = []
    for g in groups:
        blocks = []
        for w in range(n_wb):
            s_w = jnp.dot(kw_ref[pl.ds(w_off[w], KB), :], qs_nat[g], preferred_element_type=F32)
            if w_bias[w] is not None:
                s_w = s_w + jnp.concatenate([w_bias[w]] * R, axis=1)
            if w < n_wb - 1:
                s_w = s_w + jnp.where(w_kb[w] >= 0, 0.0, NEG)
            blocks.append(s_w)
        w_s.append(blocks)

    jrow = lax.broadcasted_iota(jnp.int32, (n_sel, 1), 0)
    cur = lax.shift_right_logical(tq, SEL_SHIFT)
    forced = (jrow == 0) | (jrow == cur) | (jrow == cur - 1)
    causal = jrow * SEL_LEN <= tq
    o_c = []
    for g in groups:
        p = jnp.exp2(s_cmp[g] - jnp.max(s_cmp[g], axis=0, keepdims=True))
        p = p * jnp.where(seen, 1.0 / jnp.sum(p, axis=0, keepdims=True), 0.0)
        o_c.append(jnp.dot(vct_ref[g * d:(g + 1) * d, :], p.astype(BF16),
                           preferred_element_type=F32))
        psum = functools.reduce(jnp.add, [p[:, hs] for hs in heads])
        p_hi = psum.astype(BF16)
        p_lo = (psum - p_hi.astype(F32)).astype(BF16)
        imp = (jnp.dot(selmap_ref[...], p_hi, preferred_element_type=F32)
               + jnp.dot(selmap_ref[...], p_lo, preferred_element_type=F32))
        val_ref[g] = jnp.where(forced, FORCE_SCORE, jnp.where(causal, imp, -FORCE_SCORE))
    cnt_ref[...] = jnp.zeros_like(cnt_ref)

    n_live = (qi + 1) * (TQ // SEL_LEN)
    n_grp = n_sel // SUBLANES
    row_groups = [slice(k * SUBLANES, (k + 1) * SUBLANES) for k in range(n_grp)]
    for c in range(n_grp):

        @pl.when(c * SUBLANES < n_live)
        def _():
            own = lax.broadcasted_iota(jnp.int32, (SUBLANES, 1), 0) + c * SUBLANES
            for g in groups:
                vals = [val_ref[g, rows, :] for rows in row_groups]
                cnts = [cnt_ref[g, rows, :] for rows in row_groups]
                for jp in range(c * SUBLANES, (c + 1) * SUBLANES):
                    cand = jnp.broadcast_to(val_ref[g, jp:jp + 1, :], (SUBLANES, TQ))
                    for k in range(n_grp):
                        if k < c:
                            hit = jnp.where(cand > vals[k], 1.0, 0.0)
                        elif k > c:
                            hit = jnp.where(cand >= vals[k], 1.0, 0.0)
                        else:
                            hit = jnp.where(own > jp, jnp.where(cand >= vals[k], 1.0, 0.0),
                                            jnp.where(cand > vals[k], 1.0, 0.0))
                        cnts[k] = cnts[k] + hit
                for rows, cnt in zip(row_groups, cnts):
                    cnt_ref[g, rows, :] = cnt

    qs_sel = []
    for g in groups:
        bias = jnp.where(cnt_ref[g] < float(N_SELECT), 0.0, NEG).astype(BF16)
        qs_sel.append([jnp.concatenate([q, bias], axis=0) for q in qs[g]])

    def sel_scores(g, kb, r):
        return jnp.dot(ks_ref[g, pl.ds(key_off(kb), KB), :], qs_sel[g][r],
                       preferred_element_type=F32)

    def sel_values(g, kb, p):
        return jnp.dot(vst_ref[g * V_ROWS:(g + 1) * V_ROWS, pl.ds(key_off(kb), KB)], p,
                       preferred_element_type=F32)

    ROW_M, ROW_A = 0, 1

    def sel_stage(i, cur_slot, *, diag, mxu_first=False):
        nxt = 1 - cur_slot
        chains = [(g, r, hs) for g in groups for r, hs in enumerate(heads)]
        pvs = {}
        for g, r, hs in chains:
            if mxu_first and r == 0:
                for r2, h2 in enumerate(heads):
                    pvs[g, r2] = sel_values(g, jnp.maximum(i - 1, 0), p_ref[g, nxt, :, h2])
                for r2, h2 in enumerate(heads):
                    s_ref[g, nxt, :, h2] = sel_scores(g, i + 1, r2)
            s = s_ref[g, cur_slot, :, hs]
            if diag:
                s = s + band_ref[0]
            m = st_ref[g, ROW_M:ROW_M + 1, hs]
            m_new = jnp.maximum(m, jnp.max(s, axis=0, keepdims=True))
            p_ref[g, cur_slot, :, hs] = jnp.exp2(s - m_new).astype(BF16)
            if mxu_first:
                pv_prev = pvs[g, r]
            else:
                pv_prev = sel_values(g, jnp.maximum(i - 1, 0), p_ref[g, nxt, :, hs])
                if not diag:
                    s_ref[g, nxt, :, hs] = sel_scores(g, i + 1, r)
            acc_ref[g, :, hs] = st_ref[g, ROW_A:ROW_A + 1, hs] * acc_ref[g, :, hs] + pv_prev
            st_ref[g, ROW_M:ROW_M + 1, hs] = m_new
            st_ref[g, ROW_A:ROW_A + 1, hs] = jnp.exp2(m - m_new)

    def sel_finish(i, cur_slot):
        for hs in heads:
            for g in groups:
                acc_ref[g, :, hs] = (st_ref[g, ROW_A:ROW_A + 1, hs] * acc_ref[g, :, hs]
                                     + sel_values(g, i, p_ref[g, cur_slot, :, hs]))

    for g in groups:
        for r, hs in enumerate(heads):
            s_ref[g, 0, :, hs] = sel_scores(g, 0, r)
        p_ref[g, 1] = jnp.zeros(p_ref.shape[2:], BF16)
        acc_ref[g] = jnp.zeros(acc_ref.shape[1:], F32)
        st_ref[g, ROW_M:ROW_M + 1, :] = jnp.full((1, R * TQ), NEG, F32)
        st_ref[g, ROW_A:ROW_A + 1, :] = jnp.ones((1, R * TQ), F32)

    acc_w = []
    for g in groups:
        m_w = functools.reduce(jnp.maximum,
                               [jnp.max(s_w, axis=0, keepdims=True) for s_w in w_s[g]])
        acc_w.append(functools.reduce(jnp.add, [
            jnp.dot(vwt_ref[g * V_ROWS:(g + 1) * V_ROWS, pl.ds(w_off[w], KB)],
                    jnp.exp2(w_s[g][w] - m_w).astype(BF16), preferred_element_type=F32)
            for w in range(n_wb)]))

    def sel_pair(j, c):
        sel_stage(2 * j, 0, diag=False)
        sel_stage(2 * j + 1, 1, diag=False, mxu_first=True)
        return c

    lax.fori_loop(0, lax.shift_right_logical(qi, 1), sel_pair, 0)
    assert NSA_TILES % 2 == 0
    if sub % 2 == 1:
        sel_stage(qi - 1, 0, diag=False)
        sel_stage(qi, 1, diag=True)
        sel_finish(qi, 1)
    else:
        sel_stage(qi, 0, diag=True)
        sel_finish(qi, 0)

    gl = jax.nn.sigmoid(glt_ref[:, tile])
    y_nsa = []
    for g in groups:
        res = []
        for r, hs in enumerate(heads):
            a_s = acc_ref[g, :, hs]
            a_w = acc_w[g][:, hs]
            row0 = g * GL_ROWS + r * N_BRANCH
            gate = [gl[row0 + br:row0 + br + 1, :] for br in range(N_BRANCH)]
            res.append(gate[0] * o_c[g][:, hs]
                       + (gate[1] / a_s[d:d + 1, :]) * a_s[0:d, :]
                       + (gate[2] / a_w[d:d + 1, :]) * a_w[0:d, :])
        ng = ng_ref[tile, g * R * d:(g + 1) * R * d]
        y_nsa.append((jnp.concatenate(res, axis=0).T * (ng * jax.nn.sigmoid(ng))).astype(BF16))

    z = part_ref[...] + jnp.dot(jnp.concatenate(y_nsa, axis=1), wo_ref[o1:o2, :],
                                preferred_element_type=F32)
    mu = jnp.mean(z, axis=-1, keepdims=True)
    zc = z - mu
    var = jnp.mean(zc * zc, axis=-1, keepdims=True)
    out_ref[tile, :] = zc * lax.rsqrt(var + LN_EPS) * lng_ref[...] + lnb_ref[...]


def _nsa_attention(qt, kc, vct, ks, vst, kw, vwt, glt, ng, x, y_ret, y_lru, layer, w_out, ln_g,
                   ln_b):
    B, _, S = qt.shape
    G, R, d, TQ = NSA_KV_HEADS, NSA_GROUP, HEAD_DIM, NSA_TQ
    selmap_t, band, cmpmask = _nsa_tables(S)
    n_sel = S // SEL_LEN
    assert NSA_KB == TQ and WINDOW % NSA_KB == 0
    assert n_sel == d
    per_b = lambda *shape: pl.BlockSpec((None,) + shape, lambda b, q: (b,) + (0,) * len(shape))
    fixed = lambda a: pl.BlockSpec(a.shape, lambda b, q: (0,) * a.ndim)
    step = NSA_TILES * TQ
    rows = lambda w: pl.BlockSpec((None, step, w), lambda b, q: (b, q, 0))
    return pl.pallas_call(
        _nsa_kernel,
        grid=(B, S // step),
        in_specs=[pl.BlockSpec((None, G * R * d, step), lambda b, q: (b, 0, q)),
                  per_b(_CMP_ROWS, G * d), per_b(G * d, _CMP_ROWS),
                  per_b(G, S, 2 * d), per_b(G * V_ROWS, S), per_b(S, G * d), per_b(G * V_ROWS, S),
                  pl.BlockSpec((None, G * GL_ROWS, step), lambda b, q: (b, 0, q)),
                  rows(G * R * d), fixed(selmap_t), fixed(band), fixed(cmpmask),
                  rows(D_MODEL), rows(RET_WIDTH), rows(LRU_WIDTH),
                  pl.BlockSpec((None, D_MODEL, D_MODEL), lambda b, q: (layer, 0, 0)),
                  pl.BlockSpec((None, 1, D_MODEL), lambda b, q: (layer, 0, 0)),
                  pl.BlockSpec((None, 1, D_MODEL), lambda b, q: (layer, 0, 0))],
        out_specs=rows(D_MODEL),
        out_shape=jax.ShapeDtypeStruct((B, S, D_MODEL), F32),
        scratch_shapes=[pltpu.VMEM((G, n_sel, TQ), F32), pltpu.VMEM((G, n_sel, TQ), F32),
                        pltpu.VMEM((G, 2, NSA_KB, R * TQ), F32),
                        pltpu.VMEM((G, 2, NSA_KB, R * TQ), BF16),
                        pltpu.VMEM((G, V_ROWS, R * TQ), F32),
                        pltpu.VMEM((G, SUBLANES, R * TQ), F32),
                        pltpu.VMEM((TQ, D_MODEL), F32)],
        compiler_params=_cparams(("parallel", "arbitrary")),
        name="nsa_attention_out",
    )(qt, kc, vct, ks, vst, kw, vwt, glt, ng, jnp.asarray(selmap_t, dtype=BF16), jnp.asarray(band),
      jnp.asarray(cmpmask), x, y_ret, y_lru, w_out, ln_g, ln_b)


def _lru_tile(lru, cw_ref, cb_ref, wa_ref, ba_ref, wx_ref, bx_ref, lam_ref, xext_ref, h_ref):
    ts, W, pad = LRU_TS, LRU_WIDTH, SUBLANES
    x = lru[:, 0:W]
    gate = lru[:, W:2 * W]
    xext_ref[pad:pad + ts, :] = x
    first = pad - (CONV_WIDTH - 1)
    xc = cb_ref[...] + cw_ref[0:1, :] * xext_ref[first:first + ts, :]
    for w in range(1, CONV_WIDTH):
        xc = xc + cw_ref[w:w + 1, :] * xext_ref[first + w:first + w + ts, :]
    xext_ref[0:pad, :] = x[ts - pad:ts, :]
    xcb = xc.astype(BF16)
    r = jax.nn.sigmoid(jnp.dot(xcb, wa_ref[...], preferred_element_type=F32) + ba_ref[...])
    gi = jax.nn.sigmoid(jnp.dot(xcb, wx_ref[...], preferred_element_type=F32) + bx_ref[...])
    nl = -lam_ref[...]
    softplus = jnp.maximum(nl, 0.0) + jnp.log1p(jnp.exp(-jnp.abs(nl)))
    log_a = (-LRU_C) * r * softplus
    a = jnp.exp(log_a)
    u = jnp.sqrt(-jnp.tanh(log_a) * (a * a + 1.0)) * (gi * xc)
    row = lax.broadcasted_iota(jnp.int32, (ts, 1), 0)
    step = 1
    while step < ts:
        if step < SUBLANES:
            keep = row >= step
            a_sh = jnp.where(keep, pltpu.roll(a, step, 0), 1.0)
            u_sh = jnp.where(keep, pltpu.roll(u, step, 0), 0.0)
        else:
            a_sh = jnp.concatenate([jnp.ones((step, W), F32), a[:ts - step, :]], axis=0)
            u_sh = jnp.concatenate([jnp.zeros((step, W), F32), u[:ts - step, :]], axis=0)
        u = a * u_sh + u
        a = a * a_sh
        step *= 2
    h = u + a * h_ref[0:1, :]
    h_ref[0:1, :] = h[ts - 1:ts, :]
    return h * (gate * jax.nn.sigmoid(gate))


def _block_diag(w):
    L, n, d, _ = w.shape
    z = jnp.zeros((L, d, d), w.dtype)
    return jnp.concatenate(
        [jnp.concatenate([w[:, i] if j == i else z for j in range(n)], axis=-1)
         for i in range(n)], axis=-2)


def kernel(x, w_in, w_out, ln_g, ln_b, nsa_cmp_pos, nsa_cmp_w, lru_conv_w, lru_conv_b,
           lru_w_a, lru_b_a, lru_w_x, lru_b_x, lru_lambda):
    assert x.shape[1] % max(PROJ_TM, RET_CHUNK, LRU_TS, NSA_TILES * NSA_TQ) == 0
    assert x.shape[1] // CMP_STRIDE == _CMP_ROWS
    row = lambda a: a[:, None, :]
    wn, wt = _prep_in_weights(w_in)
    prow, wl = _prep_cmp_weights(nsa_cmp_pos, nsa_cmp_w)
    wa, wx = _block_diag(lru_w_a).astype(BF16), _block_diag(lru_w_x).astype(BF16)
    wo = w_out.astype(BF16)
    for layer in range(DEPTH):
        y_ret, y_lru, kcn, vcn, ks, kw, ng, qt, vst, vwt, glt = _project_and_mix(
            x, layer, wn, wt, lru_conv_w, row(lru_conv_b), wa, row(lru_b_a), wx, row(lru_b_x),
            row(lru_lambda))
        kc, vct = _compress(kcn, vcn, layer, prow, wl)
        x = _nsa_attention(qt, kc, vct, ks, vst, kw, vwt, glt, ng, x, y_ret, y_lru, layer, wo,
                           row(ln_g), row(ln_b))
    return x
```
